```python
import jax, jax.numpy as jnp
from jax import lax
import numpy as np

D_MODEL = 1024
BATCH = 8
SEQ = 2048
DEPTH = 4
DEC_BATCH = 8
DEC_SEQ = 4096
PAST_LEN = 128

HEAD_DIM = 64
A_HEADS = 6
A_PAIRS = ((128, 1), (512, 4), (2048, 16))
B_HEADS = 4
B_KV_HEADS = 2
B_HALF_WINDOW = 128
C_HEADS = 6
C_KV_HEADS = 2
C_BLOCK = 128
MIX_WIDTH = (A_HEADS + B_HEADS + C_HEADS) * HEAD_DIM
IN_WIDTH = (3 * A_HEADS + B_HEADS + 2 * B_KV_HEADS + C_HEADS + 2 * C_KV_HEADS) * HEAD_DIM
GRID_W = 64
ROPE_THETA = 10000.0
N_EXPERTS = 16
CAPACITY_FACTOR = 2
D_FF = 2816
EPS = 1e-6
NEG_INF = -1e30

kernel_name = "hymba_dilated_sink_axial_ec_encoder"


def rms_norm(x, g):
    xf = x.astype(jnp.float32)
    y = xf * lax.rsqrt(jnp.mean(xf * xf, axis=-1, keepdims=True) + EPS)
    return (y * g.astype(jnp.float32)).astype(x.dtype)


def rope_angles(pos, dim):
    inv = ROPE_THETA ** (-jnp.arange(0, dim, 2, dtype=jnp.float32) / dim)
    return pos[:, None] * inv[None, :]


def apply_rope(x, ang):
    half = x.shape[-1] // 2
    cos = jnp.cos(ang)[None, :, None, :]
    sin = jnp.sin(ang)[None, :, None, :]
    xf = x.astype(jnp.float32)
    x1, x2 = xf[..., :half], xf[..., half:]
    return jnp.concatenate([x1 * cos - x2 * sin, x2 * cos + x1 * sin], axis=-1).astype(x.dtype)


def apply_axial_rope(x, ang_row, ang_col):
    half = x.shape[-1] // 2
    return jnp.concatenate([apply_rope(x[..., :half], ang_row), apply_rope(x[..., half:], ang_col)], axis=-1)


def banded_attention(q, k, v, half_window, sink=None):
    B, L, Hq, dh = q.shape
    Hkv = k.shape[2]
    G = Hq // Hkv
    W = half_window
    nblk = -(-L // W)
    Lp = nblk * W
    pad = Lp - L
    qp = jnp.pad(q, ((0, 0), (0, pad), (0, 0), (0, 0)))
    kp = jnp.pad(k, ((0, 0), (W, pad + W), (0, 0), (0, 0)))
    vp = jnp.pad(v, ((0, 0), (W, pad + W), (0, 0), (0, 0)))
    qb = qp.reshape(B, nblk, W, Hkv, G, dh)
    kb = kp.reshape(B, nblk + 2, W, Hkv, dh)
    vb = vp.reshape(B, nblk + 2, W, Hkv, dh)
    kw = jnp.concatenate([kb[:, :-2], kb[:, 1:-1], kb[:, 2:]], axis=2)
    vw = jnp.concatenate([vb[:, :-2], vb[:, 1:-1], vb[:, 2:]], axis=2)
    s = jnp.einsum('bnqhgd,bnkhd->bnhgqk', qb, kw, preferred_element_type=jnp.float32) * (dh ** -0.5)
    qpos = jnp.arange(Lp).reshape(nblk, W)
    kpos = jnp.arange(nblk)[:, None] * W - W + jnp.arange(3 * W)[None, :]
    rel = kpos[:, None, :] - qpos[:, :, None]
    valid = (jnp.abs(rel) <= W) & (kpos[:, None, :] >= 0) & (kpos[:, None, :] < L)
    s = jnp.where(valid[None, :, None, None], s, NEG_INF)
    m = jnp.max(s, axis=-1, keepdims=True)
    if sink is not None:
        sk = sink.astype(jnp.float32).reshape(Hkv, G)[None, None, :, :, None, None]
        m = jnp.maximum(m, sk)
    p = jnp.exp(s - m)
    den = jnp.sum(p, axis=-1, keepdims=True)
    if sink is not None:
        den = den + jnp.exp(sk - m)
    o = jnp.einsum('bnhgqk,bnkhd->bnqhgd', (p / den).astype(v.dtype), vw)
    o = o.reshape(B, Lp, Hq, dh)[:, :L]
    lse = (m + jnp.log(den))[..., 0]
    lse = lse.transpose(0, 1, 4, 2, 3).reshape(B, Lp, Hq)[:, :L]
    return o, lse


def dilated_attention(q, k, v, dilation, steps):
    B, L, H, dh = q.shape
    d = dilation
    Ls = L // d

    def split(t):
        return t.reshape(B, Ls, d, H, dh).transpose(0, 2, 1, 3, 4).reshape(B * d, Ls, H, dh)

    o, lse = banded_attention(split(q), split(k), split(v), steps)
    o = o.reshape(B, d, Ls, H, dh).transpose(0, 2, 1, 3, 4).reshape(B, L, H, dh)
    lse = lse.reshape(B, d, Ls, H).transpose(0, 2, 1, 3).reshape(B, L, H)
    return o, lse


def mixer_a(q, k, v):
    outs, lses = [], []
    for window, dil in A_PAIRS:
        o, lse = dilated_attention(q, k, v, dil, window // 2 // dil)
        outs.append(o)
        lses.append(lse)
    w = jax.nn.softmax(jnp.stack(lses, axis=0), axis=0)
    o = jnp.sum(w[..., None] * jnp.stack(outs, axis=0).astype(jnp.float32), axis=0)
    return o.astype(q.dtype)


def blocked_dense_attention(q, k, v):
    B, L, Hq, dh = q.shape
    Hkv = k.shape[2]
    G = Hq // Hkv
    nb = L // C_BLOCK
    qb = q.reshape(B, nb, C_BLOCK, Hkv, G, dh).transpose(1, 0, 2, 3, 4, 5)

    def one_block(qblk):
        s = jnp.einsum('bqhgd,bkhd->bhgqk', qblk, k, preferred_element_type=jnp.float32) * (dh ** -0.5)
        p = jax.nn.softmax(s, axis=-1)
        return jnp.einsum('bhgqk,bkhd->bqhgd', p.astype(v.dtype), v)

    o = lax.map(one_block, qb)
    return o.transpose(1, 0, 2, 3, 4, 5).reshape(B, L, Hq, dh)


def split_projection(proj):
    widths = [A_HEADS * HEAD_DIM] * 3 + [B_HEADS * HEAD_DIM, B_KV_HEADS * HEAD_DIM, B_KV_HEADS * HEAD_DIM] + [C_HEADS * HEAD_DIM, C_KV_HEADS * HEAD_DIM, C_KV_HEADS * HEAD_DIM]
    points = np.cumsum(widths)[:-1].tolist()
    return jnp.split(proj, points, axis=-1)


def mixer_layer(h, ang_1d, ang_row, ang_col, w_in, g_q_c, g_k_c, sink_b, g_out_a, g_out_b, g_out_c, w_out):
    B, L, _ = h.shape
    proj = jnp.einsum('bld,de->ble', h, w_in)
    qa, ka, va, qb, kb, vb, qc, kc, vc = split_projection(proj)
    heads = lambda t, n: t.reshape(B, L, n, HEAD_DIM)
    qa = apply_rope(heads(qa, A_HEADS), ang_1d)
    ka = apply_rope(heads(ka, A_HEADS), ang_1d)
    oa = mixer_a(qa, ka, heads(va, A_HEADS))
    qb = apply_rope(heads(qb, B_HEADS), ang_1d)
    kb = apply_rope(heads(kb, B_KV_HEADS), ang_1d)
    ob, _ = banded_attention(qb, kb, heads(vb, B_KV_HEADS), B_HALF_WINDOW, sink=sink_b)
    qc = apply_axial_rope(rms_norm(heads(qc, C_HEADS), g_q_c), ang_row, ang_col)
    kc = apply_axial_rope(rms_norm(heads(kc, C_KV_HEADS), g_k_c), ang_row, ang_col)
    oc = blocked_dense_attention(qc, kc, heads(vc, C_KV_HEADS))
    merged = jnp.concatenate([
        rms_norm(oa.reshape(B, L, -1), g_out_a),
        rms_norm(ob.reshape(B, L, -1), g_out_b),
        rms_norm(oc.reshape(B, L, -1), g_out_c)], axis=-1)
    return jnp.einsum('ble,ed->bld', merged, w_out)


def expert_choice_ffn(h, w_router, w_gate, w_up, w_down):
    B, L, D = h.shape
    n_tok = B * L
    cap = CAPACITY_FACTOR * n_tok // N_EXPERTS
    xt = h.reshape(n_tok, D)
    logits = jnp.einsum('nd,de->ne', xt, w_router).astype(jnp.float32)
    aff = jax.nn.softmax(logits, axis=-1)
    gate, idx = lax.top_k(aff.T, cap)
    xe = xt[idx]
    hg = jnp.einsum('ecd,edf->ecf', xe, w_gate)
    hu = jnp.einsum('ecd,edf->ecf', xe, w_up)
    ye = jnp.einsum('ecf,efd->ecd', jax.nn.silu(hg) * hu, w_down) * gate[..., None].astype(xt.dtype)
    y = jnp.zeros_like(xt).at[idx.reshape(-1)].add(ye.reshape(-1, D))
    return y.reshape(B, L, D)


def trunk(x, g_attn, w_in, g_q_c, g_k_c, sink_b, g_out_a, g_out_b, g_out_c, w_out, g_ffn, w_router, w_gate, w_up, w_down, g_final):
    L = x.shape[1]
    rows = L // GRID_W
    pos = jnp.arange(L, dtype=jnp.float32)
    row_idx = jnp.repeat(jnp.arange(rows, dtype=jnp.float32), GRID_W)
    col_idx = jnp.tile(jnp.arange(GRID_W, dtype=jnp.float32), rows)
    ang_1d = rope_angles(pos, HEAD_DIM)
    ang_row = rope_angles(row_idx, HEAD_DIM // 2)
    ang_col = rope_angles(col_idx, HEAD_DIM // 2)
    for l in range(DEPTH):
        x = x + mixer_layer(rms_norm(x, g_attn[l]), ang_1d, ang_row, ang_col, w_in[l], g_q_c[l], g_k_c[l], sink_b[l], g_out_a[l], g_out_b[l], g_out_c[l], w_out[l])
        x = x + expert_choice_ffn(rms_norm(x, g_ffn[l]), w_router[l], w_gate[l], w_up[l], w_down[l])
    return rms_norm(x, g_final)


def setup_inputs(seed: int = 0) -> dict:
    key = jax.random.key(seed)
    ks = jax.random.split(key, 20)
    f32 = jnp.float32
    nrm = lambda k, shape, scale: jax.random.normal(k, shape, f32) * scale
    gain = lambda k, shape: 1.0 + 0.01 * jax.random.normal(k, shape, f32)
    return {
        "x_prompt": jax.random.normal(ks[0], (BATCH, SEQ, D_MODEL), f32),
        "x_sample": jax.random.normal(ks[1], (DEC_BATCH, DEC_SEQ, D_MODEL), f32),
        "g_attn": gain(ks[2], (DEPTH, D_MODEL)),
        "w_in": nrm(ks[3], (DEPTH, D_MODEL, IN_WIDTH), D_MODEL ** -0.5),
        "g_q_c": gain(ks[4], (DEPTH, HEAD_DIM)),
        "g_k_c": gain(ks[5], (DEPTH, HEAD_DIM)),
        "sink_b": nrm(ks[6], (DEPTH, B_HEADS), 0.5),
        "g_out_a": gain(ks[7], (DEPTH, A_HEADS * HEAD_DIM)),
        "g_out_b": gain(ks[8], (DEPTH, B_HEADS * HEAD_DIM)),
        "g_out_c": gain(ks[9], (DEPTH, C_HEADS * HEAD_DIM)),
        "w_out": nrm(ks[10], (DEPTH, MIX_WIDTH, D_MODEL), (MIX_WIDTH * 2 * DEPTH) ** -0.5),
        "g_ffn": gain(ks[11], (DEPTH, D_MODEL)),
        "w_router": nrm(ks[12], (DEPTH, D_MODEL, N_EXPERTS), D_MODEL ** -0.5),
        "w_gate": nrm(ks[13], (DEPTH, N_EXPERTS, D_MODEL, D_FF), D_MODEL ** -0.5),
        "w_up": nrm(ks[14], (DEPTH, N_EXPERTS, D_MODEL, D_FF), D_MODEL ** -0.5),
        "w_down": nrm(ks[15], (DEPTH, N_EXPERTS, D_FF, D_MODEL), D_FF ** -0.5),
        "g_final": gain(ks[16], (D_MODEL,)),
    }


def reference(x_prompt, x_sample, g_attn, w_in, g_q_c, g_k_c, sink_b, g_out_a, g_out_b, g_out_c, w_out, g_ffn, w_router, w_gate, w_up, w_down, g_final):
    y_prompt = trunk(x_prompt, g_attn, w_in, g_q_c, g_k_c, sink_b, g_out_a, g_out_b, g_out_c, w_out, g_ffn, w_router, w_gate, w_up, w_down, g_final)
    y_sample = trunk(x_sample, g_attn, w_in, g_q_c, g_k_c, sink_b, g_out_a, g_out_b, g_out_c, w_out, g_ffn, w_router, w_gate, w_up, w_down, g_final)
    return (y_prompt, y_sample)
```

```python
import functools

import jax
import jax.numpy as jnp
import numpy as np
from jax import lax
from jax.experimental import pallas as pl
from jax.experimental.pallas import tpu as pltpu

D_MODEL = 1024
DEPTH = 4
HEAD_DIM = 64
A_HEADS = 6
A_PAIRS = ((128, 1), (512, 4), (2048, 16))
B_HEADS = 4
B_KV_HEADS = 2
B_HALF_WINDOW = 128
C_HEADS = 6
C_KV_HEADS = 2
GRID_W = 64
ROPE_THETA = 10000.0
N_EXPERTS = 16
CAPACITY_FACTOR = 2
D_FF = 2816
EPS = 1e-6
NEG_INF = -1e30

LANES = 128
A_WIDTH = A_HEADS * HEAD_DIM
B_WIDTH = B_HEADS * HEAD_DIM
C_WIDTH = C_HEADS * HEAD_DIM
IN_WIDTH = 3 * A_WIDTH + B_WIDTH + 2 * B_KV_HEADS * HEAD_DIM + C_WIDTH + 2 * C_KV_HEADS * HEAD_DIM
N_GROUPS = IN_WIDTH // LANES
QA, KA, VA, QB, KB, VB, QC, KC, VC = 0, 3, 6, 9, 11, 12, 13, 16, 17
ROPE_NONE, ROPE_1D, ROPE_AXIAL_Q, ROPE_AXIAL_K = 0, 1, 2, 3
GROUP_KIND = ([(ROPE_1D, True)] * 3 + [(ROPE_1D, False)] * 3 + [(ROPE_NONE, False)] * 3
              + [(ROPE_1D, True)] * 2 + [(ROPE_1D, False)] + [(ROPE_NONE, False)]
              + [(ROPE_AXIAL_Q, True)] * 3 + [(ROPE_AXIAL_K, False)] + [(ROPE_NONE, False)])
Q_SCALE = HEAD_DIM ** -0.5

VMEM_LIMIT = 56 * 1024 * 1024

B_Q_ORDER = (0, 2, 1, 3)
C_Q_ORDER = (0, 3, 1, 4, 2, 5)


def _head_perm(order):
    return np.concatenate([np.arange(h * HEAD_DIM, (h + 1) * HEAD_DIM) for h in order])


def _in_perm():
    widths = [A_WIDTH] * 3 + [B_WIDTH, 128, 128, C_WIDTH, 128, 128]
    offs = np.concatenate([[0], np.cumsum(widths)])
    parts = [np.arange(offs[i], offs[i + 1]) for i in range(9)]
    parts[3] = offs[3] + _head_perm(B_Q_ORDER)
    parts[6] = offs[6] + _head_perm(C_Q_ORDER)
    return np.concatenate(parts)


def _out_perms():
    pb = _head_perm(B_Q_ORDER)
    pc = _head_perm(C_Q_ORDER)
    return pb, pc, np.concatenate([np.arange(A_WIDTH), A_WIDTH + pb, A_WIDTH + B_WIDTH + pc])


def _rope_tables(seq):
    pos = jnp.arange(seq, dtype=jnp.float32)
    inv1 = ROPE_THETA ** (-jnp.arange(0, HEAD_DIM, 2, dtype=jnp.float32) / HEAD_DIM)
    ang = pos[:, None] * inv1[None, :]
    c, s = jnp.cos(ang), jnp.sin(ang)
    cos1 = jnp.tile(jnp.concatenate([c, c], -1), (1, 2))
    sin1 = jnp.tile(jnp.concatenate([-s, s], -1), (1, 2))
    half = HEAD_DIM // 2
    inv2 = ROPE_THETA ** (-jnp.arange(0, half, 2, dtype=jnp.float32) / half)
    row = jnp.floor(pos / GRID_W)
    col = pos - row * GRID_W
    ar, ac = row[:, None] * inv2[None, :], col[:, None] * inv2[None, :]
    cr, sr, cc, sc = jnp.cos(ar), jnp.sin(ar), jnp.cos(ac), jnp.sin(ac)
    cos2 = jnp.tile(jnp.concatenate([cr, cr, cc, cc], -1), (1, 2))
    sin2 = jnp.tile(jnp.concatenate([-sr, sr, -sc, sc], -1), (1, 2))
    return cos1, sin1, cos2, sin2


def _swap_halves(x, block):
    half = block // 2
    lane = lax.broadcasted_iota(jnp.int32, x.shape, 1)
    return jnp.where(lane % block < half, pltpu.roll(x, LANES - half, 1), pltpu.roll(x, half, 1))


def _inproj_kernel(x_ref, g_ref, w_ref, cos1_ref, sin1_ref, cos2_ref, sin2_ref, gq_ref, gk_ref, seg_ref, o_ref):
    x = x_ref[...]
    y = x * lax.rsqrt(jnp.mean(x * x, axis=-1, keepdims=True) + EPS)
    h = (y * g_ref[...]).astype(jnp.bfloat16)
    for c in range(N_GROUPS // 2):
        acc = jnp.dot(h, w_ref[:, c * 2 * LANES:(c + 1) * 2 * LANES], preferred_element_type=jnp.float32)
        for half in range(2):
            grp = 2 * c + half
            a = acc[:, half * LANES:(half + 1) * LANES]
            kind, is_q = GROUP_KIND[grp]
            if kind == ROPE_1D:
                a = a * cos1_ref[...] + _swap_halves(a, HEAD_DIM) * sin1_ref[...]
            elif kind in (ROPE_AXIAL_Q, ROPE_AXIAL_K):
                gain = gq_ref[...] if kind == ROPE_AXIAL_Q else gk_ref[...]
                ss = jnp.dot(a * a, seg_ref[...], preferred_element_type=jnp.float32, precision=lax.Precision.HIGHEST)
                a = a * lax.rsqrt(ss * (1.0 / HEAD_DIM) + EPS) * gain
                a = a * cos2_ref[...] + _swap_halves(a, HEAD_DIM // 2) * sin2_ref[...]
            if is_q:
                a = a * Q_SCALE
            o_ref[:, grp * LANES:(grp + 1) * LANES] = a.astype(jnp.bfloat16)


def _in_projection(x, g, w, tables, gq, gk, seg, seqs, tm):
    n = x.shape[0]
    (n0, l0), (n1, l1) = seqs
    t0 = n0 // tm

    def tab_map(i):
        return (jnp.where(i < t0, i % (l0 // tm), (i - t0) % (l1 // tm)), 0)

    tab_spec = pl.BlockSpec((tm, LANES), tab_map)
    const = lambda shape: pl.BlockSpec(shape, lambda i: (0, 0))
    return pl.pallas_call(
        _inproj_kernel,
        grid=(n // tm,),
        in_specs=[pl.BlockSpec((tm, D_MODEL), lambda i: (i, 0)), const((1, D_MODEL)), const((D_MODEL, IN_WIDTH)),
                  tab_spec, tab_spec, tab_spec, tab_spec, const((1, LANES)), const((1, LANES)), const((LANES, LANES))],
        out_specs=pl.BlockSpec((tm, IN_WIDTH), lambda i: (i, 0)),
        out_shape=jax.ShapeDtypeStruct((n, IN_WIDTH), jnp.bfloat16),
        compiler_params=pltpu.CompilerParams(dimension_semantics=("arbitrary",), vmem_limit_bytes=VMEM_LIMIT),
        name="in_projection",
    )(x, g, w, *tables, gq, gk, seg)


def _stack_heads(q):
    lane = lax.broadcasted_iota(jnp.int32, q.shape, 1)
    zero = jnp.zeros_like(q)
    return jnp.concatenate([jnp.where(lane < HEAD_DIM, q, zero), jnp.where(lane >= HEAD_DIM, q, zero)], axis=0)


def _unstack_heads(x, tq):
    lane = lax.broadcasted_iota(jnp.int32, (tq, x.shape[1]), 1)
    return jnp.where(lane < HEAD_DIM, x[:tq], x[tq:])


def _banded_kernel(sink_ref, q_ref, k_ref, v_ref, o_ref, *lse_refs, tq, half_window, win, seq, use_sink):
    g = pl.program_id(2)
    i = pl.program_id(3)
    start = jnp.clip(i * tq - half_window, 0, seq - win)
    start = pl.multiple_of(start, 64)
    kw = k_ref[0, pl.ds(start, win), :]
    vw = v_ref[0, pl.ds(start, win), :]
    s = lax.dot_general(_stack_heads(q_ref[0]), kw, (((1,), (1,)), ((), ())), preferred_element_type=jnp.float32)
    row = lax.broadcasted_iota(jnp.int32, s.shape, 0)
    qpos = i * tq + jnp.where(row < tq, row, row - tq)
    kpos = start + lax.broadcasted_iota(jnp.int32, s.shape, 1)
    s = jnp.where(jnp.abs(qpos - kpos) <= half_window, s, NEG_INF)
    m = jnp.max(s, axis=-1, keepdims=True)
    if use_sink:
        row1 = lax.broadcasted_iota(jnp.int32, m.shape, 0)
        sk = jnp.where(row1 < tq, sink_ref[g], sink_ref[g + B_KV_HEADS])
        m = jnp.maximum(m, sk)
    p = jnp.exp(s - m)
    den = jnp.sum(p, axis=-1, keepdims=True)
    if use_sink:
        den = den + jnp.exp(sk - m)
    pn = (p * (1.0 / den)).astype(jnp.bfloat16)
    o = jnp.dot(pn, vw, preferred_element_type=jnp.float32)
    o_ref[0] = _unstack_heads(o, tq).astype(o_ref.dtype)
    if lse_refs:
        lse = m + jnp.log(den)
        lse_refs[0][0] = _unstack_heads(jnp.broadcast_to(lse, (2 * tq, LANES)), tq)


def _banded_attention(proj, sink, *, batch, seq, dilation, half_window, q_off, n_q_groups, k_off, v_off, kv_per_group,
                      with_lse, tq=128):
    ls = seq // dilation
    tq = min(tq, ls)
    win = min(tq + 2 * half_window, ls)
    view = proj.reshape(batch, ls, dilation * IN_WIDTH)
    width = n_q_groups * LANES
    kg = (lambda g: g) if kv_per_group else (lambda g: 0)
    kernel = functools.partial(_banded_kernel, tq=tq, half_window=half_window, win=win, seq=ls, use_sink=sink is not None)
    out_shape = [jax.ShapeDtypeStruct((batch, ls, dilation * width), jnp.bfloat16)]
    out_spec = pl.BlockSpec((1, tq, LANES), lambda b, r, g, i, *_: (b, i, r * n_q_groups + g))
    out_specs = [out_spec]
    if with_lse:
        out_shape.append(jax.ShapeDtypeStruct((batch, ls, dilation * width), jnp.float32))
        out_specs.append(out_spec)
    if sink is None:
        sink = jnp.zeros((B_HEADS,), jnp.float32)
    outs = pl.pallas_call(
        kernel,
        grid_spec=pltpu.PrefetchScalarGridSpec(
            num_scalar_prefetch=0,
            grid=(batch, dilation, n_q_groups, ls // tq),
            in_specs=[
                pl.BlockSpec(memory_space=pltpu.SMEM),
                pl.BlockSpec((1, tq, LANES), lambda b, r, g, i: (b, i, r * N_GROUPS + q_off + g)),
                pl.BlockSpec((1, ls, LANES), lambda b, r, g, i: (b, 0, r * N_GROUPS + k_off + kg(g))),
                pl.BlockSpec((1, ls, LANES), lambda b, r, g, i: (b, 0, r * N_GROUPS + v_off + kg(g))),
            ],
            out_specs=out_specs,
        ),
        out_shape=out_shape,
        compiler_params=pltpu.CompilerParams(dimension_semantics=("arbitrary",) * 4, vmem_limit_bytes=VMEM_LIMIT),
        name=f"banded_attention_d{dilation}_w{half_window}",
    )(sink, view, view, view)
    return [o.reshape(batch * seq, width) for o in outs]


def _dense_kernel(q_ref, k_ref, v_ref, o_ref, *, tq):
    s = lax.dot_general(_stack_heads(q_ref[0]), k_ref[0], (((1,), (1,)), ((), ())), preferred_element_type=jnp.float32)
    m = jnp.max(s, axis=-1, keepdims=True)
    p = jnp.exp(s - m)
    den = jnp.sum(p, axis=-1, keepdims=True)
    o = jnp.dot(p.astype(jnp.bfloat16), v_ref[0], preferred_element_type=jnp.float32) * (1.0 / den)
    o_ref[0] = _unstack_heads(o, tq).astype(o_ref.dtype)


def _dense_attention(proj, *, batch, seq, tq=256):
    view = proj.reshape(batch, seq, IN_WIDTH)
    n_groups = C_WIDTH // LANES
    out = pl.pallas_call(
        functools.partial(_dense_kernel, tq=tq),
        grid=(batch, seq // tq, n_groups),
        in_specs=[
            pl.BlockSpec((1, tq, LANES), lambda b, i, g: (b, i, QC + g)),
            pl.BlockSpec((1, seq, LANES), lambda b, i, g: (b, 0, KC)),
            pl.BlockSpec((1, seq, LANES), lambda b, i, g: (b, 0, VC)),
        ],
        out_specs=pl.BlockSpec((1, tq, LANES), lambda b, i, g: (b, i, g)),
        out_shape=jax.ShapeDtypeStruct((batch, seq, C_WIDTH), jnp.bfloat16),
        compiler_params=pltpu.CompilerParams(dimension_semantics=("arbitrary",) * 3, vmem_limit_bytes=VMEM_LIMIT),
        name="dense_attention",
    )(view, view, view)
    return out.reshape(batch * seq, C_WIDTH)


def _rms(x, g):
    return x * lax.rsqrt(jnp.mean(x * x, axis=-1, keepdims=True) + EPS) * g


def _outproj_kernel(oa1_ref, oa4_ref, oa16_ref, l1_ref, l4_ref, l16_ref, ob_ref, oc_ref, x_ref, w_ref, ga_ref, gb_ref,
                    gc_ref, gf_ref, wr_ref, xo_ref, h_ref, aff_ref):
    l1, l4, l16 = l1_ref[...], l4_ref[...], l16_ref[...]
    mx = jnp.maximum(jnp.maximum(l1, l4), l16)
    e1, e4, e16 = jnp.exp(l1 - mx), jnp.exp(l4 - mx), jnp.exp(l16 - mx)
    inv = 1.0 / (e1 + e4 + e16)
    f32 = jnp.float32
    oa = (e1 * inv) * oa1_ref[...].astype(f32) + (e4 * inv) * oa4_ref[...].astype(f32) \
        + (e16 * inv) * oa16_ref[...].astype(f32)
    merged = jnp.concatenate([_rms(oa, ga_ref[...]), _rms(ob_ref[...].astype(f32), gb_ref[...]),
                              _rms(oc_ref[...].astype(f32), gc_ref[...])], axis=-1).astype(jnp.bfloat16)
    xn = x_ref[...] + jnp.dot(merged, w_ref[...], preferred_element_type=f32)
    xo_ref[...] = xn
    h = _rms(xn, gf_ref[...])
    h_ref[...] = h.astype(jnp.bfloat16)
    logits = lax.dot_general(wr_ref[...], h, (((1,), (1,)), ((), ())), preferred_element_type=f32,
                             precision=lax.Precision.HIGHEST)
    z = jnp.exp(logits - jnp.max(logits, axis=0, keepdims=True))
    aff_ref[...] = z / jnp.sum(z, axis=0, keepdims=True)


def _out_projection(oa, lse, ob, oc, x, w, ga, gb, gc, gf, wr_t, tm):
    n = x.shape[0]
    rows = lambda width: pl.BlockSpec((tm, width), lambda i: (i, 0))
    const = lambda shape: pl.BlockSpec(shape, lambda i: (0, 0))
    return pl.pallas_call(
        _outproj_kernel,
        grid=(n // tm,),
        in_specs=[rows(A_WIDTH)] * 6 + [rows(B_WIDTH), rows(C_WIDTH), rows(D_MODEL), const((D_MODEL, D_MODEL)),
                                        const((1, A_WIDTH)), const((1, B_WIDTH)), const((1, C_WIDTH)),
                                        const((1, D_MODEL)), const((N_EXPERTS, D_MODEL))],
        out_specs=[rows(D_MODEL), rows(D_MODEL), pl.BlockSpec((N_EXPERTS, tm), lambda i: (0, i))],
        out_shape=[jax.ShapeDtypeStruct((n, D_MODEL), jnp.float32), jax.ShapeDtypeStruct((n, D_MODEL), jnp.bfloat16),
                   jax.ShapeDtypeStruct((N_EXPERTS, n), jnp.float32)],
        compiler_params=pltpu.CompilerParams(dimension_semantics=("arbitrary",), vmem_limit_bytes=VMEM_LIMIT),
        name="out_projection",
    )(*oa, *lse, ob, oc, x, w, ga, gb, gc, gf, wr_t)


def _ffn_kernel(x_ref, wg_ref, wu_ref, wd_ref, gate_ref, o_ref):
    j = pl.program_id(2)
    x = x_ref[0]
    hg = jnp.dot(x, wg_ref[...].astype(jnp.bfloat16), preferred_element_type=jnp.float32)
    hu = jnp.dot(x, wu_ref[...].astype(jnp.bfloat16), preferred_element_type=jnp.float32)
    act = (hg * jax.nn.sigmoid(hg) * hu).astype(jnp.bfloat16)
    y = jnp.dot(act, wd_ref[...].astype(jnp.bfloat16), preferred_element_type=jnp.float32)

    @pl.when(j == 0)
    def _():
        o_ref[0] = y

    @pl.when(j > 0)
    def _():
        o_ref[0] += y

    @pl.when(j == pl.num_programs(2) - 1)
    def _():
        o_ref[0] = o_ref[0] * gate_ref[0]


def _expert_ffn(xe, gate, w_gate, w_up, w_down, layer, tr, tf):
    n_e, rows, d = xe.shape
    d_ff = w_gate.shape[-1]
    return pl.pallas_call(
        _ffn_kernel,
        grid=(n_e, rows // tr, d_ff // tf),
        in_specs=[
            pl.BlockSpec((1, tr, d), lambda e, c, j: (e, c, 0)),
            pl.BlockSpec((None, None, d, tf), lambda e, c, j: (layer, e, 0, j)),
            pl.BlockSpec((None, None, d, tf), lambda e, c, j: (layer, e, 0, j)),
            pl.BlockSpec((None, None, tf, d), lambda e, c, j: (layer, e, j, 0)),
            pl.BlockSpec((1, tr, 1), lambda e, c, j: (e, c, 0)),
        ],
        out_specs=pl.BlockSpec((1, tr, d), lambda e, c, j: (e, c, 0)),
        out_shape=jax.ShapeDtypeStruct((n_e, rows, d), jnp.float32),
        compiler_params=pltpu.CompilerParams(dimension_semantics=("arbitrary",) * 3, vmem_limit_bytes=VMEM_LIMIT),
        name="expert_ffn",
    )(xe, w_gate, w_up, w_down, gate)


def _final_norm_kernel(x_ref, g_ref, o_ref):
    o_ref[...] = _rms(x_ref[...], g_ref[...])


def _final_norm(x, g, tm):
    n = x.shape[0]
    return pl.pallas_call(
        _final_norm_kernel,
        grid=(n // tm,),
        in_specs=[pl.BlockSpec((tm, D_MODEL), lambda i: (i, 0)), pl.BlockSpec((1, D_MODEL), lambda i: (0, 0))],
        out_specs=pl.BlockSpec((tm, D_MODEL), lambda i: (i, 0)),
        out_shape=jax.ShapeDtypeStruct((n, D_MODEL), jnp.float32),
        name="final_norm",
    )(x, g)


def _mixers(proj, sink, batch, seq):
    oa, lse = [], []
    for window, dil in A_PAIRS:
        o, l = _banded_attention(proj, None, batch=batch, seq=seq, dilation=dil, half_window=window // 2 // dil,
                                 q_off=QA, n_q_groups=3, k_off=KA, v_off=VA, kv_per_group=True, with_lse=True)
        oa.append(o)
        lse.append(l)
    (ob,) = _banded_attention(proj, sink, batch=batch, seq=seq, dilation=1, half_window=B_HALF_WINDOW, q_off=QB,
                              n_q_groups=2, k_off=KB, v_off=VB, kv_per_group=False, with_lse=False)
    oc = _dense_attention(proj, batch=batch, seq=seq)
    return oa, lse, ob, oc


def _forward(x_prompt, x_sample, g_attn, w_in, g_q_c, g_k_c, sink_b, g_out_a, g_out_b, g_out_c, w_out, g_ffn, w_router,
             w_gate, w_up, w_down, g_final, *, tm, tr, tf):
    f32, bf16 = jnp.float32, jnp.bfloat16
    shapes = (x_prompt.shape[:2], x_sample.shape[:2])
    seqs = tuple((b * l, l) for b, l in shapes)
    n_tok = [n for n, _ in seqs]
    x = jnp.concatenate([x_prompt.reshape(-1, D_MODEL), x_sample.reshape(-1, D_MODEL)], axis=0)

    tables = _rope_tables(max(l for _, l in shapes))
    seg = jnp.asarray(np.kron(np.eye(LANES // HEAD_DIM), np.ones((HEAD_DIM, HEAD_DIM))), f32)
    pb, pc, perm_out = _out_perms()
    w_in_p = w_in[:, :, _in_perm()].astype(bf16)
    w_out_p = w_out[:, perm_out, :].astype(bf16)
    sink_p = sink_b[:, np.asarray(B_Q_ORDER)]
    tile2 = lambda g: jnp.tile(g, (1, 2))[:, None, :]
    gq, gk = tile2(g_q_c), tile2(g_k_c)
    wr_t = jnp.swapaxes(w_router, 1, 2)
    caps = [CAPACITY_FACTOR * n // N_EXPERTS for n in n_tok]

    for l in range(DEPTH):
        proj = _in_projection(x, g_attn[l][None], w_in_p[l], tables, gq[l], gk[l], seg, seqs, tm)
        parts = []
        off = 0
        for (b, s), n in zip(shapes, n_tok):
            parts.append(_mixers(proj[off:off + n], sink_p[l], b, s))
            off += n
        cat = lambda k: jnp.concatenate([p[k] for p in parts], axis=0)
        oa = [jnp.concatenate([p[0][d] for p in parts], axis=0) for d in range(3)]
        lse = [jnp.concatenate([p[1][d] for p in parts], axis=0) for d in range(3)]
        x, h, aff = _out_projection(oa, lse, cat(2), cat(3), x, w_out_p[l], g_out_a[l][None], g_out_b[l][pb][None],
                                    g_out_c[l][pc][None], g_ffn[l][None], wr_t[l], tm)
        gates, idxs = [], []
        off = 0
        for n, cap in zip(n_tok, caps):
            gate, idx = lax.top_k(aff[:, off:off + n], cap)
            gates.append(gate)
            idxs.append(idx + off)
            off += n
        gate = jnp.concatenate(gates, axis=1)
        idx = jnp.concatenate(idxs, axis=1)
        xe = h[idx]
        ye = _expert_ffn(xe, gate[..., None], w_gate, w_up, w_down, l, tr, tf)
        x = x.at[idx.reshape(-1)].add(ye.reshape(-1, D_MODEL))

    y = _final_norm(x, g_final[None], tm)
    return (y[:n_tok[0]].reshape(x_prompt.shape), y[n_tok[0]:].reshape(x_sample.shape))


def kernel(x_prompt, x_sample, g_attn, w_in, g_q_c, g_k_c, sink_b, g_out_a, g_out_b, g_out_c, w_out, g_ffn, w_router,
           w_gate, w_up, w_down, g_final):
    return _forward(x_prompt, x_sample, g_attn, w_in, g_q_c, g_k_c, sink_b, g_out_a, g_out_b, g_out_c, w_out, g_ffn,
                    w_router, w_gate, w_up, w_down, g_final, tm=512, tr=2048, tf=256)
```

```python
import functools

import jax
import jax.numpy as jnp
import numpy as np
from jax import lax
from jax.experimental import pallas as pl
from jax.experimental.pallas import tpu as pltpu

D_MODEL = 1024
DEPTH = 4
HEAD_DIM = 64
A_HEADS = 6
A_PAIRS = ((128, 1), (512, 4), (2048, 16))
B_HEADS = 4
B_KV_HEADS = 2
B_HALF_WINDOW = 128
C_HEADS = 6
C_KV_HEADS = 2
GRID_W = 64
ROPE_THETA = 10000.0
N_EXPERTS = 16
CAPACITY_FACTOR = 2
D_FF = 2816
EPS = 1e-6
NEG_INF = -1e30

LANES = 128
A_WIDTH = A_HEADS * HEAD_DIM
B_WIDTH = B_HEADS * HEAD_DIM
C_WIDTH = C_HEADS * HEAD_DIM
IN_WIDTH = 3 * A_WIDTH + B_WIDTH + 2 * B_KV_HEADS * HEAD_DIM + C_WIDTH + 2 * C_KV_HEADS * HEAD_DIM
N_GROUPS = IN_WIDTH // LANES
A_GROUPS = 3 * A_WIDTH // LANES
QA, KA, VA, QB, KB, VB, QC, KC, VC = 0, 3, 6, 9, 11, 12, 13, 16, 17
ROPE_NONE, ROPE_1D, ROPE_AXIAL_Q, ROPE_AXIAL_K = 0, 1, 2, 3
GROUP_KIND = ([(ROPE_1D, True)] * 3 + [(ROPE_1D, False)] * 3 + [(ROPE_NONE, False)] * 3
              + [(ROPE_1D, True)] * 2 + [(ROPE_1D, False)] + [(ROPE_NONE, False)]
              + [(ROPE_AXIAL_Q, True)] * 3 + [(ROPE_AXIAL_K, False)] + [(ROPE_NONE, False)])
Q_SCALE = HEAD_DIM ** -0.5
A_DILATIONS = tuple(d for _, d in A_PAIRS)
A_HALF_WINDOW = A_PAIRS[0][0] // 2
assert all(w // 2 // d == A_HALF_WINDOW for w, d in A_PAIRS) and A_DILATIONS == (1, 4, 16)
BAND_TQ = 128

VMEM_LIMIT = 56 * 1024 * 1024

B_Q_ORDER = (0, 2, 1, 3)
C_Q_ORDER = (0, 3, 1, 4, 2, 5)


def _head_perm(order):
    return np.concatenate([np.arange(h * HEAD_DIM, (h + 1) * HEAD_DIM) for h in order])


def _in_perm():
    widths = [A_WIDTH] * 3 + [B_WIDTH, 128, 128, C_WIDTH, 128, 128]
    offs = np.concatenate([[0], np.cumsum(widths)])
    parts = [np.arange(offs[i], offs[i + 1]) for i in range(9)]
    parts[3] = offs[3] + _head_perm(B_Q_ORDER)
    parts[6] = offs[6] + _head_perm(C_Q_ORDER)
    return np.concatenate(parts)


def _out_perms():
    pb = _head_perm(B_Q_ORDER)
    pc = _head_perm(C_Q_ORDER)
    return pb, pc, np.concatenate([np.arange(A_WIDTH), A_WIDTH + pb, A_WIDTH + B_WIDTH + pc])


def _rope_tables(seq):
    pos = jnp.arange(seq, dtype=jnp.float32)
    inv1 = ROPE_THETA ** (-jnp.arange(0, HEAD_DIM, 2, dtype=jnp.float32) / HEAD_DIM)
    ang = pos[:, None] * inv1[None, :]
    c, s = jnp.cos(ang), jnp.sin(ang)
    cos1 = jnp.tile(jnp.concatenate([c, c], -1), (1, 2))
    sin1 = jnp.tile(jnp.concatenate([-s, s], -1), (1, 2))
    half = HEAD_DIM // 2
    inv2 = ROPE_THETA ** (-jnp.arange(0, half, 2, dtype=jnp.float32) / half)
    row = jnp.floor(pos / GRID_W)
    col = pos - row * GRID_W
    ar, ac = row[:, None] * inv2[None, :], col[:, None] * inv2[None, :]
    cr, sr, cc, sc = jnp.cos(ar), jnp.sin(ar), jnp.cos(ac), jnp.sin(ac)
    cos2 = jnp.tile(jnp.concatenate([cr, cr, cc, cc], -1), (1, 2))
    sin2 = jnp.tile(jnp.concatenate([-sr, sr, -sc, sc], -1), (1, 2))
    return cos1, sin1, cos2, sin2


def _swap_halves(x, block):
    half = block // 2
    lane = lax.broadcasted_iota(jnp.int32, x.shape, 1)
    return jnp.where(lane % block < half, pltpu.roll(x, LANES - half, 1), pltpu.roll(x, half, 1))


def _inproj_kernel(x_ref, g_ref, w_ref, cos1_ref, sin1_ref, cos2_ref, sin2_ref, gq_ref, gk_ref, seg_ref, o_ref,
                   a4_ref, a16_ref, rows_ref):
    x = x_ref[...]
    tm = x.shape[0]
    y = x * lax.rsqrt(jnp.mean(x * x, axis=-1, keepdims=True) + EPS)
    h = (y * g_ref[...]).astype(jnp.bfloat16)
    for c in range(N_GROUPS // 2):
        acc = jnp.dot(h, w_ref[:, c * 2 * LANES:(c + 1) * 2 * LANES], preferred_element_type=jnp.float32)
        for half in range(2):
            grp = 2 * c + half
            cols = slice(grp * LANES, (grp + 1) * LANES)
            a = acc[:, half * LANES:(half + 1) * LANES]
            kind, is_q = GROUP_KIND[grp]
            if kind == ROPE_1D:
                a = a * cos1_ref[...] + _swap_halves(a, HEAD_DIM) * sin1_ref[...]
            elif kind in (ROPE_AXIAL_Q, ROPE_AXIAL_K):
                gain = gq_ref[...] if kind == ROPE_AXIAL_Q else gk_ref[...]
                ss = jnp.dot(a * a, seg_ref[...], preferred_element_type=jnp.float32, precision=lax.Precision.HIGHEST)
                a = a * lax.rsqrt(ss * (1.0 / HEAD_DIM) + EPS) * gain
                a = a * cos2_ref[...] + _swap_halves(a, HEAD_DIM // 2) * sin2_ref[...]
            if is_q:
                a = a * Q_SCALE
            o_ref[:, cols] = a.astype(jnp.bfloat16)
            if grp < A_GROUPS:
                rows_ref[...] = a
                for d, ref in ((4, a4_ref), (16, a16_ref)):
                    for r in range(d):
                        ref[r, :, cols] = rows_ref[pl.ds(r, tm // d, stride=d), :].astype(jnp.bfloat16)


def _in_projection(x, g, w, tables, gq, gk, seg, seqs, tm):
    n = x.shape[0]
    (n0, l0), (n1, l1) = seqs
    t0 = n0 // tm

    def tab_map(i):
        return (jnp.where(i < t0, i % (l0 // tm), (i - t0) % (l1 // tm)), 0)

    tab_spec = pl.BlockSpec((tm, LANES), tab_map)
    const = lambda shape: pl.BlockSpec(shape, lambda i: (0, 0))
    wa = A_GROUPS * LANES
    return pl.pallas_call(
        _inproj_kernel,
        grid=(n // tm,),
        in_specs=[pl.BlockSpec((tm, D_MODEL), lambda i: (i, 0)), const((1, D_MODEL)), const((D_MODEL, IN_WIDTH)),
                  tab_spec, tab_spec, tab_spec, tab_spec, const((1, LANES)), const((1, LANES)), const((LANES, LANES))],
        out_specs=[pl.BlockSpec((tm, IN_WIDTH), lambda i: (i, 0)),
                   pl.BlockSpec((4, tm // 4, wa), lambda i: (0, i, 0)),
                   pl.BlockSpec((16, tm // 16, wa), lambda i: (0, i, 0))],
        out_shape=[jax.ShapeDtypeStruct((n, IN_WIDTH), jnp.bfloat16),
                   jax.ShapeDtypeStruct((4, n // 4, wa), jnp.bfloat16),
                   jax.ShapeDtypeStruct((16, n // 16, wa), jnp.bfloat16)],
        scratch_shapes=[pltpu.VMEM((tm, LANES), jnp.float32)],
        compiler_params=pltpu.CompilerParams(dimension_semantics=("arbitrary",), vmem_limit_bytes=VMEM_LIMIT),
        name="in_projection",
    )(x, g, w, *tables, gq, gk, seg)


def _stack_heads(q):
    lane = lax.broadcasted_iota(jnp.int32, q.shape, 1)
    zero = jnp.zeros_like(q)
    return jnp.concatenate([jnp.where(lane < HEAD_DIM, q, zero), jnp.where(lane >= HEAD_DIM, q, zero)], axis=0)


def _unstack_heads(x, tq):
    lane = lax.broadcasted_iota(jnp.int32, (tq, x.shape[1]), 1)
    return jnp.where(lane < HEAD_DIM, x[:tq], x[tq:])


def _unstack_column(col, tq):
    return _unstack_heads(jnp.broadcast_to(col, (2 * tq, LANES)), tq)


def _band_bias(tq, win, half_window):
    row = np.arange(2 * tq)[:, None] % tq
    col = np.arange(win)[None, :]
    kinds = [np.where(np.abs(row + off - col) <= half_window, 0.0, NEG_INF) for off in (0, half_window, 2 * half_window)]
    return jnp.asarray(np.stack(kinds), jnp.float32)


def _band_tile(q, kw, vw, bias, sink=None):
    s = lax.dot_general(_stack_heads(q), kw, (((1,), (1,)), ((), ())), preferred_element_type=jnp.float32) + bias
    m = jnp.max(s, axis=-1, keepdims=True)
    if sink is not None:
        m = jnp.maximum(m, sink)
    p = jnp.exp(s - m)
    den = jnp.sum(p, axis=-1, keepdims=True)
    if sink is not None:
        den = den + jnp.exp(sink - m)
    num = jnp.dot(p.astype(jnp.bfloat16), vw, preferred_element_type=jnp.float32)
    return num, m, den


def _tile_window(i, n_tiles, tq, win, half_window, seq):
    start = pl.multiple_of(jnp.clip(i * tq - half_window, 0, seq - win), 64)
    kind = jnp.where(i == 0, 0, jnp.where(i == n_tiles - 1, 2, 1))
    return start, kind


def _mixer_a_kernel(bias1_ref, bias4_ref, bias16_ref, q1_ref, k1_ref, v1_ref, q4_ref, k4_ref, v4_ref, q16_ref, k16_ref, v16_ref,
                    o_ref, m_scr, l_scr, n_scr, *, seq):
    tq, hw = BAND_TQ, A_HALF_WINDOW

    def run_tile(q_ref, k_ref, v_ref, b_ref, lead, i, ls):
        tqc = min(tq, ls)
        win = min(tqc + 2 * hw, ls)
        n_tiles = ls // tqc
        start, kind = _tile_window(i, n_tiles, tqc, win, hw, ls)
        num, m, den = _band_tile(q_ref[lead, pl.ds(i * tqc, tqc), :], k_ref[lead, pl.ds(start, win), :],
                                 v_ref[lead, pl.ds(start, win), :], b_ref[kind])
        return _unstack_heads(num, tqc), _unstack_column(m, tqc), _unstack_column(den, tqc), tqc

    def tile1(i, carry):
        num, m, den, _ = run_tile(q1_ref, k1_ref, v1_ref, bias1_ref, 0, i, seq)
        rows = pl.ds(pl.multiple_of(i * tq, tq), tq)
        m_scr[rows, :] = m
        l_scr[rows, :] = den
        n_scr[rows, :] = num
        return carry

    lax.fori_loop(0, seq // tq, tile1, 0, unroll=4)

    def merge(tiles):
        old = [(m_scr[rows, :], l_scr[rows, :], n_scr[rows, :]) for rows, _, _, _ in tiles]
        for (rows, num, m, den), (m_old, l_old, n_old) in zip(tiles, old):
            m_new = jnp.maximum(m_old, m)
            a, b = jnp.exp(m_old - m_new), jnp.exp(m - m_new)
            m_scr[rows, :] = m_new
            l_scr[rows, :] = a * l_old + b * den
            n_scr[rows, :] = a * n_old + b * num

    ls4 = seq // 4

    def tile4(i, carry):
        tiles = []
        for r in range(4):
            num, m, den, tqc = run_tile(q4_ref, k4_ref, v4_ref, bias4_ref, r, i, ls4)
            tiles.append((pl.ds(i * (tqc * 4) + r, tqc, stride=4), num, m, den))
        merge(tiles)
        return carry

    lax.fori_loop(0, ls4 // min(tq, ls4), tile4, 0)

    ls16 = seq // 16

    def class16(r2, carry):
        tiles = []
        for r in (2 * r2, 2 * r2 + 1):
            for i in range(ls16 // min(tq, ls16)):
                num, m, den, tqc = run_tile(q16_ref, k16_ref, v16_ref, bias16_ref, r, i, ls16)
                tiles.append((pl.ds(i * (tqc * 16) + r, tqc, stride=16), num, m, den))
        merge(tiles)
        return carry

    lax.fori_loop(0, 8, class16, 0)
    o_ref[0] = (n_scr[...] * (1.0 / l_scr[...])).astype(o_ref.dtype)


def _mixer_a(proj, a4, a16, *, batch, seq, row0):
    n = proj.shape[0]
    b0 = row0 // seq
    hw = A_HALF_WINDOW

    def class_bias(ls):
        tq = min(BAND_TQ, ls)
        return _band_bias(tq, min(tq + 2 * hw, ls), hw)

    biases = [class_bias(seq // d) for d in A_DILATIONS]
    view = proj.reshape(n // seq, seq, IN_WIDTH)
    nat = lambda off: pl.BlockSpec((1, seq, LANES), lambda b, g: (b0 + b, 0, off + g))
    cls = lambda d, off: pl.BlockSpec((d, seq // d, LANES), lambda b, g: (0, b0 + b, off + g))
    full = lambda a: pl.BlockSpec(a.shape, lambda b, g: (0, 0, 0))
    return pl.pallas_call(
        functools.partial(_mixer_a_kernel, seq=seq),
        grid=(batch, A_WIDTH // LANES),
        in_specs=[full(biases[0]), full(biases[1]), full(biases[2]), nat(QA), nat(KA), nat(VA), cls(4, QA), cls(4, KA), cls(4, VA),
                  cls(16, QA), cls(16, KA), cls(16, VA)],
        out_specs=pl.BlockSpec((1, seq, LANES), lambda b, g: (b, 0, g)),
        out_shape=jax.ShapeDtypeStruct((batch, seq, A_WIDTH), jnp.bfloat16),
        scratch_shapes=[pltpu.VMEM((seq, LANES), jnp.float32)] * 3,
        compiler_params=pltpu.CompilerParams(dimension_semantics=("arbitrary",) * 2, vmem_limit_bytes=VMEM_LIMIT),
        name="mixer_a",
    )(*biases, view, view, view, a4, a4, a4, a16, a16, a16)


def _mixer_b_kernel(sink_ref, bias_ref, q_ref, k_ref, v_ref, o_ref, *, seq):
    tq, hw = BAND_TQ, B_HALF_WINDOW
    win = tq + 2 * hw
    n_tiles = seq // tq
    g = pl.program_id(1)
    row = lax.broadcasted_iota(jnp.int32, (2 * tq, 1), 0)
    sink = jnp.where(row < tq, sink_ref[g], sink_ref[g + B_KV_HEADS])

    def tile(i, carry):
        start, kind = _tile_window(i, n_tiles, tq, win, hw, seq)
        rows = pl.ds(pl.multiple_of(i * tq, tq), tq)
        num, _, den = _band_tile(q_ref[0, rows, :], k_ref[0, pl.ds(start, win), :], v_ref[0, pl.ds(start, win), :],
                                 bias_ref[kind], sink)
        o_ref[0, rows, :] = _unstack_heads(num * (1.0 / den), tq).astype(o_ref.dtype)
        return carry

    lax.fori_loop(0, n_tiles, tile, 0, unroll=4)


def _mixer_b(proj, sink, *, batch, seq, row0):
    n = proj.shape[0]
    b0 = row0 // seq
    bias = _band_bias(BAND_TQ, BAND_TQ + 2 * B_HALF_WINDOW, B_HALF_WINDOW)
    view = proj.reshape(n // seq, seq, IN_WIDTH)
    return pl.pallas_call(
        functools.partial(_mixer_b_kernel, seq=seq),
        grid=(batch, B_WIDTH // LANES),
        in_specs=[pl.BlockSpec(memory_space=pltpu.SMEM), pl.BlockSpec(bias.shape, lambda b, g: (0, 0, 0)),
                  pl.BlockSpec((1, seq, LANES), lambda b, g: (b0 + b, 0, QB + g)),
                  pl.BlockSpec((1, seq, LANES), lambda b, g: (b0 + b, 0, KB)),
                  pl.BlockSpec((1, seq, LANES), lambda b, g: (b0 + b, 0, VB))],
        out_specs=pl.BlockSpec((1, seq, LANES), lambda b, g: (b, 0, g)),
        out_shape=jax.ShapeDtypeStruct((batch, seq, B_WIDTH), jnp.bfloat16),
        compiler_params=pltpu.CompilerParams(dimension_semantics=("arbitrary",) * 2, vmem_limit_bytes=VMEM_LIMIT),
        name="mixer_b",
    )(sink, bias, view, view, view)


def _mixer_c_kernel(q_ref, k_ref, v_ref, o_ref, *, tq, chunk):
    lhs = _stack_heads(q_ref[0])
    seq = k_ref.shape[1]
    m = den = acc = None
    for c in range(seq // chunk):
        keys = slice(c * chunk, (c + 1) * chunk)
        s = lax.dot_general(lhs, k_ref[0, keys, :], (((1,), (1,)), ((), ())), preferred_element_type=jnp.float32)
        m_c = jnp.max(s, axis=-1, keepdims=True)
        m_new = m_c if m is None else jnp.maximum(m, m_c)
        p = jnp.exp(s - m_new)
        den_c = jnp.sum(p, axis=-1, keepdims=True)
        acc_c = jnp.dot(p.astype(jnp.bfloat16), v_ref[0, keys, :], preferred_element_type=jnp.float32)
        if m is None:
            den, acc = den_c, acc_c
        else:
            alpha = jnp.exp(m - m_new)
            den, acc = alpha * den + den_c, alpha * acc + acc_c
        m = m_new
    o_ref[0] = _unstack_heads(acc * (1.0 / den), tq).astype(o_ref.dtype)


def _mixer_c(proj, *, batch, seq, row0, tq=256, chunk=512):
    n = proj.shape[0]
    b0 = row0 // seq
    view = proj.reshape(n // seq, seq, IN_WIDTH)
    return pl.pallas_call(
        functools.partial(_mixer_c_kernel, tq=tq, chunk=chunk),
        grid=(batch, seq // tq, C_WIDTH // LANES),
        in_specs=[
            pl.BlockSpec((1, tq, LANES), lambda b, i, g: (b0 + b, i, QC + g)),
            pl.BlockSpec((1, seq, LANES), lambda b, i, g: (b0 + b, 0, KC)),
            pl.BlockSpec((1, seq, LANES), lambda b, i, g: (b0 + b, 0, VC)),
        ],
        out_specs=pl.BlockSpec((1, tq, LANES), lambda b, i, g: (b, i, g)),
        out_shape=jax.ShapeDtypeStruct((batch, seq, C_WIDTH), jnp.bfloat16),
        compiler_params=pltpu.CompilerParams(dimension_semantics=("arbitrary",) * 3, vmem_limit_bytes=VMEM_LIMIT),
        name="mixer_c",
    )(view, view, view)


def _rms(x, g):
    return x * lax.rsqrt(jnp.mean(x * x, axis=-1, keepdims=True) + EPS) * g


def _outproj_kernel(oa0_ref, oa1_ref, ob0_ref, ob1_ref, oc0_ref, oc1_ref, x_ref, w_ref, ga_ref, gb_ref, gc_ref, gf_ref,
                    wr_ref, xo_ref, h_ref, aff_ref, *, t0):
    f32 = jnp.float32
    first = pl.program_id(0) < t0
    pick = lambda r0, r1: jnp.where(first, r0[...], r1[...]).astype(f32)
    merged = jnp.concatenate([_rms(pick(oa0_ref, oa1_ref), ga_ref[...]), _rms(pick(ob0_ref, ob1_ref), gb_ref[...]),
                              _rms(pick(oc0_ref, oc1_ref), gc_ref[...])], axis=-1).astype(jnp.bfloat16)
    xn = x_ref[...] + jnp.dot(merged, w_ref[...], preferred_element_type=f32)
    xo_ref[...] = xn
    h = _rms(xn, gf_ref[...])
    h_ref[...] = h.astype(jnp.bfloat16)
    logits = lax.dot_general(wr_ref[...], h, (((1,), (1,)), ((), ())), preferred_element_type=f32,
                             precision=lax.Precision.HIGHEST)
    z = jnp.exp(logits - jnp.max(logits, axis=0, keepdims=True))
    aff_ref[...] = z / jnp.sum(z, axis=0, keepdims=True)


def _out_projection(oa, ob, oc, x, w, ga, gb, gc, gf, wr_t, tm):
    n = x.shape[0]
    t0 = oa[0].shape[0] // tm
    t1 = oa[1].shape[0] // tm
    rows = lambda width: pl.BlockSpec((tm, width), lambda i: (i, 0))
    rows0 = lambda width: pl.BlockSpec((tm, width), lambda i: (jnp.minimum(i, t0 - 1), 0))
    rows1 = lambda width: pl.BlockSpec((tm, width), lambda i: (jnp.clip(i - t0, 0, t1 - 1), 0))
    const = lambda shape: pl.BlockSpec(shape, lambda i: (0, 0))
    return pl.pallas_call(
        functools.partial(_outproj_kernel, t0=t0),
        grid=(n // tm,),
        in_specs=[rows0(A_WIDTH), rows1(A_WIDTH), rows0(B_WIDTH), rows1(B_WIDTH), rows0(C_WIDTH), rows1(C_WIDTH),
                  rows(D_MODEL), const((D_MODEL, D_MODEL)), const((1, A_WIDTH)), const((1, B_WIDTH)),
                  const((1, C_WIDTH)), const((1, D_MODEL)), const((N_EXPERTS, D_MODEL))],
        out_specs=[rows(D_MODEL), rows(D_MODEL), pl.BlockSpec((N_EXPERTS, tm), lambda i: (0, i))],
        out_shape=[jax.ShapeDtypeStruct((n, D_MODEL), jnp.float32), jax.ShapeDtypeStruct((n, D_MODEL), jnp.bfloat16),
                   jax.ShapeDtypeStruct((N_EXPERTS, n), jnp.float32)],
        compiler_params=pltpu.CompilerParams(dimension_semantics=("arbitrary",), vmem_limit_bytes=VMEM_LIMIT),
        name="out_projection",
    )(oa[0], oa[1], ob[0], ob[1], oc[0], oc[1], x, w, ga, gb, gc, gf, wr_t)


def _ffn_kernel(x_ref, wg_ref, wu_ref, wd_ref, gate_ref, o_ref):
    j = pl.program_id(2)
    x = x_ref[0]
    hg = jnp.dot(x, wg_ref[...].astype(jnp.bfloat16), preferred_element_type=jnp.float32)
    hu = jnp.dot(x, wu_ref[...].astype(jnp.bfloat16), preferred_element_type=jnp.float32)
    act = (hg * jax.nn.sigmoid(hg) * hu).astype(jnp.bfloat16)
    y = jnp.dot(act, wd_ref[...].astype(jnp.bfloat16), preferred_element_type=jnp.float32)

    @pl.when(j == 0)
    def _():
        o_ref[0] = y

    @pl.when(j > 0)
    def _():
        o_ref[0] += y

    @pl.when(j == pl.num_programs(2) - 1)
    def _():
        o_ref[0] = o_ref[0] * gate_ref[0]


def _expert_ffn(xe, gate, w_gate, w_up, w_down, layer, tr, tf):
    n_e, rows, d = xe.shape
    d_ff = w_gate.shape[-1]
    return pl.pallas_call(
        _ffn_kernel,
        grid=(n_e, rows // tr, d_ff // tf),
        in_specs=[
            pl.BlockSpec((1, tr, d), lambda e, c, j: (e, c, 0)),
            pl.BlockSpec((None, None, d, tf), lambda e, c, j: (layer, e, 0, j)),
            pl.BlockSpec((None, None, d, tf), lambda e, c, j: (layer, e, 0, j)),
            pl.BlockSpec((None, None, tf, d), lambda e, c, j: (layer, e, j, 0)),
            pl.BlockSpec((1, tr, 1), lambda e, c, j: (e, c, 0)),
        ],
        out_specs=pl.BlockSpec((1, tr, d), lambda e, c, j: (e, c, 0)),
        out_shape=jax.ShapeDtypeStruct((n_e, rows, d), jnp.float32),
        compiler_params=pltpu.CompilerParams(dimension_semantics=("arbitrary",) * 3, vmem_limit_bytes=VMEM_LIMIT),
        name="expert_ffn",
    )(xe, w_gate, w_up, w_down, gate)


def _final_norm_kernel(x_ref, g_ref, o_ref):
    o_ref[...] = _rms(x_ref[...], g_ref[...])


def _final_norm(x, g, tm):
    n = x.shape[0]
    return pl.pallas_call(
        _final_norm_kernel,
        grid=(n // tm,),
        in_specs=[pl.BlockSpec((tm, D_MODEL), lambda i: (i, 0)), pl.BlockSpec((1, D_MODEL), lambda i: (0, 0))],
        out_specs=pl.BlockSpec((tm, D_MODEL), lambda i: (i, 0)),
        out_shape=jax.ShapeDtypeStruct((n, D_MODEL), jnp.float32),
        name="final_norm",
    )(x, g)


def _forward(x_prompt, x_sample, g_attn, w_in, g_q_c, g_k_c, sink_b, g_out_a, g_out_b, g_out_c, w_out, g_ffn, w_router,
             w_gate, w_up, w_down, g_final, *, tm, tr, tf):
    f32, bf16 = jnp.float32, jnp.bfloat16
    shapes = (x_prompt.shape[:2], x_sample.shape[:2])
    seqs = tuple((b * l, l) for b, l in shapes)
    n_tok = [n for n, _ in seqs]
    row0 = (0, n_tok[0])
    assert all(r % l == 0 and l % tm == 0 for r, (_, l) in zip(row0, shapes))
    x = jnp.concatenate([x_prompt.reshape(-1, D_MODEL), x_sample.reshape(-1, D_MODEL)], axis=0)

    tables = _rope_tables(max(l for _, l in shapes))
    seg = jnp.asarray(np.kron(np.eye(LANES // HEAD_DIM), np.ones((HEAD_DIM, HEAD_DIM))), f32)
    pb, pc, perm_out = _out_perms()
    w_in_p = w_in[:, :, _in_perm()].astype(bf16)
    w_out_p = w_out[:, perm_out, :].astype(bf16)
    sink_p = sink_b[:, np.asarray(B_Q_ORDER)]
    tile2 = lambda g: jnp.tile(g, (1, 2))[:, None, :]
    gq, gk = tile2(g_q_c), tile2(g_k_c)
    wr_t = jnp.swapaxes(w_router, 1, 2)
    caps = [CAPACITY_FACTOR * n // N_EXPERTS for n in n_tok]

    for l in range(DEPTH):
        proj, a4, a16 = _in_projection(x, g_attn[l][None], w_in_p[l], tables, gq[l], gk[l], seg, seqs, tm)
        oa, ob, oc = [], [], []
        for (b, s), r0 in zip(shapes, row0):
            oa.append(_mixer_a(proj, a4, a16, batch=b, seq=s, row0=r0).reshape(b * s, A_WIDTH))
            ob.append(_mixer_b(proj, sink_p[l], batch=b, seq=s, row0=r0).reshape(b * s, B_WIDTH))
            oc.append(_mixer_c(proj, batch=b, seq=s, row0=r0).reshape(b * s, C_WIDTH))
        x, h, aff = _out_projection(oa, ob, oc, x, w_out_p[l], g_out_a[l][None], g_out_b[l][pb][None],
                                    g_out_c[l][pc][None], g_ffn[l][None], wr_t[l], tm)
        gates, idxs = [], []
        off = 0
        for n, cap in zip(n_tok, caps):
            gate, idx = lax.top_k(aff[:, off:off + n], cap)
            gates.append(gate)
            idxs.append(idx + off)
            off += n
        gate = jnp.concatenate(gates, axis=1)
        idx = jnp.concatenate(idxs, axis=1)
        xe = h[idx]
        ye = _expert_ffn(xe, gate[..., None], w_gate, w_up, w_down, l, tr, tf)
        x = x.at[idx.reshape(-1)].add(ye.reshape(-1, D_MODEL))

    y = _final_norm(x, g_final[None], tm)
    return (y[:n_tok[0]].reshape(x_prompt.shape), y[n_tok[0]:].reshape(x_sample.shape))


def kernel(x_prompt, x_sample, g_attn, w_in, g_q_c, g_k_c, sink_b, g_out_a, g_out_b, g_out_c, w_out, g_ffn, w_router,
           w_gate, w_up, w_down, g_final):
    return _forward(x_prompt, x_sample, g_attn, w_in, g_q_c, g_k_c, sink_b, g_out_a, g_out_b, g_out_c, w_out, g_ffn,
                    w_router, w_gate, w_up, w_down, g_final, tm=512, tr=2048, tf=256)
```

```python
import functools

import jax
import jax.numpy as jnp
import numpy as np
from jax import lax
from jax.experimental import pallas as pl
from jax.experimental.pallas import tpu as pltpu
from jax.experimental.pallas import tpu_sc as plsc

D_MODEL = 1024
DEPTH = 4
HEAD_DIM = 64
A_HEADS = 6
A_PAIRS = ((128, 1), (512, 4), (2048, 16))
B_HEADS = 4
B_KV_HEADS = 2
B_HALF_WINDOW = 128
C_HEADS = 6
C_KV_HEADS = 2
GRID_W = 64
ROPE_THETA = 10000.0
N_EXPERTS = 16
CAPACITY_FACTOR = 2
D_FF = 2816
EPS = 1e-6
NEG_INF = -1e30

LANES = 128
A_WIDTH = A_HEADS * HEAD_DIM
B_WIDTH = B_HEADS * HEAD_DIM
C_WIDTH = C_HEADS * HEAD_DIM
IN_WIDTH = 3 * A_WIDTH + B_WIDTH + 2 * B_KV_HEADS * HEAD_DIM + C_WIDTH + 2 * C_KV_HEADS * HEAD_DIM
N_GROUPS = IN_WIDTH // LANES
A_GROUPS = 3 * A_WIDTH // LANES
QA, KA, VA, QB, KB, VB, QC, KC, VC = 0, 3, 6, 9, 11, 12, 13, 16, 17
ROPE_NONE, ROPE_1D, ROPE_AXIAL_Q, ROPE_AXIAL_K = 0, 1, 2, 3
GROUP_KIND = ([(ROPE_1D, True)] * 3 + [(ROPE_1D, False)] * 3 + [(ROPE_NONE, False)] * 3
              + [(ROPE_1D, True)] * 2 + [(ROPE_1D, False)] + [(ROPE_NONE, False)]
              + [(ROPE_AXIAL_Q, True)] * 3 + [(ROPE_AXIAL_K, False)] + [(ROPE_NONE, False)])
Q_SCALE = HEAD_DIM ** -0.5
A_DILATIONS = tuple(d for _, d in A_PAIRS)
A_HALF_WINDOW = A_PAIRS[0][0] // 2
assert all(w // 2 // d == A_HALF_WINDOW for w, d in A_PAIRS) and A_DILATIONS == (1, 4, 16)
BAND_TQ = 128
ROW_WIDTH = D_MODEL + LANES
GATE_LANE, DEST_LANE, OFF_LANE, MULT_LANE = 0, N_EXPERTS, 2 * N_EXPERTS, 2 * N_EXPERTS + 1
ONE_BITS = 0x3F800000
SC_WORKERS = 32

VMEM_LIMIT = 56 * 1024 * 1024

B_Q_ORDER = (0, 2, 1, 3)
C_Q_ORDER = (0, 3, 1, 4, 2, 5)


def _head_perm(order):
    return np.concatenate([np.arange(h * HEAD_DIM, (h + 1) * HEAD_DIM) for h in order])


def _in_perm():
    widths = [A_WIDTH] * 3 + [B_WIDTH, 128, 128, C_WIDTH, 128, 128]
    offs = np.concatenate([[0], np.cumsum(widths)])
    parts = [np.arange(offs[i], offs[i + 1]) for i in range(9)]
    parts[3] = offs[3] + _head_perm(B_Q_ORDER)
    parts[6] = offs[6] + _head_perm(C_Q_ORDER)
    return np.concatenate(parts)


def _out_perms():
    pb = _head_perm(B_Q_ORDER)
    pc = _head_perm(C_Q_ORDER)
    return pb, pc, np.concatenate([np.arange(A_WIDTH), A_WIDTH + pb, A_WIDTH + B_WIDTH + pc])


def _rope_tables(seq):
    pos = jnp.arange(seq, dtype=jnp.float32)
    inv1 = ROPE_THETA ** (-jnp.arange(0, HEAD_DIM, 2, dtype=jnp.float32) / HEAD_DIM)
    ang = pos[:, None] * inv1[None, :]
    c, s = jnp.cos(ang), jnp.sin(ang)
    cos1 = jnp.tile(jnp.concatenate([c, c], -1), (1, 2))
    sin1 = jnp.tile(jnp.concatenate([-s, s], -1), (1, 2))
    half = HEAD_DIM // 2
    inv2 = ROPE_THETA ** (-jnp.arange(0, half, 2, dtype=jnp.float32) / half)
    row = jnp.floor(pos / GRID_W)
    col = pos - row * GRID_W
    ar, ac = row[:, None] * inv2[None, :], col[:, None] * inv2[None, :]
    cr, sr, cc, sc = jnp.cos(ar), jnp.sin(ar), jnp.cos(ac), jnp.sin(ac)
    cos2 = jnp.tile(jnp.concatenate([cr, cr, cc, cc], -1), (1, 2))
    sin2 = jnp.tile(jnp.concatenate([-sr, sr, -sc, sc], -1), (1, 2))
    return cos1, sin1, cos2, sin2


def _swap_halves(x, block):
    half = block // 2
    lane = lax.broadcasted_iota(jnp.int32, x.shape, 1)
    return jnp.where(lane % block < half, pltpu.roll(x, LANES - half, 1), pltpu.roll(x, half, 1))


def _inproj_kernel(x_ref, g_ref, w_ref, cos1_ref, sin1_ref, cos2_ref, sin2_ref, gq_ref, gk_ref, seg_ref, o_ref,
                   a4_ref, a16_ref, rows_ref):
    x = x_ref[...]
    tm = x.shape[0]
    y = x * lax.rsqrt(jnp.mean(x * x, axis=-1, keepdims=True) + EPS)
    h = (y * g_ref[...]).astype(jnp.bfloat16)
    for c in range(N_GROUPS // 2):
        acc = jnp.dot(h, w_ref[:, c * 2 * LANES:(c + 1) * 2 * LANES], preferred_element_type=jnp.float32)
        for half in range(2):
            grp = 2 * c + half
            cols = slice(grp * LANES, (grp + 1) * LANES)
            a = acc[:, half * LANES:(half + 1) * LANES]
            kind, is_q = GROUP_KIND[grp]
            if kind == ROPE_1D:
                a = a * cos1_ref[...] + _swap_halves(a, HEAD_DIM) * sin1_ref[...]
            elif kind in (ROPE_AXIAL_Q, ROPE_AXIAL_K):
                gain = gq_ref[...] if kind == ROPE_AXIAL_Q else gk_ref[...]
                ss = jnp.dot(a * a, seg_ref[...], preferred_element_type=jnp.float32, precision=lax.Precision.HIGHEST)
                a = a * lax.rsqrt(ss * (1.0 / HEAD_DIM) + EPS) * gain
                a = a * cos2_ref[...] + _swap_halves(a, HEAD_DIM // 2) * sin2_ref[...]
            if is_q:
                a = a * Q_SCALE
            o_ref[:, cols] = a.astype(jnp.bfloat16)
            if grp < A_GROUPS:
                rows_ref[...] = a
                for d, ref in ((4, a4_ref), (16, a16_ref)):
                    for r in range(d):
                        ref[r, :, cols] = rows_ref[pl.ds(r, tm // d, stride=d), :].astype(jnp.bfloat16)


def _in_projection(x, g, w, tables, gq, gk, seg, seqs, tm):
    n = x.shape[0]
    (n0, l0), (n1, l1) = seqs
    t0 = n0 // tm

    def tab_map(i):
        return (jnp.where(i < t0, i % (l0 // tm), (i - t0) % (l1 // tm)), 0)

    tab_spec = pl.BlockSpec((tm, LANES), tab_map)
    const = lambda shape: pl.BlockSpec(shape, lambda i: (0, 0))
    wa = A_GROUPS * LANES
    return pl.pallas_call(
        _inproj_kernel,
        grid=(n // tm,),
        in_specs=[pl.BlockSpec((tm, D_MODEL), lambda i: (i, 0)), const((1, D_MODEL)), const((D_MODEL, IN_WIDTH)),
                  tab_spec, tab_spec, tab_spec, tab_spec, const((1, LANES)), const((1, LANES)), const((LANES, LANES))],
        out_specs=[pl.BlockSpec((tm, IN_WIDTH), lambda i: (i, 0)),
                   pl.BlockSpec((4, tm // 4, wa), lambda i: (0, i, 0)),
                   pl.BlockSpec((16, tm // 16, wa), lambda i: (0, i, 0))],
        out_shape=[jax.ShapeDtypeStruct((n, IN_WIDTH), jnp.bfloat16),
                   jax.ShapeDtypeStruct((4, n // 4, wa), jnp.bfloat16),
                   jax.ShapeDtypeStruct((16, n // 16, wa), jnp.bfloat16)],
        scratch_shapes=[pltpu.VMEM((tm, LANES), jnp.float32)],
        compiler_params=pltpu.CompilerParams(dimension_semantics=("arbitrary",), vmem_limit_bytes=VMEM_LIMIT),
        name="in_projection",
    )(x, g, w, *tables, gq, gk, seg)


def _stack_heads(q):
    lane = lax.broadcasted_iota(jnp.int32, q.shape, 1)
    zero = jnp.zeros_like(q)
    return jnp.concatenate([jnp.where(lane < HEAD_DIM, q, zero), jnp.where(lane >= HEAD_DIM, q, zero)], axis=0)


def _unstack_heads(x, tq):
    lane = lax.broadcasted_iota(jnp.int32, (tq, x.shape[1]), 1)
    return jnp.where(lane < HEAD_DIM, x[:tq], x[tq:])


def _unstack_column(col, tq):
    return _unstack_heads(jnp.broadcast_to(col, (2 * tq, LANES)), tq)


def _band_bias(tq, win, half_window):
    row = np.arange(2 * tq)[:, None] % tq
    col = np.arange(win)[None, :]
    kinds = [np.where(np.abs(row + off - col) <= half_window, 0.0, NEG_INF) for off in (0, half_window, 2 * half_window)]
    return jnp.asarray(np.stack(kinds), jnp.float32)


def _band_tile(q, kw, vw, bias, sink=None):
    s = lax.dot_general(_stack_heads(q), kw, (((1,), (1,)), ((), ())), preferred_element_type=jnp.float32) + bias
    m = jnp.max(s, axis=-1, keepdims=True)
    if sink is not None:
        m = jnp.maximum(m, sink)
    p = jnp.exp(s - m)
    den = jnp.sum(p, axis=-1, keepdims=True)
    if sink is not None:
        den = den + jnp.exp(sink - m)
    num = jnp.dot(p.astype(jnp.bfloat16), vw, preferred_element_type=jnp.float32)
    return num, m, den


def _tile_window(i, n_tiles, tq, win, half_window, seq):
    start = pl.multiple_of(jnp.clip(i * tq - half_window, 0, seq - win), 64)
    kind = jnp.where(i == 0, 0, jnp.where(i == n_tiles - 1, 2, 1))
    return start, kind


def _mixer_a_kernel(bias1_ref, bias4_ref, bias16_ref, q1_ref, k1_ref, v1_ref, q4_ref, k4_ref, v4_ref, q16_ref, k16_ref, v16_ref,
                    o_ref, m_scr, l_scr, n_scr, *, seq):
    tq, hw = BAND_TQ, A_HALF_WINDOW

    def run_tile(q_ref, k_ref, v_ref, b_ref, lead, i, ls):
        tqc = min(tq, ls)
        win = min(tqc + 2 * hw, ls)
        n_tiles = ls // tqc
        start, kind = _tile_window(i, n_tiles, tqc, win, hw, ls)
        num, m, den = _band_tile(q_ref[lead, pl.ds(i * tqc, tqc), :], k_ref[lead, pl.ds(start, win), :],
                                 v_ref[lead, pl.ds(start, win), :], b_ref[kind])
        return _unstack_heads(num, tqc), _unstack_column(m, tqc), _unstack_column(den, tqc), tqc

    def tile1(i, carry):
        num, m, den, _ = run_tile(q1_ref, k1_ref, v1_ref, bias1_ref, 0, i, seq)
        rows = pl.ds(pl.multiple_of(i * tq, tq), tq)
        m_scr[rows, :] = m
        l_scr[rows, :] = den
        n_scr[rows, :] = num
        return carry

    lax.fori_loop(0, seq // tq, tile1, 0, unroll=4)

    def merge(tiles):
        old = [(m_scr[rows, :], l_scr[rows, :], n_scr[rows, :]) for rows, _, _, _ in tiles]
        for (rows, num, m, den), (m_old, l_old, n_old) in zip(tiles, old):
            m_new = jnp.maximum(m_old, m)
            a, b = jnp.exp(m_old - m_new), jnp.exp(m - m_new)
            m_scr[rows, :] = m_new
            l_scr[rows, :] = a * l_old + b * den
            n_scr[rows, :] = a * n_old + b * num

    ls4 = seq // 4

    def tile4(i, carry):
        tiles = []
        for r in range(4):
            num, m, den, tqc = run_tile(q4_ref, k4_ref, v4_ref, bias4_ref, r, i, ls4)
            tiles.append((pl.ds(i * (tqc * 4) + r, tqc, stride=4), num, m, den))
        merge(tiles)
        return carry

    lax.fori_loop(0, ls4 // min(tq, ls4), tile4, 0)

    ls16 = seq // 16

    def class16(r2, carry):
        tiles = []
        for r in (2 * r2, 2 * r2 + 1):
            for i in range(ls16 // min(tq, ls16)):
                num, m, den, tqc = run_tile(q16_ref, k16_ref, v16_ref, bias16_ref, r, i, ls16)
                tiles.append((pl.ds(i * (tqc * 16) + r, tqc, stride=16), num, m, den))
        merge(tiles)
        return carry

    lax.fori_loop(0, 8, class16, 0)
    o_ref[0] = (n_scr[...] * (1.0 / l_scr[...])).astype(o_ref.dtype)


def _mixer_a(proj, a4, a16, *, batch, seq, row0):
    n = proj.shape[0]
    b0 = row0 // seq
    hw = A_HALF_WINDOW

    def class_bias(ls):
        tq = min(BAND_TQ, ls)
        return _band_bias(tq, min(tq + 2 * hw, ls), hw)

    biases = [class_bias(seq // d) for d in A_DILATIONS]
    view = proj.reshape(n // seq, seq, IN_WIDTH)
    nat = lambda off: pl.BlockSpec((1, seq, LANES), lambda b, g: (b0 + b, 0, off + g))
    cls = lambda d, off: pl.BlockSpec((d, seq // d, LANES), lambda b, g: (0, b0 + b, off + g))
    full = lambda a: pl.BlockSpec(a.shape, lambda b, g: (0, 0, 0))
    return pl.pallas_call(
        functools.partial(_mixer_a_kernel, seq=seq),
        grid=(batch, A_WIDTH // LANES),
        in_specs=[full(biases[0]), full(biases[1]), full(biases[2]), nat(QA), nat(KA), nat(VA), cls(4, QA), cls(4, KA), cls(4, VA),
                  cls(16, QA), cls(16, KA), cls(16, VA)],
        out_specs=pl.BlockSpec((1, seq, LANES), lambda b, g: (b, 0, g)),
        out_shape=jax.ShapeDtypeStruct((batch, seq, A_WIDTH), jnp.bfloat16),
        scratch_shapes=[pltpu.VMEM((seq, LANES), jnp.float32)] * 3,
        compiler_params=pltpu.CompilerParams(dimension_semantics=("arbitrary",) * 2, vmem_limit_bytes=VMEM_LIMIT),
        name="mixer_a",
    )(*biases, view, view, view, a4, a4, a4, a16, a16, a16)


def _mixer_b_kernel(sink_ref, bias_ref, q_ref, k_ref, v_ref, o_ref, *, seq):
    tq, hw = BAND_TQ, B_HALF_WINDOW
    win = tq + 2 * hw
    n_tiles = seq // tq
    g = pl.program_id(1)
    row = lax.broadcasted_iota(jnp.int32, (2 * tq, 1), 0)
    sink = jnp.where(row < tq, sink_ref[g], sink_ref[g + B_KV_HEADS])

    def tile(i, carry):
        start, kind = _tile_window(i, n_tiles, tq, win, hw, seq)
        rows = pl.ds(pl.multiple_of(i * tq, tq), tq)
        num, _, den = _band_tile(q_ref[0, rows, :], k_ref[0, pl.ds(start, win), :], v_ref[0, pl.ds(start, win), :],
                                 bias_ref[kind], sink)
        o_ref[0, rows, :] = _unstack_heads(num * (1.0 / den), tq).astype(o_ref.dtype)
        return carry

    lax.fori_loop(0, n_tiles, tile, 0, unroll=4)


def _mixer_b(proj, sink, *, batch, seq, row0):
    n = proj.shape[0]
    b0 = row0 // seq
    bias = _band_bias(BAND_TQ, BAND_TQ + 2 * B_HALF_WINDOW, B_HALF_WINDOW)
    view = proj.reshape(n // seq, seq, IN_WIDTH)
    return pl.pallas_call(
        functools.partial(_mixer_b_kernel, seq=seq),
        grid=(batch, B_WIDTH // LANES),
        in_specs=[pl.BlockSpec(memory_space=pltpu.SMEM), pl.BlockSpec(bias.shape, lambda b, g: (0, 0, 0)),
                  pl.BlockSpec((1, seq, LANES), lambda b, g: (b0 + b, 0, QB + g)),
                  pl.BlockSpec((1, seq, LANES), lambda b, g: (b0 + b, 0, KB)),
                  pl.BlockSpec((1, seq, LANES), lambda b, g: (b0 + b, 0, VB))],
        out_specs=pl.BlockSpec((1, seq, LANES), lambda b, g: (b, 0, g)),
        out_shape=jax.ShapeDtypeStruct((batch, seq, B_WIDTH), jnp.bfloat16),
        compiler_params=pltpu.CompilerParams(dimension_semantics=("arbitrary",) * 2, vmem_limit_bytes=VMEM_LIMIT),
        name="mixer_b",
    )(sink, bias, view, view, view)


def _mixer_c_kernel(q_ref, k_ref, v_ref, o_ref, *, tq, chunk):
    lhs = _stack_heads(q_ref[0])
    seq = k_ref.shape[1]
    m = den = acc = None
    for c in range(seq // chunk):
        keys = slice(c * chunk, (c + 1) * chunk)
        s = lax.dot_general(lhs, k_ref[0, keys, :], (((1,), (1,)), ((), ())), preferred_element_type=jnp.float32)
        m_c = jnp.max(s, axis=-1, keepdims=True)
        m_new = m_c if m is None else jnp.maximum(m, m_c)
        p = jnp.exp(s - m_new)
        den_c = jnp.sum(p, axis=-1, keepdims=True)
        acc_c = jnp.dot(p.astype(jnp.bfloat16), v_ref[0, keys, :], preferred_element_type=jnp.float32)
        if m is None:
            den, acc = den_c, acc_c
        else:
            alpha = jnp.exp(m - m_new)
            den, acc = alpha * den + den_c, alpha * acc + acc_c
        m = m_new
    o_ref[0] = _unstack_heads(acc * (1.0 / den), tq).astype(o_ref.dtype)


def _mixer_c(proj, *, batch, seq, row0, tq=256, chunk=512):
    n = proj.shape[0]
    b0 = row0 // seq
    view = proj.reshape(n // seq, seq, IN_WIDTH)
    return pl.pallas_call(
        functools.partial(_mixer_c_kernel, tq=tq, chunk=chunk),
        grid=(batch, seq // tq, C_WIDTH // LANES),
        in_specs=[
            pl.BlockSpec((1, tq, LANES), lambda b, i, g: (b0 + b, i, QC + g)),
            pl.BlockSpec((1, seq, LANES), lambda b, i, g: (b0 + b, 0, KC)),
            pl.BlockSpec((1, seq, LANES), lambda b, i, g: (b0 + b, 0, VC)),
        ],
        out_specs=pl.BlockSpec((1, tq, LANES), lambda b, i, g: (b, i, g)),
        out_shape=jax.ShapeDtypeStruct((batch, seq, C_WIDTH), jnp.bfloat16),
        compiler_params=pltpu.CompilerParams(dimension_semantics=("arbitrary",) * 3, vmem_limit_bytes=VMEM_LIMIT),
        name="mixer_c",
    )(view, view, view)


def _rms(x, g):
    return x * lax.rsqrt(jnp.mean(x * x, axis=-1, keepdims=True) + EPS) * g


def _outproj_kernel(oa0_ref, oa1_ref, ob0_ref, ob1_ref, oc0_ref, oc1_ref, x_ref, w_ref, ga_ref, gb_ref, gc_ref, gf_ref,
                    wr_ref, xo_ref, h_ref, aff_ref, *, t0):
    f32 = jnp.float32
    first = pl.program_id(0) < t0
    pick = lambda r0, r1: jnp.where(first, r0[...], r1[...]).astype(f32)
    merged = jnp.concatenate([_rms(pick(oa0_ref, oa1_ref), ga_ref[...]), _rms(pick(ob0_ref, ob1_ref), gb_ref[...]),
                              _rms(pick(oc0_ref, oc1_ref), gc_ref[...])], axis=-1).astype(jnp.bfloat16)
    xn = x_ref[...] + jnp.dot(merged, w_ref[...], preferred_element_type=f32)
    xo_ref[...] = xn
    h = _rms(xn, gf_ref[...])
    h_ref[:, :D_MODEL] = h
    h_ref[:, D_MODEL:] = jnp.zeros((h.shape[0], LANES), f32)
    logits = lax.dot_general(wr_ref[...], h, (((1,), (1,)), ((), ())), preferred_element_type=f32,
                             precision=lax.Precision.HIGHEST)
    z = jnp.exp(logits - jnp.max(logits, axis=0, keepdims=True))
    aff = z / jnp.sum(z, axis=0, keepdims=True)
    for c in range(aff.shape[1] // LANES):
        aff_ref[c] = aff[:, c * LANES:(c + 1) * LANES]


def _out_projection(oa, ob, oc, x, w, ga, gb, gc, gf, wr_t, tm):
    n = x.shape[0]
    t0 = oa[0].shape[0] // tm
    t1 = oa[1].shape[0] // tm
    rows = lambda width: pl.BlockSpec((tm, width), lambda i: (i, 0))
    rows0 = lambda width: pl.BlockSpec((tm, width), lambda i: (jnp.minimum(i, t0 - 1), 0))
    rows1 = lambda width: pl.BlockSpec((tm, width), lambda i: (jnp.clip(i - t0, 0, t1 - 1), 0))
    const = lambda shape: pl.BlockSpec(shape, lambda i: (0, 0))
    return pl.pallas_call(
        functools.partial(_outproj_kernel, t0=t0),
        grid=(n // tm,),
        in_specs=[rows0(A_WIDTH), rows1(A_WIDTH), rows0(B_WIDTH), rows1(B_WIDTH), rows0(C_WIDTH), rows1(C_WIDTH),
                  rows(D_MODEL), const((D_MODEL, D_MODEL)), const((1, A_WIDTH)), const((1, B_WIDTH)),
                  const((1, C_WIDTH)), const((1, D_MODEL)), const((N_EXPERTS, D_MODEL))],
        out_specs=[rows(D_MODEL), rows(ROW_WIDTH), pl.BlockSpec((tm // LANES, N_EXPERTS, LANES), lambda i: (i, 0, 0))],
        out_shape=[jax.ShapeDtypeStruct((n, D_MODEL), jnp.float32), jax.ShapeDtypeStruct((n, ROW_WIDTH), jnp.float32),
                   jax.ShapeDtypeStruct((n // LANES, N_EXPERTS, LANES), jnp.float32)],
        compiler_params=pltpu.CompilerParams(dimension_semantics=("arbitrary",), vmem_limit_bytes=VMEM_LIMIT),
        name="out_projection",
    )(oa[0], oa[1], ob[0], ob[1], oc[0], oc[1], x, w, ga, gb, gc, gf, wr_t)


def _ffn_kernel(x_ref, wg_ref, wu_ref, wd_ref, o_ref, dest_ref, xb_scr, gate_scr):
    e = pl.program_id(0)
    j = pl.program_id(2)

    @pl.when(j == 0)
    def _():
        xb_scr[...] = x_ref[0, :, :D_MODEL].astype(jnp.bfloat16)
        route = x_ref[0, :, D_MODEL:]
        lane = lax.broadcasted_iota(jnp.int32, route.shape, 1)
        pick = lambda k: jnp.sum(jnp.where(lane == k + e, route, 0.0), axis=-1, keepdims=True)
        gate_scr[...] = pick(GATE_LANE)
        dest_ref[0] = pick(DEST_LANE).astype(jnp.int32)

    x = xb_scr[...]
    hg = jnp.dot(x, wg_ref[...].astype(jnp.bfloat16), preferred_element_type=jnp.float32)
    hu = jnp.dot(x, wu_ref[...].astype(jnp.bfloat16), preferred_element_type=jnp.float32)
    act = (hg * jax.nn.sigmoid(hg) * hu).astype(jnp.bfloat16)
    y = jnp.dot(act, wd_ref[...].astype(jnp.bfloat16), preferred_element_type=jnp.float32)

    @pl.when(j == 0)
    def _():
        o_ref[0] = y

    @pl.when(j > 0)
    def _():
        o_ref[0] += y

    @pl.when(j == pl.num_programs(2) - 1)
    def _():
        o_ref[0] = o_ref[0] * gate_scr[...]


def _expert_ffn(xe, w_gate, w_up, w_down, layer, tr, tf):
    n_e, rows, _ = xe.shape
    d = D_MODEL
    d_ff = w_gate.shape[-1]
    return pl.pallas_call(
        _ffn_kernel,
        grid=(n_e, rows // tr, d_ff // tf),
        in_specs=[
            pl.BlockSpec((1, tr, ROW_WIDTH), lambda e, c, j: (e, c, 0)),
            pl.BlockSpec((None, None, d, tf), lambda e, c, j: (layer, e, 0, j)),
            pl.BlockSpec((None, None, d, tf), lambda e, c, j: (layer, e, 0, j)),
            pl.BlockSpec((None, None, tf, d), lambda e, c, j: (layer, e, j, 0)),
        ],
        out_specs=[pl.BlockSpec((1, tr, d), lambda e, c, j: (e, c, 0)),
                   pl.BlockSpec((1, tr, 1), lambda e, c, j: (e, c, 0))],
        out_shape=[jax.ShapeDtypeStruct((n_e, rows, d), jnp.float32),
                   jax.ShapeDtypeStruct((n_e, rows, 1), jnp.int32)],
        scratch_shapes=[pltpu.VMEM((tr, d), jnp.bfloat16), pltpu.VMEM((tr, 1), jnp.float32)],
        compiler_params=pltpu.CompilerParams(dimension_semantics=("arbitrary",) * 3, vmem_limit_bytes=VMEM_LIMIT),
        name="expert_ffn",
    )(xe, w_gate, w_up, w_down)


def _lane_cumsum(m, tri):
    nc, r, _ = m.shape
    flat = m.reshape(nc * r, LANES).astype(jnp.bfloat16)
    return jnp.dot(flat, tri, preferred_element_type=jnp.float32).reshape(nc, r, LANES)


def _lead_cumsum_exclusive(t):
    n = t.shape[0]
    inc, k = t, 1
    while k < n:
        inc = inc + jnp.concatenate([jnp.zeros((k,) + t.shape[1:], t.dtype), inc[:n - k]], axis=0)
        k *= 2
    return inc - t


def _token_cumsum(m, tri):
    inside = _lane_cumsum(m, tri)
    total = inside[:, :, LANES - 1:]
    return _lead_cumsum_exclusive(total), inside, total


def _select_kernel(aff_ref, tri_ref, idx_ref, stats_ref, split_scr, before_scr, through_scr, *, groups):
    f32 = jnp.float32
    tri = tri_ref[...]
    slot_base = 0
    for c0, nc, cap, s0 in groups:
        aff = aff_ref[c0:c0 + nc]
        bits = pltpu.bitcast(aff, jnp.int32)
        count = lambda mask: jnp.sum(jnp.sum(mask, axis=0, keepdims=True), axis=2, keepdims=True)

        def bisect(_, carry):
            lo, hi = carry
            mid = lo + ((hi - lo) >> 1)
            ok = count(jnp.where(bits >= mid, 1.0, 0.0)) >= cap
            return jnp.where(ok, mid, lo), jnp.where(ok, hi, mid)

        shape = (1, N_EXPERTS, 1)
        thr, _ = lax.fori_loop(0, 31, bisect, (jnp.zeros(shape, jnp.int32), jnp.full(shape, ONE_BITS + 1, jnp.int32)))
        above, tie = bits > thr, bits == thr
        tie_f = jnp.where(tie, 1.0, 0.0)
        need = cap - count(jnp.where(above, 1.0, 0.0))
        before, inside, _ = _token_cumsum(tie_f, tri)
        chosen = jnp.where(above | (tie & (before + inside - tie_f < need)), 1.0, 0.0)

        before, inside, total = _token_cumsum(chosen, tri)
        through = before + inside
        mult = jnp.sum(chosen, axis=1, keepdims=True)
        m_before, m_inside, _ = _token_cumsum(mult, tri)
        stats_ref[c0:c0 + nc, 0:N_EXPERTS, :] = chosen
        stats_ref[c0:c0 + nc, N_EXPERTS:N_EXPERTS + 1, :] = slot_base + m_before + m_inside - mult
        stats_ref[c0:c0 + nc, N_EXPERTS + 1:N_EXPERTS + 2, :] = mult
        stats_ref[c0:c0 + nc, N_EXPERTS + 2:, :] = jnp.zeros((nc, 6, LANES), f32)
        slot_base += N_EXPERTS * cap

        for e in range(N_EXPERTS):
            t_e = through[:, e, :]
            hi_digit = jnp.floor(t_e * (1.0 / 64))
            split_scr[e, 0:nc, 0:LANES] = hi_digit.astype(jnp.bfloat16)
            split_scr[e, 0:nc, LANES:] = (t_e - 64.0 * hi_digit).astype(jnp.bfloat16)
            before_scr[e, 0:nc, :] = jnp.broadcast_to(before[:, e, :], (nc, LANES))
            through_scr[e, 0:nc, :] = jnp.broadcast_to((before + total)[:, e, :], (nc, LANES))

        chunk_id = lax.broadcasted_iota(jnp.int32, (1, nc), 1).astype(f32)
        lane_id = lax.broadcasted_iota(jnp.int32, (1, LANES), 1)
        row_id = lax.broadcasted_iota(jnp.int32, (LANES, 1), 0)

        def compact(it, carry):
            e, s = it // (cap // LANES), it % (cap // LANES)
            slot_row = (s * LANES + lane_id).astype(f32)
            slot_col = (s * LANES + row_id).astype(f32)
            holds = (before_scr[e, 0:nc, :] <= slot_row) & (slot_row < through_scr[e, 0:nc, :])
            onehot = jnp.where(holds, 1.0, 0.0).T
            digits = jnp.dot(onehot.astype(jnp.bfloat16), split_scr[e, 0:nc, :], preferred_element_type=f32)
            counts = 64.0 * digits[:, :LANES] + digits[:, LANES:]
            inside_pos = jnp.sum(jnp.where(counts <= slot_col, 1.0, 0.0), axis=-1, keepdims=True)
            chunk = jnp.sum(onehot * chunk_id, axis=-1, keepdims=True)
            token = (c0 + chunk) * LANES + inside_pos
            idx_ref[e, pl.ds(s0 + s, 1), :] = jnp.broadcast_to(token, (LANES, LANES)).T[0:1, :].astype(jnp.int32)
            return carry

        lax.fori_loop(0, N_EXPERTS * (cap // LANES), compact, 0)


def _select(aff, groups, slots):
    n_chunks = aff.shape[0]
    nc_max = max(nc for _, nc, _, _ in groups)
    tri = jnp.asarray(np.triu(np.ones((LANES, LANES))), jnp.bfloat16)
    return pl.pallas_call(
        functools.partial(_select_kernel, groups=groups),
        out_shape=[jax.ShapeDtypeStruct((N_EXPERTS, slots // LANES, LANES), jnp.int32),
                   jax.ShapeDtypeStruct((n_chunks, 24, LANES), jnp.float32)],
        scratch_shapes=[pltpu.VMEM((N_EXPERTS, nc_max, 2 * LANES), jnp.bfloat16),
                        pltpu.VMEM((N_EXPERTS, nc_max, LANES), jnp.float32),
                        pltpu.VMEM((N_EXPERTS, nc_max, LANES), jnp.float32)],
        compiler_params=pltpu.CompilerParams(vmem_limit_bytes=VMEM_LIMIT),
        name="expert_select",
    )(aff, tri)


def _route_rows_kernel(aff_ref, stats_ref, below_ref, rows_in_ref, o_ref):
    del rows_in_ref
    n = aff_ref.shape[0]
    pad = jnp.zeros((LANES - N_EXPERTS - stats_ref.shape[1], LANES), jnp.float32)
    lane = lax.broadcasted_iota(jnp.int32, (LANES, LANES), 1)
    for c in range(n):
        t = jnp.concatenate([aff_ref[c], stats_ref[c], pad], axis=0).T
        rank = jnp.dot(t.astype(jnp.bfloat16), below_ref[...], preferred_element_type=jnp.float32)
        first = t[:, OFF_LANE:OFF_LANE + 1]
        o_ref[c * LANES:(c + 1) * LANES, :] = jnp.where((lane >= DEST_LANE) & (lane < OFF_LANE), first + rank, t)


def _route_rows(aff, stats, rows, tm):
    n = rows.shape[0]
    k = tm // LANES
    below = np.zeros((LANES, LANES))
    below[DEST_LANE:OFF_LANE, DEST_LANE:OFF_LANE] = np.triu(np.ones((N_EXPERTS, N_EXPERTS)), 1)
    return pl.pallas_call(
        _route_rows_kernel,
        grid=(n // tm,),
        in_specs=[pl.BlockSpec((k, N_EXPERTS, LANES), lambda i: (i, 0, 0)),
                  pl.BlockSpec((k, stats.shape[1], LANES), lambda i: (i, 0, 0)),
                  pl.BlockSpec((LANES, LANES), lambda i: (0, 0)),
                  pl.BlockSpec(memory_space=pl.ANY)],
        out_specs=pl.BlockSpec((tm, LANES), lambda i: (i, D_MODEL // LANES)),
        out_shape=jax.ShapeDtypeStruct(rows.shape, rows.dtype),
        input_output_aliases={3: 0},
        compiler_params=pltpu.CompilerParams(dimension_semantics=("arbitrary",)),
        name="route_rows",
    )(aff, stats, jnp.asarray(below, jnp.bfloat16), rows)


SC_ROWS = 32


def _sc_mesh():
    return plsc.VectorSubcoreMesh(core_axis_name="core", subcore_axis_name="subcore")


def _sc_gather(table, idx):
    m, w = idx.shape[0], table.shape[1]
    per = m // (SC_WORKERS * SC_ROWS)
    assert per * SC_WORKERS * SC_ROWS == m

    @functools.partial(pl.kernel, out_type=jax.ShapeDtypeStruct((m, w), table.dtype), mesh=_sc_mesh(),
                       scratch_types=[pltpu.VMEM((1, SC_ROWS), jnp.int32), pltpu.VMEM((SC_ROWS, w), table.dtype)])
    def gather(table_hbm, idx_hbm, out_hbm, idx_v, buf):
        worker = lax.axis_index("core") * (SC_WORKERS // 2) + lax.axis_index("subcore")

        @pl.loop(0, per)
        def _(b):
            blk = worker * per + b
            pltpu.sync_copy(idx_hbm.at[pl.ds(blk, 1)], idx_v)
            pltpu.sync_copy(table_hbm.at[idx_v.at[0]], buf)
            pltpu.sync_copy(buf, out_hbm.at[pl.ds(blk * SC_ROWS, SC_ROWS)])

    return gather(table, idx.reshape(m // SC_ROWS, SC_ROWS))


def _sc_scatter(rows, dest):
    m, w = rows.shape
    per = m // (SC_WORKERS * SC_ROWS)
    assert per * SC_WORKERS * SC_ROWS == m

    @functools.partial(pl.kernel, out_type=jax.ShapeDtypeStruct((m, w), rows.dtype), mesh=_sc_mesh(),
                       scratch_types=[pltpu.VMEM((1, SC_ROWS), jnp.int32), pltpu.VMEM((SC_ROWS, w), rows.dtype)])
    def scatter(rows_hbm, dest_hbm, out_hbm, dest_v, buf):
        worker = lax.axis_index("core") * (SC_WORKERS // 2) + lax.axis_index("subcore")

        @pl.loop(0, per)
        def _(b):
            blk = worker * per + b
            pltpu.sync_copy(dest_hbm.at[pl.ds(blk, 1)], dest_v)
            pltpu.sync_copy(rows_hbm.at[pl.ds(blk * SC_ROWS, SC_ROWS)], buf)
            pltpu.sync_copy(buf, out_hbm.at[dest_v.at[0]])

    return scatter(rows, dest.reshape(m // SC_ROWS, SC_ROWS))


COMBINE_ROWS = 256


def _combine_kernel(lo_ref, x_ref, route_ref, z_hbm, o_ref, buf, sem, *, n_rows):
    i = pl.program_id(0)
    f32, bf16 = jnp.float32, jnp.bfloat16
    lo = (lo_ref[i] // 8) * 8
    n_steps = (lo_ref[i + 1] - lo + COMBINE_ROWS - 1) // COMBINE_ROWS
    first = route_ref[:, OFF_LANE:OFF_LANE + 1]
    last = first + route_ref[:, MULT_LANE:MULT_LANE + 1]
    o_ref[...] = x_ref[...]

    def step(k, carry):
        want = lo + k * COMBINE_ROWS
        start = pl.multiple_of(jnp.minimum(want, n_rows - COMBINE_ROWS), 8)
        copy = pltpu.make_async_copy(z_hbm.at[pl.ds(start, COMBINE_ROWS)], buf, sem)
        copy.start()
        row = (start + lax.broadcasted_iota(jnp.int32, (1, COMBINE_ROWS), 1)).astype(f32)
        own = jnp.where((first <= row) & (row < last) & (row >= want.astype(f32)), 1.0, 0.0).astype(bf16)
        copy.wait()
        z = buf[...]
        z1 = z.astype(bf16)
        r1 = z - z1.astype(f32)
        z2 = r1.astype(bf16)
        z3 = (r1 - z2.astype(f32)).astype(bf16)
        o_ref[...] += (jnp.dot(own, z1, preferred_element_type=f32) + jnp.dot(own, z2, preferred_element_type=f32)
                       + jnp.dot(own, z3, preferred_element_type=f32))
        return carry

    lax.fori_loop(0, n_steps, step, 0)


def _combine(x, rows, z, tile_lo, tt):
    n = x.shape[0]
    return pl.pallas_call(
        functools.partial(_combine_kernel, n_rows=z.shape[0]),
        grid_spec=pltpu.PrefetchScalarGridSpec(
            num_scalar_prefetch=1,
            grid=(n // tt,),
            in_specs=[pl.BlockSpec((tt, D_MODEL), lambda i, lo: (i, 0)),
                      pl.BlockSpec((tt, LANES), lambda i, lo: (i, D_MODEL // LANES)),
                      pl.BlockSpec(memory_space=pl.ANY)],
            out_specs=pl.BlockSpec((tt, D_MODEL), lambda i, lo: (i, 0)),
            scratch_shapes=[pltpu.VMEM((COMBINE_ROWS, D_MODEL), jnp.float32), pltpu.SemaphoreType.DMA(())],
        ),
        out_shape=jax.ShapeDtypeStruct((n, D_MODEL), jnp.float32),
        compiler_params=pltpu.CompilerParams(dimension_semantics=("arbitrary",), vmem_limit_bytes=VMEM_LIMIT),
        name="expert_combine",
    )(tile_lo, x, rows, z)


def _final_norm_kernel(x_ref, g_ref, o_ref):
    o_ref[...] = _rms(x_ref[...], g_ref[...])


def _final_norm(x, g, tm):
    n = x.shape[0]
    return pl.pallas_call(
        _final_norm_kernel,
        grid=(n // tm,),
        in_specs=[pl.BlockSpec((tm, D_MODEL), lambda i: (i, 0)), pl.BlockSpec((1, D_MODEL), lambda i: (0, 0))],
        out_specs=pl.BlockSpec((tm, D_MODEL), lambda i: (i, 0)),
        out_shape=jax.ShapeDtypeStruct((n, D_MODEL), jnp.float32),
        name="final_norm",
    )(x, g)


def _forward(x_prompt, x_sample, g_attn, w_in, g_q_c, g_k_c, sink_b, g_out_a, g_out_b, g_out_c, w_out, g_ffn, w_router,
             w_gate, w_up, w_down, g_final, *, tm, tr, tf):
    f32, bf16 = jnp.float32, jnp.bfloat16
    shapes = (x_prompt.shape[:2], x_sample.shape[:2])
    seqs = tuple((b * l, l) for b, l in shapes)
    n_tok = [n for n, _ in seqs]
    row0 = (0, n_tok[0])
    assert all(r % l == 0 and l % tm == 0 for r, (_, l) in zip(row0, shapes))
    x = jnp.concatenate([x_prompt.reshape(-1, D_MODEL), x_sample.reshape(-1, D_MODEL)], axis=0)

    tables = _rope_tables(max(l for _, l in shapes))
    seg = jnp.asarray(np.kron(np.eye(LANES // HEAD_DIM), np.ones((HEAD_DIM, HEAD_DIM))), f32)
    pb, pc, perm_out = _out_perms()
    w_in_p = w_in[:, :, _in_perm()].astype(bf16)
    w_out_p = w_out[:, perm_out, :].astype(bf16)
    sink_p = sink_b[:, np.asarray(B_Q_ORDER)]
    tile2 = lambda g: jnp.tile(g, (1, 2))[:, None, :]
    gq, gk = tile2(g_q_c), tile2(g_k_c)
    wr_t = jnp.swapaxes(w_router, 1, 2)
    caps = [CAPACITY_FACTOR * n // N_EXPERTS for n in n_tok]
    slots = sum(caps)
    tt = 256
    groups, c0, s0 = [], 0, 0
    for n, cap in zip(n_tok, caps):
        assert n % LANES == 0 and cap % LANES == 0
        groups.append((c0, n // LANES, cap, s0))
        c0, s0 = c0 + n // LANES, s0 + cap // LANES
    groups = tuple(groups)

    for l in range(DEPTH):
        proj, a4, a16 = _in_projection(x, g_attn[l][None], w_in_p[l], tables, gq[l], gk[l], seg, seqs, tm)
        oa, ob, oc = [], [], []
        for (b, s), r0 in zip(shapes, row0):
            oa.append(_mixer_a(proj, a4, a16, batch=b, seq=s, row0=r0).reshape(b * s, A_WIDTH))
            ob.append(_mixer_b(proj, sink_p[l], batch=b, seq=s, row0=r0).reshape(b * s, B_WIDTH))
            oc.append(_mixer_c(proj, batch=b, seq=s, row0=r0).reshape(b * s, C_WIDTH))
        x, rows, aff = _out_projection(oa, ob, oc, x, w_out_p[l], g_out_a[l][None], g_out_b[l][pb][None],
                                       g_out_c[l][pc][None], g_ffn[l][None], wr_t[l], tm)
        idx, stats = _select(aff, groups, slots)
        rows = _route_rows(aff, stats, rows, tm)
        xe = _sc_gather(rows, idx.reshape(-1)).reshape(N_EXPERTS, slots, ROW_WIDTH)
        ye, dest = _expert_ffn(xe, w_gate, w_up, w_down, l, tr, tf)
        z = _sc_scatter(ye.reshape(-1, D_MODEL), dest.reshape(-1))
        tile_lo = jnp.concatenate([stats[::tt // LANES, N_EXPERTS, 0].astype(jnp.int32),
                                   jnp.full((1,), N_EXPERTS * slots, jnp.int32)])
        x = _combine(x, rows, z, tile_lo, tt)

    y = _final_norm(x, g_final[None], tm)
    return (y[:n_tok[0]].reshape(x_prompt.shape), y[n_tok[0]:].reshape(x_sample.shape))


def kernel(x_prompt, x_sample, g_attn, w_in, g_q_c, g_k_c, sink_b, g_out_a, g_out_b, g_out_c, w_out, g_ffn, w_router,
           w_gate, w_up, w_down, g_final):
    return _forward(x_prompt, x_sample, g_attn, w_in, g_q_c, g_k_c, sink_b, g_out_a, g_out_b, g_out_c, w_out, g_ffn,
                    w_router, w_gate, w_up, w_down, g_final, tm=512, tr=2048, tf=256)
```

```python
import functools

import jax
import jax.numpy as jnp
import numpy as np
from jax import lax
from jax.experimental import pallas as pl
from jax.experimental.pallas import tpu as pltpu
from jax.experimental.pallas import tpu_sc as plsc

D_MODEL = 1024
DEPTH = 4
HEAD_DIM = 64
A_HEADS = 6
A_PAIRS = ((128, 1), (512, 4), (2048, 16))
B_HEADS = 4
B_KV_HEADS = 2
B_HALF_WINDOW = 128
C_HEADS = 6
C_KV_HEADS = 2
GRID_W = 64
ROPE_THETA = 10000.0
N_EXPERTS = 16
CAPACITY_FACTOR = 2
D_FF = 2816
EPS = 1e-6
NEG_INF = -1e30

LANES = 128
A_WIDTH = A_HEADS * HEAD_DIM
B_WIDTH = B_HEADS * HEAD_DIM
C_WIDTH = C_HEADS * HEAD_DIM
IN_WIDTH = 3 * A_WIDTH + B_WIDTH + 2 * B_KV_HEADS * HEAD_DIM + C_WIDTH + 2 * C_KV_HEADS * HEAD_DIM
N_GROUPS = IN_WIDTH // LANES
A_GROUPS = 3 * A_WIDTH // LANES
QA, KA, VA, QB, KB, VB, QC, KC, VC = 0, 3, 6, 9, 11, 12, 13, 16, 17
ROPE_NONE, ROPE_1D, ROPE_AXIAL_Q, ROPE_AXIAL_K = 0, 1, 2, 3
GROUP_KIND = ([(ROPE_1D, True)] * 3 + [(ROPE_1D, False)] * 3 + [(ROPE_NONE, False)] * 3
              + [(ROPE_1D, True)] * 2 + [(ROPE_1D, False)] + [(ROPE_NONE, False)]
              + [(ROPE_AXIAL_Q, True)] * 3 + [(ROPE_AXIAL_K, False)] + [(ROPE_NONE, False)])
Q_SCALE = HEAD_DIM ** -0.5
A_DILATIONS = tuple(d for _, d in A_PAIRS)
A_HALF_WINDOW = A_PAIRS[0][0] // 2
assert all(w // 2 // d == A_HALF_WINDOW for w, d in A_PAIRS) and A_DILATIONS == (1, 4, 16)
BAND_TQ = 128
ROW_WIDTH = D_MODEL + LANES
GATE_LANE, DEST_LANE, OFF_LANE, MULT_LANE = 0, N_EXPERTS, 2 * N_EXPERTS, 2 * N_EXPERTS + 1
ONE_BITS = 0x3F800000
SC_WORKERS = 32

VMEM_LIMIT = 56 * 1024 * 1024

B_Q_ORDER = (0, 2, 1, 3)
C_Q_ORDER = (0, 3, 1, 4, 2, 5)


def _head_perm(order):
    return np.concatenate([np.arange(h * HEAD_DIM, (h + 1) * HEAD_DIM) for h in order])


def _in_perm():
    widths = [A_WIDTH] * 3 + [B_WIDTH, 128, 128, C_WIDTH, 128, 128]
    offs = np.concatenate([[0], np.cumsum(widths)])
    parts = [np.arange(offs[i], offs[i + 1]) for i in range(9)]
    parts[3] = offs[3] + _head_perm(B_Q_ORDER)
    parts[6] = offs[6] + _head_perm(C_Q_ORDER)
    return np.concatenate(parts)


def _out_perms():
    pb = _head_perm(B_Q_ORDER)
    pc = _head_perm(C_Q_ORDER)
    return pb, pc, np.concatenate([np.arange(A_WIDTH), A_WIDTH + pb, A_WIDTH + B_WIDTH + pc])


def _rope_tables(seq):
    pos = jnp.arange(seq, dtype=jnp.float32)
    inv1 = ROPE_THETA ** (-jnp.arange(0, HEAD_DIM, 2, dtype=jnp.float32) / HEAD_DIM)
    ang = pos[:, None] * inv1[None, :]
    c, s = jnp.cos(ang), jnp.sin(ang)
    cos1 = jnp.tile(jnp.concatenate([c, c], -1), (1, 2))
    sin1 = jnp.tile(jnp.concatenate([-s, s], -1), (1, 2))
    half = HEAD_DIM // 2
    inv2 = ROPE_THETA ** (-jnp.arange(0, half, 2, dtype=jnp.float32) / half)
    row = jnp.floor(pos / GRID_W)
    col = pos - row * GRID_W
    ar, ac = row[:, None] * inv2[None, :], col[:, None] * inv2[None, :]
    cr, sr, cc, sc = jnp.cos(ar), jnp.sin(ar), jnp.cos(ac), jnp.sin(ac)
    cos2 = jnp.tile(jnp.concatenate([cr, cr, cc, cc], -1), (1, 2))
    sin2 = jnp.tile(jnp.concatenate([-sr, sr, -sc, sc], -1), (1, 2))
    return cos1, sin1, cos2, sin2


def _swap_halves(x, block):
    half = block // 2
    lane = lax.broadcasted_iota(jnp.int32, x.shape, 1)
    return jnp.where(lane % block < half, pltpu.roll(x, LANES - half, 1), pltpu.roll(x, half, 1))


def _inproj_kernel(x_ref, g_ref, w_ref, cos1_ref, sin1_ref, cos2_ref, sin2_ref, gq_ref, gk_ref, seg_ref, o_ref,
                   a4_ref, a16_ref, rows_ref):
    x = x_ref[...]
    tm = x.shape[0]
    y = x * lax.rsqrt(jnp.mean(x * x, axis=-1, keepdims=True) + EPS)
    h = (y * g_ref[...]).astype(jnp.bfloat16)
    for c in range(N_GROUPS // 2):
        acc = jnp.dot(h, w_ref[:, c * 2 * LANES:(c + 1) * 2 * LANES], preferred_element_type=jnp.float32)
        for half in range(2):
            grp = 2 * c + half
            cols = slice(grp * LANES, (grp + 1) * LANES)
            a = acc[:, half * LANES:(half + 1) * LANES]
            kind, is_q = GROUP_KIND[grp]
            if kind == ROPE_1D:
                a = a * cos1_ref[...] + _swap_halves(a, HEAD_DIM) * sin1_ref[...]
            elif kind in (ROPE_AXIAL_Q, ROPE_AXIAL_K):
                gain = gq_ref[...] if kind == ROPE_AXIAL_Q else gk_ref[...]
                ss = jnp.dot(a * a, seg_ref[...], preferred_element_type=jnp.float32, precision=lax.Precision.HIGHEST)
                a = a * lax.rsqrt(ss * (1.0 / HEAD_DIM) + EPS) * gain
                a = a * cos2_ref[...] + _swap_halves(a, HEAD_DIM // 2) * sin2_ref[...]
            if is_q:
                a = a * Q_SCALE
            o_ref[:, cols] = a.astype(jnp.bfloat16)
            if grp < A_GROUPS:
                rows_ref[...] = a
                for d, ref in ((4, a4_ref), (16, a16_ref)):
                    for r in range(d):
                        ref[r, :, cols] = rows_ref[pl.ds(r, tm // d, stride=d), :].astype(jnp.bfloat16)


def _in_projection(x, g, w, tables, gq, gk, seg, seqs, tm):
    n = x.shape[0]
    (n0, l0), (n1, l1) = seqs
    t0 = n0 // tm

    def tab_map(i):
        return (jnp.where(i < t0, i % (l0 // tm), (i - t0) % (l1 // tm)), 0)

    tab_spec = pl.BlockSpec((tm, LANES), tab_map)
    const = lambda shape: pl.BlockSpec(shape, lambda i: (0, 0))
    wa = A_GROUPS * LANES
    return pl.pallas_call(
        _inproj_kernel,
        grid=(n // tm,),
        in_specs=[pl.BlockSpec((tm, D_MODEL), lambda i: (i, 0)), const((1, D_MODEL)), const((D_MODEL, IN_WIDTH)),
                  tab_spec, tab_spec, tab_spec, tab_spec, const((1, LANES)), const((1, LANES)), const((LANES, LANES))],
        out_specs=[pl.BlockSpec((tm, IN_WIDTH), lambda i: (i, 0)),
                   pl.BlockSpec((4, tm // 4, wa), lambda i: (0, i, 0)),
                   pl.BlockSpec((16, tm // 16, wa), lambda i: (0, i, 0))],
        out_shape=[jax.ShapeDtypeStruct((n, IN_WIDTH), jnp.bfloat16),
                   jax.ShapeDtypeStruct((4, n // 4, wa), jnp.bfloat16),
                   jax.ShapeDtypeStruct((16, n // 16, wa), jnp.bfloat16)],
        scratch_shapes=[pltpu.VMEM((tm, LANES), jnp.float32)],
        compiler_params=pltpu.CompilerParams(dimension_semantics=("arbitrary",), vmem_limit_bytes=VMEM_LIMIT),
        name="in_projection",
    )(x, g, w, *tables, gq, gk, seg)


def _stack_heads(q):
    lane = lax.broadcasted_iota(jnp.int32, q.shape, 1)
    zero = jnp.zeros_like(q)
    return jnp.concatenate([jnp.where(lane < HEAD_DIM, q, zero), jnp.where(lane >= HEAD_DIM, q, zero)], axis=0)


def _unstack_heads(x, tq):
    lane = lax.broadcasted_iota(jnp.int32, (tq, x.shape[1]), 1)
    return jnp.where(lane < HEAD_DIM, x[:tq], x[tq:])


def _unstack_column(col, tq):
    return _unstack_heads(jnp.broadcast_to(col, (2 * tq, LANES)), tq)


def _band_bias(tq, win, half_window):
    row = np.arange(2 * tq)[:, None] % tq
    col = np.arange(win)[None, :]
    kinds = [np.where(np.abs(row + off - col) <= half_window, 0.0, NEG_INF) for off in (0, half_window, 2 * half_window)]
    return jnp.asarray(np.stack(kinds), jnp.float32)


def _band_tile(q, kw, vw, bias, sink=None):
    s = lax.dot_general(_stack_heads(q), kw, (((1,), (1,)), ((), ())), preferred_element_type=jnp.float32) + bias
    m = jnp.max(s, axis=-1, keepdims=True)
    if sink is not None:
        m = jnp.maximum(m, sink)
    p = jnp.exp(s - m)
    den = jnp.sum(p, axis=-1, keepdims=True)
    if sink is not None:
        den = den + jnp.exp(sink - m)
    num = jnp.dot(p.astype(jnp.bfloat16), vw, preferred_element_type=jnp.float32)
    return num, m, den


def _tile_window(i, n_tiles, tq, win, half_window, seq):
    start = pl.multiple_of(jnp.clip(i * tq - half_window, 0, seq - win), 64)
    kind = jnp.where(i == 0, 0, jnp.where(i == n_tiles - 1, 2, 1))
    return start, kind


def _mixer_a_kernel(bias1_ref, bias4_ref, bias16_ref, q1_ref, k1_ref, v1_ref, q4_ref, k4_ref, v4_ref, q16_ref, k16_ref, v16_ref,
                    o_ref, m_scr, l_scr, n_scr, *, seq):
    tq, hw = BAND_TQ, A_HALF_WINDOW

    def run_tile(q_ref, k_ref, v_ref, b_ref, lead, i, ls):
        tqc = min(tq, ls)
        win = min(tqc + 2 * hw, ls)
        n_tiles = ls // tqc
        start, kind = _tile_window(i, n_tiles, tqc, win, hw, ls)
        num, m, den = _band_tile(q_ref[lead, pl.ds(i * tqc, tqc), :], k_ref[lead, pl.ds(start, win), :],
                                 v_ref[lead, pl.ds(start, win), :], b_ref[kind])
        return _unstack_heads(num, tqc), _unstack_column(m, tqc), _unstack_column(den, tqc), tqc

    def tile1(i, carry):
        num, m, den, _ = run_tile(q1_ref, k1_ref, v1_ref, bias1_ref, 0, i, seq)
        rows = pl.ds(pl.multiple_of(i * tq, tq), tq)
        m_scr[rows, :] = m
        l_scr[rows, :] = den
        n_scr[rows, :] = num
        return carry

    lax.fori_loop(0, seq // tq, tile1, 0, unroll=4)

    def merge(tiles):
        old = [(m_scr[rows, :], l_scr[rows, :], n_scr[rows, :]) for rows, _, _, _ in tiles]
        for (rows, num, m, den), (m_old, l_old, n_old) in zip(tiles, old):
            m_new = jnp.maximum(m_old, m)
            a, b = jnp.exp(m_old - m_new), jnp.exp(m - m_new)
            m_scr[rows, :] = m_new
            l_scr[rows, :] = a * l_old + b * den
            n_scr[rows, :] = a * n_old + b * num

    ls4 = seq // 4

    def tile4(i, carry):
        tiles = []
        for r in range(4):
            num, m, den, tqc = run_tile(q4_ref, k4_ref, v4_ref, bias4_ref, r, i, ls4)
            tiles.append((pl.ds(i * (tqc * 4) + r, tqc, stride=4), num, m, den))
        merge(tiles)
        return carry

    lax.fori_loop(0, ls4 // min(tq, ls4), tile4, 0)

    ls16 = seq // 16

    def class16(r2, carry):
        tiles = []
        for r in (2 * r2, 2 * r2 + 1):
            for i in range(ls16 // min(tq, ls16)):
                num, m, den, tqc = run_tile(q16_ref, k16_ref, v16_ref, bias16_ref, r, i, ls16)
                tiles.append((pl.ds(i * (tqc * 16) + r, tqc, stride=16), num, m, den))
        merge(tiles)
        return carry

    lax.fori_loop(0, 8, class16, 0)
    o_ref[0] = (n_scr[...] * (1.0 / l_scr[...])).astype(o_ref.dtype)


def _mixer_a(proj, a4, a16, *, batch, seq, row0):
    n = proj.shape[0]
    b0 = row0 // seq
    hw = A_HALF_WINDOW

    def class_bias(ls):
        tq = min(BAND_TQ, ls)
        return _band_bias(tq, min(tq + 2 * hw, ls), hw)

    biases = [class_bias(seq // d) for d in A_DILATIONS]
    view = proj.reshape(n // seq, seq, IN_WIDTH)
    nat = lambda off: pl.BlockSpec((1, seq, LANES), lambda b, g: (b0 + b, 0, off + g))
    cls = lambda d, off: pl.BlockSpec((d, seq // d, LANES), lambda b, g: (0, b0 + b, off + g))
    full = lambda a: pl.BlockSpec(a.shape, lambda b, g: (0, 0, 0))
    return pl.pallas_call(
        functools.partial(_mixer_a_kernel, seq=seq),
        grid=(batch, A_WIDTH // LANES),
        in_specs=[full(biases[0]), full(biases[1]), full(biases[2]), nat(QA), nat(KA), nat(VA), cls(4, QA), cls(4, KA), cls(4, VA),
                  cls(16, QA), cls(16, KA), cls(16, VA)],
        out_specs=pl.BlockSpec((1, seq, LANES), lambda b, g: (b, 0, g)),
        out_shape=jax.ShapeDtypeStruct((batch, seq, A_WIDTH), jnp.bfloat16),
        scratch_shapes=[pltpu.VMEM((seq, LANES), jnp.float32)] * 3,
        compiler_params=pltpu.CompilerParams(dimension_semantics=("arbitrary",) * 2, vmem_limit_bytes=VMEM_LIMIT),
        name="mixer_a",
    )(*biases, view, view, view, a4, a4, a4, a16, a16, a16)


def _mixer_b_kernel(sink_ref, bias_ref, q_ref, k_ref, v_ref, o_ref, *, seq):
    tq, hw = BAND_TQ, B_HALF_WINDOW
    win = tq + 2 * hw
    n_tiles = seq // tq
    g = pl.program_id(1)
    row = lax.broadcasted_iota(jnp.int32, (2 * tq, 1), 0)
    sink = jnp.where(row < tq, sink_ref[g], sink_ref[g + B_KV_HEADS])

    def tile(i, carry):
        start, kind = _tile_window(i, n_tiles, tq, win, hw, seq)
        rows = pl.ds(pl.multiple_of(i * tq, tq), tq)
        num, _, den = _band_tile(q_ref[0, rows, :], k_ref[0, pl.ds(start, win), :], v_ref[0, pl.ds(start, win), :],
                                 bias_ref[kind], sink)
        o_ref[0, rows, :] = _unstack_heads(num * (1.0 / den), tq).astype(o_ref.dtype)
        return carry

    lax.fori_loop(0, n_tiles, tile, 0, unroll=4)


def _mixer_b(proj, sink, *, batch, seq, row0):
    n = proj.shape[0]
    b0 = row0 // seq
    bias = _band_bias(BAND_TQ, BAND_TQ + 2 * B_HALF_WINDOW, B_HALF_WINDOW)
    view = proj.reshape(n // seq, seq, IN_WIDTH)
    return pl.pallas_call(
        functools.partial(_mixer_b_kernel, seq=seq),
        grid=(batch, B_WIDTH // LANES),
        in_specs=[pl.BlockSpec(memory_space=pltpu.SMEM), pl.BlockSpec(bias.shape, lambda b, g: (0, 0, 0)),
                  pl.BlockSpec((1, seq, LANES), lambda b, g: (b0 + b, 0, QB + g)),
                  pl.BlockSpec((1, seq, LANES), lambda b, g: (b0 + b, 0, KB)),
                  pl.BlockSpec((1, seq, LANES), lambda b, g: (b0 + b, 0, VB))],
        out_specs=pl.BlockSpec((1, seq, LANES), lambda b, g: (b, 0, g)),
        out_shape=jax.ShapeDtypeStruct((batch, seq, B_WIDTH), jnp.bfloat16),
        compiler_params=pltpu.CompilerParams(dimension_semantics=("arbitrary",) * 2, vmem_limit_bytes=VMEM_LIMIT),
        name="mixer_b",
    )(sink, bias, view, view, view)


def _mixer_c_kernel(q_ref, k_ref, v_ref, o_ref, *, tq, chunk):
    lhs = _stack_heads(q_ref[0])
    seq = k_ref.shape[1]
    m = den = acc = None
    for c in range(seq // chunk):
        keys = slice(c * chunk, (c + 1) * chunk)
        s = lax.dot_general(lhs, k_ref[0, keys, :], (((1,), (1,)), ((), ())), preferred_element_type=jnp.float32)
        m_c = jnp.max(s, axis=-1, keepdims=True)
        m_new = m_c if m is None else jnp.maximum(m, m_c)
        p = jnp.exp(s - m_new)
        den_c = jnp.sum(p, axis=-1, keepdims=True)
        acc_c = jnp.dot(p.astype(jnp.bfloat16), v_ref[0, keys, :], preferred_element_type=jnp.float32)
        if m is None:
            den, acc = den_c, acc_c
        else:
            alpha = jnp.exp(m - m_new)
            den, acc = alpha * den + den_c, alpha * acc + acc_c
        m = m_new
    o_ref[0] = _unstack_heads(acc * (1.0 / den), tq).astype(o_ref.dtype)


def _mixer_c(proj, *, batch, seq, row0, tq=256, chunk=512):
    n = proj.shape[0]
    b0 = row0 // seq
    view = proj.reshape(n // seq, seq, IN_WIDTH)
    return pl.pallas_call(
        functools.partial(_mixer_c_kernel, tq=tq, chunk=chunk),
        grid=(batch, seq // tq, C_WIDTH // LANES),
        in_specs=[
            pl.BlockSpec((1, tq, LANES), lambda b, i, g: (b0 + b, i, QC + g)),
            pl.BlockSpec((1, seq, LANES), lambda b, i, g: (b0 + b, 0, KC)),
            pl.BlockSpec((1, seq, LANES), lambda b, i, g: (b0 + b, 0, VC)),
        ],
        out_specs=pl.BlockSpec((1, tq, LANES), lambda b, i, g: (b, i, g)),
        out_shape=jax.ShapeDtypeStruct((batch, seq, C_WIDTH), jnp.bfloat16),
        compiler_params=pltpu.CompilerParams(dimension_semantics=("arbitrary",) * 3, vmem_limit_bytes=VMEM_LIMIT),
        name="mixer_c",
    )(view, view, view)


def _rms(x, g):
    return x * lax.rsqrt(jnp.mean(x * x, axis=-1, keepdims=True) + EPS) * g


def _outproj_kernel(oa0_ref, oa1_ref, ob0_ref, ob1_ref, oc0_ref, oc1_ref, x_ref, w_ref, ga_ref, gb_ref, gc_ref, gf_ref,
                    wr_ref, xo_ref, h_ref, aff_ref, *, t0):
    f32 = jnp.float32
    first = pl.program_id(0) < t0
    pick = lambda r0, r1: jnp.where(first, r0[...], r1[...]).astype(f32)
    merged = jnp.concatenate([_rms(pick(oa0_ref, oa1_ref), ga_ref[...]), _rms(pick(ob0_ref, ob1_ref), gb_ref[...]),
                              _rms(pick(oc0_ref, oc1_ref), gc_ref[...])], axis=-1).astype(jnp.bfloat16)
    xn = x_ref[...] + jnp.dot(merged, w_ref[...], preferred_element_type=f32)
    xo_ref[...] = xn
    h = _rms(xn, gf_ref[...])
    h_ref[:, :D_MODEL] = h
    h_ref[:, D_MODEL:] = jnp.zeros((h.shape[0], LANES), f32)
    logits = lax.dot_general(wr_ref[...], h, (((1,), (1,)), ((), ())), preferred_element_type=f32,
                             precision=lax.Precision.HIGHEST)
    z = jnp.exp(logits - jnp.max(logits, axis=0, keepdims=True))
    aff = z / jnp.sum(z, axis=0, keepdims=True)
    for c in range(aff.shape[1] // LANES):
        aff_ref[c] = aff[:, c * LANES:(c + 1) * LANES]


def _out_projection(oa, ob, oc, x, w, ga, gb, gc, gf, wr_t, tm):
    n = x.shape[0]
    t0 = oa[0].shape[0] // tm
    t1 = oa[1].shape[0] // tm
    rows = lambda width: pl.BlockSpec((tm, width), lambda i: (i, 0))
    rows0 = lambda width: pl.BlockSpec((tm, width), lambda i: (jnp.minimum(i, t0 - 1), 0))
    rows1 = lambda width: pl.BlockSpec((tm, width), lambda i: (jnp.clip(i - t0, 0, t1 - 1), 0))
    const = lambda shape: pl.BlockSpec(shape, lambda i: (0, 0))
    return pl.pallas_call(
        functools.partial(_outproj_kernel, t0=t0),
        grid=(n // tm,),
        in_specs=[rows0(A_WIDTH), rows1(A_WIDTH), rows0(B_WIDTH), rows1(B_WIDTH), rows0(C_WIDTH), rows1(C_WIDTH),
                  rows(D_MODEL), const((D_MODEL, D_MODEL)), const((1, A_WIDTH)), const((1, B_WIDTH)),
                  const((1, C_WIDTH)), const((1, D_MODEL)), const((N_EXPERTS, D_MODEL))],
        out_specs=[rows(D_MODEL), rows(ROW_WIDTH), pl.BlockSpec((tm // LANES, N_EXPERTS, LANES), lambda i: (i, 0, 0))],
        out_shape=[jax.ShapeDtypeStruct((n, D_MODEL), jnp.float32), jax.ShapeDtypeStruct((n, ROW_WIDTH), jnp.float32),
                   jax.ShapeDtypeStruct((n // LANES, N_EXPERTS, LANES), jnp.float32)],
        compiler_params=pltpu.CompilerParams(dimension_semantics=("arbitrary",), vmem_limit_bytes=VMEM_LIMIT),
        name="out_projection",
    )(oa[0], oa[1], ob[0], ob[1], oc[0], oc[1], x, w, ga, gb, gc, gf, wr_t)


def _ffn_kernel(x_ref, wg_ref, wu_ref, wd_ref, o_ref, dest_ref, xb_scr, gate_scr):
    e = pl.program_id(0)
    j = pl.program_id(2)

    @pl.when(j == 0)
    def _():
        xb_scr[...] = x_ref[0, :, :D_MODEL].astype(jnp.bfloat16)
        route = x_ref[0, :, D_MODEL:]
        lane = lax.broadcasted_iota(jnp.int32, route.shape, 1)
        pick = lambda k: jnp.sum(jnp.where(lane == k + e, route, 0.0), axis=-1, keepdims=True)
        gate_scr[...] = pick(GATE_LANE)
        dest_ref[0] = pick(DEST_LANE).astype(jnp.int32)
        o_ref[0] = jnp.zeros(o_ref.shape[1:], o_ref.dtype)

    x = xb_scr[...]
    hg = jnp.dot(x, wg_ref[...].astype(jnp.bfloat16), preferred_element_type=jnp.float32)
    hu = jnp.dot(x, wu_ref[...].astype(jnp.bfloat16), preferred_element_type=jnp.float32)
    act = (hg * jax.nn.sigmoid(hg) * hu).astype(jnp.bfloat16)
    o_ref[0] += jnp.dot(act, wd_ref[...].astype(jnp.bfloat16), preferred_element_type=jnp.float32)

    @pl.when(j == pl.num_programs(2) - 1)
    def _():
        o_ref[0] = o_ref[0] * gate_scr[...]


def _expert_ffn(xe, w_gate, w_up, w_down, layer, tr, tf):
    n_e, rows, _ = xe.shape
    d = D_MODEL
    d_ff = w_gate.shape[-1]
    return pl.pallas_call(
        _ffn_kernel,
        grid=(n_e, rows // tr, d_ff // tf),
        in_specs=[
            pl.BlockSpec((1, tr, ROW_WIDTH), lambda e, c, j: (e, c, 0)),
            pl.BlockSpec((None, None, d, tf), lambda e, c, j: (layer, e, 0, j)),
            pl.BlockSpec((None, None, d, tf), lambda e, c, j: (layer, e, 0, j)),
            pl.BlockSpec((None, None, tf, d), lambda e, c, j: (layer, e, j, 0)),
        ],
        out_specs=[pl.BlockSpec((1, tr, d), lambda e, c, j: (e, c, 0)),
                   pl.BlockSpec((1, tr, 1), lambda e, c, j: (e, c, 0))],
        out_shape=[jax.ShapeDtypeStruct((n_e, rows, d), jnp.float32),
                   jax.ShapeDtypeStruct((n_e, rows, 1), jnp.int32)],
        scratch_shapes=[pltpu.VMEM((tr, d), jnp.bfloat16), pltpu.VMEM((tr, 1), jnp.float32)],
        compiler_params=pltpu.CompilerParams(dimension_semantics=("arbitrary",) * 3, vmem_limit_bytes=VMEM_LIMIT),
        name="expert_ffn",
    )(xe, w_gate, w_up, w_down)


def _lane_cumsum(m, tri):
    nc, r, _ = m.shape
    flat = m.reshape(nc * r, LANES).astype(jnp.bfloat16)
    return jnp.dot(flat, tri, preferred_element_type=jnp.float32).reshape(nc, r, LANES)


def _lead_cumsum_exclusive(t):
    n = t.shape[0]
    inc, k = t, 1
    while k < n:
        inc = inc + jnp.concatenate([jnp.zeros((k,) + t.shape[1:], t.dtype), inc[:n - k]], axis=0)
        k *= 2
    return inc - t


def _token_cumsum(m, tri):
    inside = _lane_cumsum(m, tri)
    total = inside[:, :, LANES - 1:]
    return _lead_cumsum_exclusive(total), inside, total


def _select_kernel(aff_ref, tri_ref, idx_ref, stats_ref, split_scr, before_scr, through_scr, *, groups):
    f32 = jnp.float32
    tri = tri_ref[...]
    slot_base = 0
    for c0, nc, cap, s0 in groups:
        aff = aff_ref[c0:c0 + nc]
        bits = pltpu.bitcast(aff, jnp.int32)
        count = lambda mask: jnp.sum(jnp.sum(mask, axis=0, keepdims=True), axis=2, keepdims=True)

        def bisect(_, carry):
            lo, hi = carry
            mid = lo + ((hi - lo) >> 1)
            ok = count(jnp.where(bits >= mid, 1.0, 0.0)) >= cap
            return jnp.where(ok, mid, lo), jnp.where(ok, hi, mid)

        shape = (1, N_EXPERTS, 1)
        thr, _ = lax.fori_loop(0, 31, bisect, (jnp.zeros(shape, jnp.int32), jnp.full(shape, ONE_BITS + 1, jnp.int32)))
        above, tie = bits > thr, bits == thr
        tie_f = jnp.where(tie, 1.0, 0.0)
        need = cap - count(jnp.where(above, 1.0, 0.0))
        before, inside, _ = _token_cumsum(tie_f, tri)
        chosen = jnp.where(above | (tie & (before + inside - tie_f < need)), 1.0, 0.0)

        before, inside, total = _token_cumsum(chosen, tri)
        through = before + inside
        mult = jnp.sum(chosen, axis=1, keepdims=True)
        m_before, m_inside, _ = _token_cumsum(mult, tri)
        stats_ref[c0:c0 + nc, 0:N_EXPERTS, :] = chosen
        stats_ref[c0:c0 + nc, N_EXPERTS:N_EXPERTS + 1, :] = slot_base + m_before + m_inside - mult
        stats_ref[c0:c0 + nc, N_EXPERTS + 1:N_EXPERTS + 2, :] = mult
        stats_ref[c0:c0 + nc, N_EXPERTS + 2:, :] = jnp.zeros((nc, 6, LANES), f32)
        slot_base += N_EXPERTS * cap

        for e in range(N_EXPERTS):
            t_e = through[:, e, :]
            hi_digit = jnp.floor(t_e * (1.0 / 64))
            split_scr[e, 0:nc, 0:LANES] = hi_digit.astype(jnp.bfloat16)
            split_scr[e, 0:nc, LANES:] = (t_e - 64.0 * hi_digit).astype(jnp.bfloat16)
            before_scr[e, 0:nc, :] = jnp.broadcast_to(before[:, e, :], (nc, LANES))
            through_scr[e, 0:nc, :] = jnp.broadcast_to((before + total)[:, e, :], (nc, LANES))

        chunk_id = lax.broadcasted_iota(jnp.int32, (1, nc), 1).astype(f32)
        lane_id = lax.broadcasted_iota(jnp.int32, (1, LANES), 1)
        row_id = lax.broadcasted_iota(jnp.int32, (LANES, 1), 0)

        def compact(it, carry):
            e, s = it // (cap // LANES), it % (cap // LANES)
            slot_row = (s * LANES + lane_id).astype(f32)
            slot_col = (s * LANES + row_id).astype(f32)
            holds = (before_scr[e, 0:nc, :] <= slot_row) & (slot_row < through_scr[e, 0:nc, :])
            onehot = jnp.where(holds, 1.0, 0.0).T
            digits = jnp.dot(onehot.astype(jnp.bfloat16), split_scr[e, 0:nc, :], preferred_element_type=f32)
            counts = 64.0 * digits[:, :LANES] + digits[:, LANES:]
            inside_pos = jnp.sum(jnp.where(counts <= slot_col, 1.0, 0.0), axis=-1, keepdims=True)
            chunk = jnp.sum(onehot * chunk_id, axis=-1, keepdims=True)
            token = (c0 + chunk) * LANES + inside_pos
            idx_ref[e, pl.ds(s0 + s, 1), :] = jnp.broadcast_to(token, (LANES, LANES)).T[0:1, :].astype(jnp.int32)
            return carry

        lax.fori_loop(0, N_EXPERTS * (cap // LANES), compact, 0, unroll=2)


def _select(aff, groups, slots):
    n_chunks = aff.shape[0]
    nc_max = max(nc for _, nc, _, _ in groups)
    tri = jnp.asarray(np.triu(np.ones((LANES, LANES))), jnp.bfloat16)
    return pl.pallas_call(
        functools.partial(_select_kernel, groups=groups),
        out_shape=[jax.ShapeDtypeStruct((N_EXPERTS, slots // LANES, LANES), jnp.int32),
                   jax.ShapeDtypeStruct((n_chunks, 24, LANES), jnp.float32)],
        scratch_shapes=[pltpu.VMEM((N_EXPERTS, nc_max, 2 * LANES), jnp.bfloat16),
                        pltpu.VMEM((N_EXPERTS, nc_max, LANES), jnp.float32),
                        pltpu.VMEM((N_EXPERTS, nc_max, LANES), jnp.float32)],
        compiler_params=pltpu.CompilerParams(vmem_limit_bytes=VMEM_LIMIT),
        name="expert_select",
    )(aff, tri)


def _route_rows_kernel(aff_ref, stats_ref, below_ref, rows_in_ref, o_ref):
    del rows_in_ref
    n = aff_ref.shape[0]
    pad = jnp.zeros((LANES - N_EXPERTS - stats_ref.shape[1], LANES), jnp.float32)
    lane = lax.broadcasted_iota(jnp.int32, (LANES, LANES), 1)
    for c in range(n):
        t = jnp.concatenate([aff_ref[c], stats_ref[c], pad], axis=0).T
        rank = jnp.dot(t.astype(jnp.bfloat16), below_ref[...], preferred_element_type=jnp.float32)
        first = t[:, OFF_LANE:OFF_LANE + 1]
        o_ref[c * LANES:(c + 1) * LANES, :] = jnp.where((lane >= DEST_LANE) & (lane < OFF_LANE), first + rank, t)


def _route_rows(aff, stats, rows, tm):
    n = rows.shape[0]
    k = tm // LANES
    below = np.zeros((LANES, LANES))
    below[DEST_LANE:OFF_LANE, DEST_LANE:OFF_LANE] = np.triu(np.ones((N_EXPERTS, N_EXPERTS)), 1)
    return pl.pallas_call(
        _route_rows_kernel,
        grid=(n // tm,),
        in_specs=[pl.BlockSpec((k, N_EXPERTS, LANES), lambda i: (i, 0, 0)),
                  pl.BlockSpec((k, stats.shape[1], LANES), lambda i: (i, 0, 0)),
                  pl.BlockSpec((LANES, LANES), lambda i: (0, 0)),
                  pl.BlockSpec(memory_space=pl.ANY)],
        out_specs=pl.BlockSpec((tm, LANES), lambda i: (i, D_MODEL // LANES)),
        out_shape=jax.ShapeDtypeStruct(rows.shape, rows.dtype),
        input_output_aliases={3: 0},
        compiler_params=pltpu.CompilerParams(dimension_semantics=("arbitrary",)),
        name="route_rows",
    )(aff, stats, jnp.asarray(below, jnp.bfloat16), rows)


SC_ROWS = 32


def _sc_mesh():
    return plsc.VectorSubcoreMesh(core_axis_name="core", subcore_axis_name="subcore")


def _sc_gather(table, idx):
    m, w = idx.shape[0], table.shape[1]
    per = m // (SC_WORKERS * SC_ROWS)
    assert per * SC_WORKERS * SC_ROWS == m

    @functools.partial(pl.kernel, out_type=jax.ShapeDtypeStruct((m, w), table.dtype), mesh=_sc_mesh(),
                       scratch_types=[pltpu.VMEM((1, SC_ROWS), jnp.int32), pltpu.VMEM((SC_ROWS, w), table.dtype)])
    def gather(table_hbm, idx_hbm, out_hbm, idx_v, buf):
        worker = lax.axis_index("core") * (SC_WORKERS // 2) + lax.axis_index("subcore")

        @pl.loop(0, per)
        def _(b):
            blk = worker * per + b
            pltpu.sync_copy(idx_hbm.at[pl.ds(blk, 1)], idx_v)
            pltpu.sync_copy(table_hbm.at[idx_v.at[0]], buf)
            pltpu.sync_copy(buf, out_hbm.at[pl.ds(blk * SC_ROWS, SC_ROWS)])

    return gather(table, idx.reshape(m // SC_ROWS, SC_ROWS))


def _sc_scatter(rows, dest):
    m, w = rows.shape
    per = m // (SC_WORKERS * SC_ROWS)
    assert per * SC_WORKERS * SC_ROWS == m

    @functools.partial(pl.kernel, out_type=jax.ShapeDtypeStruct((m, w), rows.dtype), mesh=_sc_mesh(),
                       scratch_types=[pltpu.VMEM((1, SC_ROWS), jnp.int32), pltpu.VMEM((SC_ROWS, w), rows.dtype)])
    def scatter(rows_hbm, dest_hbm, out_hbm, dest_v, buf):
        worker = lax.axis_index("core") * (SC_WORKERS // 2) + lax.axis_index("subcore")

        @pl.loop(0, per)
        def _(b):
            blk = worker * per + b
            pltpu.sync_copy(dest_hbm.at[pl.ds(blk, 1)], dest_v)
            pltpu.sync_copy(rows_hbm.at[pl.ds(blk * SC_ROWS, SC_ROWS)], buf)
            pltpu.sync_copy(buf, out_hbm.at[dest_v.at[0]])

    return scatter(rows, dest.reshape(m // SC_ROWS, SC_ROWS))


COMBINE_ROWS = 256


def _combine_kernel(tile_ref, start_ref, want_ref, flags_ref, x_ref, route_ref, z_ref, o_ref):
    s = pl.program_id(0)
    f32, bf16 = jnp.float32, jnp.bfloat16

    @pl.when(flags_ref[s] == 1)
    def _():
        o_ref[...] = x_ref[...]

    @pl.when(flags_ref[s] != 2)
    def _():
        first = route_ref[:, OFF_LANE:OFF_LANE + 1]
        last = first + route_ref[:, MULT_LANE:MULT_LANE + 1]
        row = start_ref[s] * 8 + lax.broadcasted_iota(jnp.int32, (1, COMBINE_ROWS), 1)
        rowf = row.astype(f32)
        own = jnp.where((first <= rowf) & (rowf < last) & (row >= want_ref[s]), 1.0, 0.0).astype(bf16)
        z = z_ref[...]
        z1 = z.astype(bf16)
        r1 = z - z1.astype(f32)
        z2 = r1.astype(bf16)
        z3 = (r1 - z2.astype(f32)).astype(bf16)
        o_ref[...] += (jnp.dot(own, z1, preferred_element_type=f32) + jnp.dot(own, z2, preferred_element_type=f32)
                       + jnp.dot(own, z3, preferred_element_type=f32))


def _combine_steps(tile_lo, n_rows):
    n_tiles = tile_lo.shape[0] - 1
    n_steps_max = n_rows // COMBINE_ROWS + 2 * n_tiles + n_tiles // 16 + 1
    lo = (tile_lo[:-1] // 8) * 8
    per_tile = jnp.maximum((tile_lo[1:] - lo + COMBINE_ROWS - 1) // COMBINE_ROWS, 1)
    ends = jnp.cumsum(per_tile)
    s = jnp.arange(n_steps_max, dtype=jnp.int32)
    valid = s < ends[-1]
    tile = jnp.minimum(jnp.searchsorted(ends, s, side="right").astype(jnp.int32), n_tiles - 1)
    k = s - (ends - per_tile)[tile]
    want = lo[tile] + k * COMBINE_ROWS
    start = jnp.minimum(want, n_rows - COMBINE_ROWS)
    last_valid = ends[-1] - 1
    hold = lambda a: jnp.where(valid, a, a[last_valid])
    flags = jnp.where(valid, (k == 0).astype(jnp.int32), 2)
    return hold(tile), hold(start) // 8, want, flags, n_steps_max


def _combine(x, rows, z, tile_lo, tt):
    n = x.shape[0]
    tile, start, want, flags, n_steps = _combine_steps(tile_lo, z.shape[0])
    return pl.pallas_call(
        _combine_kernel,
        grid_spec=pltpu.PrefetchScalarGridSpec(
            num_scalar_prefetch=4,
            grid=(n_steps,),
            in_specs=[pl.BlockSpec((tt, D_MODEL), lambda s, tile, *_: (tile[s], 0)),
                      pl.BlockSpec((tt, LANES), lambda s, tile, *_: (tile[s], D_MODEL // LANES)),
                      pl.BlockSpec((pl.Element(COMBINE_ROWS), pl.Element(D_MODEL)),
                                   lambda s, tile, start, *_: (start[s] * 8, 0))],
            out_specs=pl.BlockSpec((tt, D_MODEL), lambda s, tile, *_: (tile[s], 0)),
        ),
        out_shape=jax.ShapeDtypeStruct((n, D_MODEL), jnp.float32),
        compiler_params=pltpu.CompilerParams(dimension_semantics=("arbitrary",), vmem_limit_bytes=VMEM_LIMIT),
        name="expert_combine",
    )(tile, start, want, flags, x, rows, z)


def _final_norm_kernel(x_ref, g_ref, o_ref):
    o_ref[...] = _rms(x_ref[...], g_ref[...])


def _final_norm(x, g, tm):
    n = x.shape[0]
    return pl.pallas_call(
        _final_norm_kernel,
        grid=(n // tm,),
        in_specs=[pl.BlockSpec((tm, D_MODEL), lambda i: (i, 0)), pl.BlockSpec((1, D_MODEL), lambda i: (0, 0))],
        out_specs=pl.BlockSpec((tm, D_MODEL), lambda i: (i, 0)),
        out_shape=jax.ShapeDtypeStruct((n, D_MODEL), jnp.float32),
        name="final_norm",
    )(x, g)


def _forward(x_prompt, x_sample, g_attn, w_in, g_q_c, g_k_c, sink_b, g_out_a, g_out_b, g_out_c, w_out, g_ffn, w_router,
             w_gate, w_up, w_down, g_final, *, tm, tr, tf):
    f32, bf16 = jnp.float32, jnp.bfloat16
    shapes = (x_prompt.shape[:2], x_sample.shape[:2])
    seqs = tuple((b * l, l) for b, l in shapes)
    n_tok = [n for n, _ in seqs]
    row0 = (0, n_tok[0])
    assert all(r % l == 0 and l % tm == 0 for r, (_, l) in zip(row0, shapes))
    x = jnp.concatenate([x_prompt.reshape(-1, D_MODEL), x_sample.reshape(-1, D_MODEL)], axis=0)

    tables = _rope_tables(max(l for _, l in shapes))
    seg = jnp.asarray(np.kron(np.eye(LANES // HEAD_DIM), np.ones((HEAD_DIM, HEAD_DIM))), f32)
    pb, pc, perm_out = _out_perms()
    w_in_p = w_in[:, :, _in_perm()].astype(bf16)
    w_out_p = w_out[:, perm_out, :].astype(bf16)
    sink_p = sink_b[:, np.asarray(B_Q_ORDER)]
    tile2 = lambda g: jnp.tile(g, (1, 2))[:, None, :]
    gq, gk = tile2(g_q_c), tile2(g_k_c)
    wr_t = jnp.swapaxes(w_router, 1, 2)
    caps = [CAPACITY_FACTOR * n // N_EXPERTS for n in n_tok]
    slots = sum(caps)
    tt = 256
    groups, c0, s0 = [], 0, 0
    for n, cap in zip(n_tok, caps):
        assert n % LANES == 0 and cap % LANES == 0
        groups.append((c0, n // LANES, cap, s0))
        c0, s0 = c0 + n // LANES, s0 + cap // LANES
    groups = tuple(groups)

    for l in range(DEPTH):
        proj, a4, a16 = _in_projection(x, g_attn[l][None], w_in_p[l], tables, gq[l], gk[l], seg, seqs, tm)
        oa, ob, oc = [], [], []
        for (b, s), r0 in zip(shapes, row0):
            oa.append(_mixer_a(proj, a4, a16, batch=b, seq=s, row0=r0).reshape(b * s, A_WIDTH))
            ob.append(_mixer_b(proj, sink_p[l], batch=b, seq=s, row0=r0).reshape(b * s, B_WIDTH))
            oc.append(_mixer_c(proj, batch=b, seq=s, row0=r0).reshape(b * s, C_WIDTH))
        x, rows, aff = _out_projection(oa, ob, oc, x, w_out_p[l], g_out_a[l][None], g_out_b[l][pb][None],
                                       g_out_c[l][pc][None], g_ffn[l][None], wr_t[l], tm)
        idx, stats = _select(aff, groups, slots)
        rows = _route_rows(aff, stats, rows, tm)
        xe = _sc_gather(rows, idx.reshape(-1)).reshape(N_EXPERTS, slots, ROW_WIDTH)
        ye, dest = _expert_ffn(xe, w_gate, w_up, w_down, l, tr, tf)
        z = _sc_scatter(ye.reshape(-1, D_MODEL), dest.reshape(-1))
        tile_lo = jnp.concatenate([stats[::tt // LANES, N_EXPERTS, 0].astype(jnp.int32),
                                   jnp.full((1,), N_EXPERTS * slots, jnp.int32)])
        x = _combine(x, rows, z, tile_lo, tt)

    y = _final_norm(x, g_final[None], tm)
    return (y[:n_tok[0]].reshape(x_prompt.shape), y[n_tok[0]:].reshape(x_sample.shape))


def kernel(x_prompt, x_sample, g_attn, w_in, g_q_c, g_k_c, sink_b, g_out_a, g_out_b, g_out_c, w_out, g_ffn, w_router,
           w_gate, w_up, w_down, g_final):
    return _forward(x_prompt, x_sample, g_attn, w_in, g_q_c, g_k_c, sink_b, g_out_a, g_out_b, g_out_c, w_out, g_ffn,
                    w_router, w_gate, w_up, w_down, g_final, tm=512, tr=2048, tf=256)
```

```python
import functools

import jax
import jax.numpy as jnp
import numpy as np
from jax import lax
from jax.experimental import pallas as pl
from jax.experimental.pallas import tpu as pltpu
from jax.experimental.pallas import tpu_sc as plsc

D_MODEL = 1024
DEPTH = 4
HEAD_DIM = 64
A_HEADS = 6
A_PAIRS = ((128, 1), (512, 4), (2048, 16))
B_HEADS = 4
B_KV_HEADS = 2
B_HALF_WINDOW = 128
C_HEADS = 6
C_KV_HEADS = 2
GRID_W = 64
ROPE_THETA = 10000.0
N_EXPERTS = 16
CAPACITY_FACTOR = 2
D_FF = 2816
EPS = 1e-6
NEG_INF = -1e30

LANES = 128
A_WIDTH = A_HEADS * HEAD_DIM
B_WIDTH = B_HEADS * HEAD_DIM
C_WIDTH = C_HEADS * HEAD_DIM
IN_WIDTH = 3 * A_WIDTH + B_WIDTH + 2 * B_KV_HEADS * HEAD_DIM + C_WIDTH + 2 * C_KV_HEADS * HEAD_DIM
N_GROUPS = IN_WIDTH // LANES
A_GROUPS = 3 * A_WIDTH // LANES
QA, KA, VA, QB, KB, VB, QC, KC, VC = 0, 3, 6, 9, 11, 12, 13, 16, 17
ROPE_NONE, ROPE_1D, ROPE_AXIAL_Q, ROPE_AXIAL_K = 0, 1, 2, 3
GROUP_KIND = ([(ROPE_1D, True)] * 3 + [(ROPE_1D, False)] * 3 + [(ROPE_NONE, False)] * 3
              + [(ROPE_1D, True)] * 2 + [(ROPE_1D, False)] + [(ROPE_NONE, False)]
              + [(ROPE_AXIAL_Q, True)] * 3 + [(ROPE_AXIAL_K, False)] + [(ROPE_NONE, False)])
Q_SCALE = HEAD_DIM ** -0.5
A_DILATIONS = tuple(d for _, d in A_PAIRS)
A_HALF_WINDOW = A_PAIRS[0][0] // 2
assert all(w // 2 // d == A_HALF_WINDOW for w, d in A_PAIRS) and A_DILATIONS == (1, 4, 16)
BAND_TQ = 128
ROW_WIDTH = D_MODEL + LANES
GATE_LANE, DEST_LANE, OFF_LANE, MULT_LANE = 0, N_EXPERTS, 2 * N_EXPERTS, 2 * N_EXPERTS + 1
ONE_BITS = 0x3F800000
SC_WORKERS = 32

VMEM_LIMIT = 56 * 1024 * 1024

B_Q_ORDER = (0, 2, 1, 3)
C_Q_ORDER = (0, 3, 1, 4, 2, 5)


def _head_perm(order):
    return np.concatenate([np.arange(h * HEAD_DIM, (h + 1) * HEAD_DIM) for h in order])


def _in_perm():
    widths = [A_WIDTH] * 3 + [B_WIDTH, 128, 128, C_WIDTH, 128, 128]
    offs = np.concatenate([[0], np.cumsum(widths)])
    parts = [np.arange(offs[i], offs[i + 1]) for i in range(9)]
    parts[3] = offs[3] + _head_perm(B_Q_ORDER)
    parts[6] = offs[6] + _head_perm(C_Q_ORDER)
    return np.concatenate(parts)


def _out_perms():
    pb = _head_perm(B_Q_ORDER)
    pc = _head_perm(C_Q_ORDER)
    return pb, pc, np.concatenate([np.arange(A_WIDTH), A_WIDTH + pb, A_WIDTH + B_WIDTH + pc])


def _rope_tables(seq):
    pos = jnp.arange(seq, dtype=jnp.float32)
    inv1 = ROPE_THETA ** (-jnp.arange(0, HEAD_DIM, 2, dtype=jnp.float32) / HEAD_DIM)
    ang = pos[:, None] * inv1[None, :]
    c, s = jnp.cos(ang), jnp.sin(ang)
    cos1 = jnp.tile(jnp.concatenate([c, c], -1), (1, 2))
    sin1 = jnp.tile(jnp.concatenate([-s, s], -1), (1, 2))
    half = HEAD_DIM // 2
    inv2 = ROPE_THETA ** (-jnp.arange(0, half, 2, dtype=jnp.float32) / half)
    row = jnp.floor(pos / GRID_W)
    col = pos - row * GRID_W
    ar, ac = row[:, None] * inv2[None, :], col[:, None] * inv2[None, :]
    cr, sr, cc, sc = jnp.cos(ar), jnp.sin(ar), jnp.cos(ac), jnp.sin(ac)
    cos2 = jnp.tile(jnp.concatenate([cr, cr, cc, cc], -1), (1, 2))
    sin2 = jnp.tile(jnp.concatenate([-sr, sr, -sc, sc], -1), (1, 2))
    return cos1, sin1, cos2, sin2


def _swap_halves(x, block):
    half = block // 2
    lane = lax.broadcasted_iota(jnp.int32, x.shape, 1)
    return jnp.where(lane % block < half, pltpu.roll(x, LANES - half, 1), pltpu.roll(x, half, 1))


def _inproj_kernel(x_ref, g_ref, w_ref, cos1_ref, sin1_ref, cos2_ref, sin2_ref, gq_ref, gk_ref, seg_ref, o_ref,
                   a4_ref, a16_ref, rows_ref):
    x = x_ref[...]
    tm = x.shape[0]
    y = x * lax.rsqrt(jnp.mean(x * x, axis=-1, keepdims=True) + EPS)
    h = (y * g_ref[...]).astype(jnp.bfloat16)
    for c in range(N_GROUPS // 2):
        acc = jnp.dot(h, w_ref[:, c * 2 * LANES:(c + 1) * 2 * LANES], preferred_element_type=jnp.float32)
        for half in range(2):
            grp = 2 * c + half
            cols = slice(grp * LANES, (grp + 1) * LANES)
            a = acc[:, half * LANES:(half + 1) * LANES]
            kind, is_q = GROUP_KIND[grp]
            if kind == ROPE_1D:
                a = a * cos1_ref[...] + _swap_halves(a, HEAD_DIM) * sin1_ref[...]
            elif kind in (ROPE_AXIAL_Q, ROPE_AXIAL_K):
                gain = gq_ref[...] if kind == ROPE_AXIAL_Q else gk_ref[...]
                sq = a * a
                sq_hi = sq.astype(jnp.bfloat16)
                sq_lo = (sq - sq_hi.astype(jnp.float32)).astype(jnp.bfloat16)
                ss = (jnp.dot(sq_hi, seg_ref[...], preferred_element_type=jnp.float32)
                      + jnp.dot(sq_lo, seg_ref[...], preferred_element_type=jnp.float32))
                a = a * lax.rsqrt(ss * (1.0 / HEAD_DIM) + EPS) * gain
                a = a * cos2_ref[...] + _swap_halves(a, HEAD_DIM // 2) * sin2_ref[...]
            if is_q:
                a = a * Q_SCALE
            o_ref[:, cols] = a.astype(jnp.bfloat16)
            if grp < A_GROUPS:
                rows_ref[...] = a
                for d, ref in ((4, a4_ref), (16, a16_ref)):
                    for r in range(d):
                        ref[r, :, cols] = rows_ref[pl.ds(r, tm // d, stride=d), :].astype(jnp.bfloat16)


def _in_projection(x, g, w, tables, gq, gk, seg, seqs, tm):
    n = x.shape[0]
    (n0, l0), (n1, l1) = seqs
    t0 = n0 // tm

    def tab_map(i):
        return (jnp.where(i < t0, i % (l0 // tm), (i - t0) % (l1 // tm)), 0)

    tab_spec = pl.BlockSpec((tm, LANES), tab_map)
    const = lambda shape: pl.BlockSpec(shape, lambda i: (0, 0))
    wa = A_GROUPS * LANES
    return pl.pallas_call(
        _inproj_kernel,
        grid=(n // tm,),
        in_specs=[pl.BlockSpec((tm, D_MODEL), lambda i: (i, 0)), const((1, D_MODEL)), const((D_MODEL, IN_WIDTH)),
                  tab_spec, tab_spec, tab_spec, tab_spec, const((1, LANES)), const((1, LANES)), const((LANES, LANES))],
        out_specs=[pl.BlockSpec((tm, IN_WIDTH), lambda i: (i, 0)),
                   pl.BlockSpec((4, tm // 4, wa), lambda i: (0, i, 0)),
                   pl.BlockSpec((16, tm // 16, wa), lambda i: (0, i, 0))],
        out_shape=[jax.ShapeDtypeStruct((n, IN_WIDTH), jnp.bfloat16),
                   jax.ShapeDtypeStruct((4, n // 4, wa), jnp.bfloat16),
                   jax.ShapeDtypeStruct((16, n // 16, wa), jnp.bfloat16)],
        scratch_shapes=[pltpu.VMEM((tm, LANES), jnp.float32)],
        compiler_params=pltpu.CompilerParams(dimension_semantics=("arbitrary",), vmem_limit_bytes=VMEM_LIMIT),
        name="in_projection",
    )(x, g, w, *tables, gq, gk, seg)


def _stack_heads(q):
    lane = lax.broadcasted_iota(jnp.int32, q.shape, 1)
    zero = jnp.zeros_like(q)
    return jnp.concatenate([jnp.where(lane < HEAD_DIM, q, zero), jnp.where(lane >= HEAD_DIM, q, zero)], axis=0)


def _unstack_heads(x, tq):
    lane = lax.broadcasted_iota(jnp.int32, (tq, x.shape[1]), 1)
    return jnp.where(lane < HEAD_DIM, x[:tq], x[tq:])


def _unstack_column(col, tq):
    return _unstack_heads(jnp.broadcast_to(col, (2 * tq, LANES)), tq)


def _band_bias(tq, win, half_window):
    row = np.arange(2 * tq)[:, None] % tq
    col = np.arange(win)[None, :]
    kinds = [np.where(np.abs(row + off - col) <= half_window, 0.0, NEG_INF) for off in (0, half_window, 2 * half_window)]
    return jnp.asarray(np.stack(kinds), jnp.float32)


def _band_tile(q, kw, vw, bias, sink=None):
    s = lax.dot_general(_stack_heads(q), kw, (((1,), (1,)), ((), ())), preferred_element_type=jnp.float32) + bias
    m = jnp.max(s, axis=-1, keepdims=True)
    if sink is not None:
        m = jnp.maximum(m, sink)
    p = jnp.exp(s - m)
    den = jnp.sum(p, axis=-1, keepdims=True)
    if sink is not None:
        den = den + jnp.exp(sink - m)
    num = jnp.dot(p.astype(jnp.bfloat16), vw, preferred_element_type=jnp.float32)
    return num, m, den


def _tile_window(i, n_tiles, tq, win, half_window, seq):
    start = pl.multiple_of(jnp.clip(i * tq - half_window, 0, seq - win), 64)
    kind = jnp.where(i == 0, 0, jnp.where(i == n_tiles - 1, 2, 1))
    return start, kind


def _mixer_a_kernel(bias1_ref, bias4_ref, bias16_ref, q1_ref, k1_ref, v1_ref, q4_ref, k4_ref, v4_ref, q16_ref, k16_ref, v16_ref,
                    o_ref, m_scr, l_scr, n_scr, *, seq):
    tq, hw = BAND_TQ, A_HALF_WINDOW

    def run_tile(q_ref, k_ref, v_ref, b_ref, lead, i, ls):
        tqc = min(tq, ls)
        win = min(tqc + 2 * hw, ls)
        n_tiles = ls // tqc
        start, kind = _tile_window(i, n_tiles, tqc, win, hw, ls)
        num, m, den = _band_tile(q_ref[lead, pl.ds(i * tqc, tqc), :], k_ref[lead, pl.ds(start, win), :],
                                 v_ref[lead, pl.ds(start, win), :], b_ref[kind])
        return _unstack_heads(num, tqc), _unstack_column(m, tqc), _unstack_column(den, tqc), tqc

    def tile1(i, carry):
        num, m, den, _ = run_tile(q1_ref, k1_ref, v1_ref, bias1_ref, 0, i, seq)
        rows = pl.ds(pl.multiple_of(i * tq, tq), tq)
        m_scr[rows, :] = m
        l_scr[rows, :] = den
        n_scr[rows, :] = num
        return carry

    lax.fori_loop(0, seq // tq, tile1, 0, unroll=4)

    def merge(tiles):
        old = [(m_scr[rows, :], l_scr[rows, :], n_scr[rows, :]) for rows, _, _, _ in tiles]
        for (rows, num, m, den), (m_old, l_old, n_old) in zip(tiles, old):
            m_new = jnp.maximum(m_old, m)
            a, b = jnp.exp(m_old - m_new), jnp.exp(m - m_new)
            m_scr[rows, :] = m_new
            l_scr[rows, :] = a * l_old + b * den
            n_scr[rows, :] = a * n_old + b * num

    ls4 = seq // 4

    def tile4(i, carry):
        tiles = []
        for r in range(4):
            num, m, den, tqc = run_tile(q4_ref, k4_ref, v4_ref, bias4_ref, r, i, ls4)
            tiles.append((pl.ds(i * (tqc * 4) + r, tqc, stride=4), num, m, den))
        merge(tiles)
        return carry

    lax.fori_loop(0, ls4 // min(tq, ls4), tile4, 0)

    ls16 = seq // 16

    def class16(r2, carry):
        tiles = []
        for r in (2 * r2, 2 * r2 + 1):
            for i in range(ls16 // min(tq, ls16)):
                num, m, den, tqc = run_tile(q16_ref, k16_ref, v16_ref, bias16_ref, r, i, ls16)
                tiles.append((pl.ds(i * (tqc * 16) + r, tqc, stride=16), num, m, den))
        merge(tiles)
        return carry

    lax.fori_loop(0, 8, class16, 0)
    o_ref[0] = (n_scr[...] * (1.0 / l_scr[...])).astype(o_ref.dtype)


def _mixer_a(proj, a4, a16, *, batch, seq, row0):
    n = proj.shape[0]
    b0 = row0 // seq
    hw = A_HALF_WINDOW

    def class_bias(ls):
        tq = min(BAND_TQ, ls)
        return _band_bias(tq, min(tq + 2 * hw, ls), hw)

    biases = [class_bias(seq // d) for d in A_DILATIONS]
    view = proj.reshape(n // seq, seq, IN_WIDTH)
    nat = lambda off: pl.BlockSpec((1, seq, LANES), lambda b, g: (b0 + b, 0, off + g))
    cls = lambda d, off: pl.BlockSpec((d, seq // d, LANES), lambda b, g: (0, b0 + b, off + g))
    full = lambda a: pl.BlockSpec(a.shape, lambda b, g: (0, 0, 0))
    return pl.pallas_call(
        functools.partial(_mixer_a_kernel, seq=seq),
        grid=(batch, A_WIDTH // LANES),
        in_specs=[full(biases[0]), full(biases[1]), full(biases[2]), nat(QA), nat(KA), nat(VA), cls(4, QA), cls(4, KA), cls(4, VA),
                  cls(16, QA), cls(16, KA), cls(16, VA)],
        out_specs=pl.BlockSpec((1, seq, LANES), lambda b, g: (b, 0, g)),
        out_shape=jax.ShapeDtypeStruct((batch, seq, A_WIDTH), jnp.bfloat16),
        scratch_shapes=[pltpu.VMEM((seq, LANES), jnp.float32)] * 3,
        compiler_params=pltpu.CompilerParams(dimension_semantics=("arbitrary",) * 2, vmem_limit_bytes=VMEM_LIMIT),
        name="mixer_a",
    )(*biases, view, view, view, a4, a4, a4, a16, a16, a16)


def _mixer_b_kernel(sink_ref, bias_ref, q_ref, k_ref, v_ref, o_ref, *, seq):
    tq, hw = BAND_TQ, B_HALF_WINDOW
    win = tq + 2 * hw
    n_tiles = seq // tq
    g = pl.program_id(1)
    row = lax.broadcasted_iota(jnp.int32, (2 * tq, 1), 0)
    sink = jnp.where(row < tq, sink_ref[g], sink_ref[g + B_KV_HEADS])

    def tile(i, carry):
        start, kind = _tile_window(i, n_tiles, tq, win, hw, seq)
        rows = pl.ds(pl.multiple_of(i * tq, tq), tq)
        num, _, den = _band_tile(q_ref[0, rows, :], k_ref[0, pl.ds(start, win), :], v_ref[0, pl.ds(start, win), :],
                                 bias_ref[kind], sink)
        o_ref[0, rows, :] = _unstack_heads(num * (1.0 / den), tq).astype(o_ref.dtype)
        return carry

    lax.fori_loop(0, n_tiles, tile, 0, unroll=4)


def _mixer_b(proj, sink, *, batch, seq, row0):
    n = proj.shape[0]
    b0 = row0 // seq
    bias = _band_bias(BAND_TQ, BAND_TQ + 2 * B_HALF_WINDOW, B_HALF_WINDOW)
    view = proj.reshape(n // seq, seq, IN_WIDTH)
    return pl.pallas_call(
        functools.partial(_mixer_b_kernel, seq=seq),
        grid=(batch, B_WIDTH // LANES),
        in_specs=[pl.BlockSpec(memory_space=pltpu.SMEM), pl.BlockSpec(bias.shape, lambda b, g: (0, 0, 0)),
                  pl.BlockSpec((1, seq, LANES), lambda b, g: (b0 + b, 0, QB + g)),
                  pl.BlockSpec((1, seq, LANES), lambda b, g: (b0 + b, 0, KB)),
                  pl.BlockSpec((1, seq, LANES), lambda b, g: (b0 + b, 0, VB))],
        out_specs=pl.BlockSpec((1, seq, LANES), lambda b, g: (b, 0, g)),
        out_shape=jax.ShapeDtypeStruct((batch, seq, B_WIDTH), jnp.bfloat16),
        compiler_params=pltpu.CompilerParams(dimension_semantics=("arbitrary",) * 2, vmem_limit_bytes=VMEM_LIMIT),
        name="mixer_b",
    )(sink, bias, view, view, view)


def _mixer_c_kernel(q_ref, k_ref, v_ref, o_ref, *, tq, chunk):
    lhs = _stack_heads(q_ref[0])
    seq = k_ref.shape[1]
    m = den = acc = None
    for c in range(seq // chunk):
        keys = slice(c * chunk, (c + 1) * chunk)
        s = lax.dot_general(lhs, k_ref[0, keys, :], (((1,), (1,)), ((), ())), preferred_element_type=jnp.float32)
        m_c = jnp.max(s, axis=-1, keepdims=True)
        m_new = m_c if m is None else jnp.maximum(m, m_c)
        p = jnp.exp(s - m_new)
        den_c = jnp.sum(p, axis=-1, keepdims=True)
        acc_c = jnp.dot(p.astype(jnp.bfloat16), v_ref[0, keys, :], preferred_element_type=jnp.float32)
        if m is None:
            den, acc = den_c, acc_c
        else:
            alpha = jnp.exp(m - m_new)
            den, acc = alpha * den + den_c, alpha * acc + acc_c
        m = m_new
    o_ref[0] = _unstack_heads(acc * (1.0 / den), tq).astype(o_ref.dtype)


def _mixer_c(proj, *, batch, seq, row0, tq=512, chunk=1024):
    n = proj.shape[0]
    tq, chunk = min(tq, seq), min(chunk, seq)
    b0 = row0 // seq
    view = proj.reshape(n // seq, seq, IN_WIDTH)
    return pl.pallas_call(
        functools.partial(_mixer_c_kernel, tq=tq, chunk=chunk),
        grid=(batch, seq // tq, C_WIDTH // LANES),
        in_specs=[
            pl.BlockSpec((1, tq, LANES), lambda b, i, g: (b0 + b, i, QC + g)),
            pl.BlockSpec((1, seq, LANES), lambda b, i, g: (b0 + b, 0, KC)),
            pl.BlockSpec((1, seq, LANES), lambda b, i, g: (b0 + b, 0, VC)),
        ],
        out_specs=pl.BlockSpec((1, tq, LANES), lambda b, i, g: (b, i, g)),
        out_shape=jax.ShapeDtypeStruct((batch, seq, C_WIDTH), jnp.bfloat16),
        compiler_params=pltpu.CompilerParams(dimension_semantics=("arbitrary",) * 3, vmem_limit_bytes=VMEM_LIMIT),
        name="mixer_c",
    )(view, view, view)


def _rms(x, g):
    return x * lax.rsqrt(jnp.mean(x * x, axis=-1, keepdims=True) + EPS) * g


def _outproj_kernel(oa0_ref, oa1_ref, ob0_ref, ob1_ref, oc0_ref, oc1_ref, x_ref, w_ref, ga_ref, gb_ref, gc_ref, gf_ref,
                    wr_ref, xo_ref, h_ref, aff_ref, *, t0):
    f32 = jnp.float32
    first = pl.program_id(0) < t0
    pick = lambda r0, r1: jnp.where(first, r0[...], r1[...]).astype(f32)
    merged = jnp.concatenate([_rms(pick(oa0_ref, oa1_ref), ga_ref[...]), _rms(pick(ob0_ref, ob1_ref), gb_ref[...]),
                              _rms(pick(oc0_ref, oc1_ref), gc_ref[...])], axis=-1).astype(jnp.bfloat16)
    xn = x_ref[...] + jnp.dot(merged, w_ref[...], preferred_element_type=f32)
    xo_ref[...] = xn
    h = _rms(xn, gf_ref[...])
    h_ref[:, :D_MODEL] = h
    h_ref[:, D_MODEL:] = jnp.zeros((h.shape[0], LANES), f32)
    logits = lax.dot_general(wr_ref[...], h, (((1,), (1,)), ((), ())), preferred_element_type=f32,
                             precision=lax.Precision.HIGHEST)
    z = jnp.exp(logits - jnp.max(logits, axis=0, keepdims=True))
    aff = z / jnp.sum(z, axis=0, keepdims=True)
    for c in range(aff.shape[1] // LANES):
        aff_ref[c] = aff[:, c * LANES:(c + 1) * LANES]


def _out_projection(oa, ob, oc, x, w, ga, gb, gc, gf, wr_t, tm):
    n = x.shape[0]
    t0 = oa[0].shape[0] // tm
    t1 = oa[1].shape[0] // tm
    rows = lambda width: pl.BlockSpec((tm, width), lambda i: (i, 0))
    rows0 = lambda width: pl.BlockSpec((tm, width), lambda i: (jnp.minimum(i, t0 - 1), 0))
    rows1 = lambda width: pl.BlockSpec((tm, width), lambda i: (jnp.clip(i - t0, 0, t1 - 1), 0))
    const = lambda shape: pl.BlockSpec(shape, lambda i: (0, 0))
    return pl.pallas_call(
        functools.partial(_outproj_kernel, t0=t0),
        grid=(n // tm,),
        in_specs=[rows0(A_WIDTH), rows1(A_WIDTH), rows0(B_WIDTH), rows1(B_WIDTH), rows0(C_WIDTH), rows1(C_WIDTH),
                  rows(D_MODEL), const((D_MODEL, D_MODEL)), const((1, A_WIDTH)), const((1, B_WIDTH)),
                  const((1, C_WIDTH)), const((1, D_MODEL)), const((N_EXPERTS, D_MODEL))],
        out_specs=[rows(D_MODEL), rows(ROW_WIDTH), pl.BlockSpec((tm // LANES, N_EXPERTS, LANES), lambda i: (i, 0, 0))],
        out_shape=[jax.ShapeDtypeStruct((n, D_MODEL), jnp.float32), jax.ShapeDtypeStruct((n, ROW_WIDTH), jnp.float32),
                   jax.ShapeDtypeStruct((n // LANES, N_EXPERTS, LANES), jnp.float32)],
        compiler_params=pltpu.CompilerParams(dimension_semantics=("arbitrary",), vmem_limit_bytes=VMEM_LIMIT),
        name="out_projection",
    )(oa[0], oa[1], ob[0], ob[1], oc[0], oc[1], x, w, ga, gb, gc, gf, wr_t)


def _ffn_kernel(x_ref, wg_ref, wu_ref, wd_ref, o_ref, dest_ref, xb_scr, gate_scr):
    e = pl.program_id(0)
    j = pl.program_id(2)

    @pl.when(j == 0)
    def _():
        xb_scr[...] = x_ref[0, :, :D_MODEL].astype(jnp.bfloat16)
        route = x_ref[0, :, D_MODEL:]
        lane = lax.broadcasted_iota(jnp.int32, route.shape, 1)
        pick = lambda k: jnp.sum(jnp.where(lane == k + e, route, 0.0), axis=-1, keepdims=True)
        gate_scr[...] = pick(GATE_LANE)
        dest_ref[0] = pick(DEST_LANE).astype(jnp.int32)
        o_ref[0] = jnp.zeros(o_ref.shape[1:], o_ref.dtype)

    x = xb_scr[...]
    hg = jnp.dot(x, wg_ref[...].astype(jnp.bfloat16), preferred_element_type=jnp.float32)
    hu = jnp.dot(x, wu_ref[...].astype(jnp.bfloat16), preferred_element_type=jnp.float32)
    act = (hg * jax.nn.sigmoid(hg) * hu).astype(jnp.bfloat16)
    o_ref[0] += jnp.dot(act, wd_ref[...].astype(jnp.bfloat16), preferred_element_type=jnp.float32)

    @pl.when(j == pl.num_programs(2) - 1)
    def _():
        o_ref[0] = o_ref[0] * gate_scr[...]


def _expert_ffn(xe, w_gate, w_up, w_down, layer, tr, tf):
    n_e, rows, _ = xe.shape
    d = D_MODEL
    d_ff = w_gate.shape[-1]
    return pl.pallas_call(
        _ffn_kernel,
        grid=(n_e, rows // tr, d_ff // tf),
        in_specs=[
            pl.BlockSpec((1, tr, ROW_WIDTH), lambda e, c, j: (e, c, 0)),
            pl.BlockSpec((None, None, d, tf), lambda e, c, j: (layer, e, 0, j)),
            pl.BlockSpec((None, None, d, tf), lambda e, c, j: (layer, e, 0, j)),
            pl.BlockSpec((None, None, tf, d), lambda e, c, j: (layer, e, j, 0)),
        ],
        out_specs=[pl.BlockSpec((1, tr, d), lambda e, c, j: (e, c, 0)),
                   pl.BlockSpec((1, tr, 1), lambda e, c, j: (e, c, 0))],
        out_shape=[jax.ShapeDtypeStruct((n_e, rows, d), jnp.float32),
                   jax.ShapeDtypeStruct((n_e, rows, 1), jnp.int32)],
        scratch_shapes=[pltpu.VMEM((tr, d), jnp.bfloat16), pltpu.VMEM((tr, 1), jnp.float32)],
        compiler_params=pltpu.CompilerParams(dimension_semantics=("arbitrary",) * 3, vmem_limit_bytes=VMEM_LIMIT),
        name="expert_ffn",
    )(xe, w_gate, w_up, w_down)


def _lane_cumsum(m, tri):
    nc, r, _ = m.shape
    flat = m.reshape(nc * r, LANES).astype(jnp.bfloat16)
    return jnp.dot(flat, tri, preferred_element_type=jnp.float32).reshape(nc, r, LANES)


def _lead_cumsum_exclusive(t):
    n = t.shape[0]
    inc, k = t, 1
    while k < n:
        inc = inc + jnp.concatenate([jnp.zeros((k,) + t.shape[1:], t.dtype), inc[:n - k]], axis=0)
        k *= 2
    return inc - t


def _token_cumsum(m, tri):
    inside = _lane_cumsum(m, tri)
    total = inside[:, :, LANES - 1:]
    return _lead_cumsum_exclusive(total), inside, total


def _select_kernel(aff_ref, tri_ref, idx_ref, stats_ref, split_scr, before_scr, through_scr, *, groups):
    f32 = jnp.float32
    tri = tri_ref[...]
    for c0, nc, cap, s0 in groups:
        aff = aff_ref[c0:c0 + nc]
        bits = pltpu.bitcast(aff, jnp.int32)
        count = lambda mask: jnp.sum(jnp.sum(mask, axis=0, keepdims=True), axis=2, keepdims=True)

        def bisect(_, carry):
            lo, hi = carry
            mid = lo + ((hi - lo) >> 1)
            ok = count(jnp.where(bits >= mid, 1.0, 0.0)) >= cap
            return jnp.where(ok, mid, lo), jnp.where(ok, hi, mid)

        shape = (1, N_EXPERTS, 1)
        thr, _ = lax.fori_loop(0, 31, bisect, (jnp.zeros(shape, jnp.int32), jnp.full(shape, ONE_BITS + 1, jnp.int32)))
        above, tie = bits > thr, bits == thr
        tie_f = jnp.where(tie, 1.0, 0.0)
        need = cap - count(jnp.where(above, 1.0, 0.0))
        before, inside, _ = _token_cumsum(tie_f, tri)
        chosen = jnp.where(above | (tie & (before + inside - tie_f < need)), 1.0, 0.0)

        before, inside, total = _token_cumsum(chosen, tri)
        through = before + inside
        mult = jnp.sum(chosen, axis=1, keepdims=True)
        m_before, m_inside, _ = _token_cumsum(mult, tri)
        stats_ref[c0:c0 + nc, 0:N_EXPERTS, :] = chosen
        stats_ref[c0:c0 + nc, N_EXPERTS:N_EXPERTS + 1, :] = m_before + m_inside - mult
        stats_ref[c0:c0 + nc, N_EXPERTS + 1:N_EXPERTS + 2, :] = mult
        stats_ref[c0:c0 + nc, N_EXPERTS + 2:, :] = jnp.zeros((nc, 6, LANES), f32)

        for e in range(N_EXPERTS):
            t_e = through[:, e, :]
            hi_digit = jnp.floor(t_e * (1.0 / 64))
            split_scr[e, 0:nc, 0:LANES] = hi_digit.astype(jnp.bfloat16)
            split_scr[e, 0:nc, LANES:] = (t_e - 64.0 * hi_digit).astype(jnp.bfloat16)
            before_scr[e, 0:nc, :] = jnp.broadcast_to(before[:, e, :], (nc, LANES))
            through_scr[e, 0:nc, :] = jnp.broadcast_to((before + total)[:, e, :], (nc, LANES))

        chunk_id = lax.broadcasted_iota(jnp.int32, (1, nc), 1).astype(f32)
        lane_id = lax.broadcasted_iota(jnp.int32, (1, LANES), 1)
        row_id = lax.broadcasted_iota(jnp.int32, (LANES, 1), 0)

        def compact(it, carry):
            e, s = it // (cap // LANES), it % (cap // LANES)
            slot_row = (s * LANES + lane_id).astype(f32)
            slot_col = (s * LANES + row_id).astype(f32)
            holds = (before_scr[e, 0:nc, :] <= slot_row) & (slot_row < through_scr[e, 0:nc, :])
            onehot = jnp.where(holds, 1.0, 0.0).T
            digits = jnp.dot(onehot.astype(jnp.bfloat16), split_scr[e, 0:nc, :], preferred_element_type=f32)
            counts = 64.0 * digits[:, :LANES] + digits[:, LANES:]
            inside_pos = jnp.sum(jnp.where(counts <= slot_col, 1.0, 0.0), axis=-1, keepdims=True)
            chunk = jnp.sum(onehot * chunk_id, axis=-1, keepdims=True)
            token = (c0 + chunk) * LANES + inside_pos
            idx_ref[e, pl.ds(s0 + s, 1), :] = jnp.broadcast_to(token, (LANES, LANES)).T[0:1, :].astype(jnp.int32)
            return carry

        lax.fori_loop(0, N_EXPERTS * (cap // LANES), compact, 0, unroll=2)


def _select(aff, groups, slots):
    n_chunks = aff.shape[0]
    nc_max = max(nc for _, nc, _, _ in groups)
    tri = jnp.asarray(np.triu(np.ones((LANES, LANES))), jnp.bfloat16)
    return pl.pallas_call(
        functools.partial(_select_kernel, groups=groups),
        out_shape=[jax.ShapeDtypeStruct((N_EXPERTS, slots // LANES, LANES), jnp.int32),
                   jax.ShapeDtypeStruct((n_chunks, 24, LANES), jnp.float32)],
        scratch_shapes=[pltpu.VMEM((N_EXPERTS, nc_max, 2 * LANES), jnp.bfloat16),
                        pltpu.VMEM((N_EXPERTS, nc_max, LANES), jnp.float32),
                        pltpu.VMEM((N_EXPERTS, nc_max, LANES), jnp.float32)],
        compiler_params=pltpu.CompilerParams(vmem_limit_bytes=VMEM_LIMIT),
        name="expert_select",
    )(aff, tri)


def _route_rows_kernel(aff_ref, stats_ref, below_ref, rows_in_ref, o_ref):
    del rows_in_ref
    n = aff_ref.shape[0]
    pad = jnp.zeros((LANES - N_EXPERTS - stats_ref.shape[1], LANES), jnp.float32)
    lane = lax.broadcasted_iota(jnp.int32, (LANES, LANES), 1)
    for c in range(n):
        t = jnp.concatenate([aff_ref[c], stats_ref[c], pad], axis=0).T
        rank = jnp.dot(t.astype(jnp.bfloat16), below_ref[...], preferred_element_type=jnp.float32)
        first = t[:, OFF_LANE:OFF_LANE + 1]
        o_ref[c * LANES:(c + 1) * LANES, :] = jnp.where((lane >= DEST_LANE) & (lane < OFF_LANE), first + rank, t)


def _route_rows(aff, stats, rows, tm):
    n = rows.shape[0]
    k = tm // LANES
    below = np.zeros((LANES, LANES))
    below[DEST_LANE:OFF_LANE, DEST_LANE:OFF_LANE] = np.triu(np.ones((N_EXPERTS, N_EXPERTS)), 1)
    return pl.pallas_call(
        _route_rows_kernel,
        grid=(n // tm,),
        in_specs=[pl.BlockSpec((k, N_EXPERTS, LANES), lambda i: (i, 0, 0)),
                  pl.BlockSpec((k, stats.shape[1], LANES), lambda i: (i, 0, 0)),
                  pl.BlockSpec((LANES, LANES), lambda i: (0, 0)),
                  pl.BlockSpec(memory_space=pl.ANY)],
        out_specs=pl.BlockSpec((tm, LANES), lambda i: (i, D_MODEL // LANES)),
        out_shape=jax.ShapeDtypeStruct(rows.shape, rows.dtype),
        input_output_aliases={3: 0},
        compiler_params=pltpu.CompilerParams(dimension_semantics=("arbitrary",)),
        name="route_rows",
    )(aff, stats, jnp.asarray(below, jnp.bfloat16), rows)


SC_ROWS = 32


def _sc_mesh():
    return plsc.VectorSubcoreMesh(core_axis_name="core", subcore_axis_name="subcore")


def _sc_gather(table, idx):
    m, w = idx.shape[0], table.shape[1]
    per = m // (SC_WORKERS * SC_ROWS)
    assert per * SC_WORKERS * SC_ROWS == m

    @functools.partial(pl.kernel, out_type=jax.ShapeDtypeStruct((m, w), table.dtype), mesh=_sc_mesh(),
                       scratch_types=[pltpu.VMEM((1, SC_ROWS), jnp.int32), pltpu.VMEM((SC_ROWS, w), table.dtype)])
    def gather(table_hbm, idx_hbm, out_hbm, idx_v, buf):
        worker = lax.axis_index("core") * (SC_WORKERS // 2) + lax.axis_index("subcore")

        @pl.loop(0, per)
        def _(b):
            blk = worker * per + b
            pltpu.sync_copy(idx_hbm.at[pl.ds(blk, 1)], idx_v)
            pltpu.sync_copy(table_hbm.at[idx_v.at[0]], buf)
            pltpu.sync_copy(buf, out_hbm.at[pl.ds(blk * SC_ROWS, SC_ROWS)])

    return gather(table, idx.reshape(m // SC_ROWS, SC_ROWS))


def _sc_scatter(rows, dest):
    m, w = rows.shape
    per = m // (SC_WORKERS * SC_ROWS)
    assert per * SC_WORKERS * SC_ROWS == m

    @functools.partial(pl.kernel, out_type=jax.ShapeDtypeStruct((m, w), rows.dtype), mesh=_sc_mesh(),
                       scratch_types=[pltpu.VMEM((1, SC_ROWS), jnp.int32), pltpu.VMEM((SC_ROWS, w), rows.dtype)])
    def scatter(rows_hbm, dest_hbm, out_hbm, dest_v, buf):
        worker = lax.axis_index("core") * (SC_WORKERS // 2) + lax.axis_index("subcore")

        @pl.loop(0, per)
        def _(b):
            blk = worker * per + b
            pltpu.sync_copy(dest_hbm.at[pl.ds(blk, 1)], dest_v)
            pltpu.sync_copy(rows_hbm.at[pl.ds(blk * SC_ROWS, SC_ROWS)], buf)
            pltpu.sync_copy(buf, out_hbm.at[dest_v.at[0]])

    return scatter(rows, dest.reshape(m // SC_ROWS, SC_ROWS))


COMBINE_ROWS = 256


def _combine_kernel(tile_ref, start_ref, want_ref, flags_ref, x_ref, route_ref, z_ref, o_ref):
    s = pl.program_id(0)
    f32, bf16 = jnp.float32, jnp.bfloat16

    @pl.when(flags_ref[s] == 1)
    def _():
        o_ref[...] = x_ref[...]

    @pl.when(flags_ref[s] != 2)
    def _():
        first = route_ref[:, OFF_LANE:OFF_LANE + 1]
        last = first + route_ref[:, MULT_LANE:MULT_LANE + 1]
        row = start_ref[s] * 8 + lax.broadcasted_iota(jnp.int32, (1, COMBINE_ROWS), 1)
        rowf = row.astype(f32)
        own = jnp.where((first <= rowf) & (rowf < last) & (row >= want_ref[s]), 1.0, 0.0).astype(bf16)
        z = z_ref[...]
        z1 = z.astype(bf16)
        r1 = z - z1.astype(f32)
        z2 = r1.astype(bf16)
        z3 = (r1 - z2.astype(f32)).astype(bf16)
        o_ref[...] += (jnp.dot(own, z1, preferred_element_type=f32) + jnp.dot(own, z2, preferred_element_type=f32)
                       + jnp.dot(own, z3, preferred_element_type=f32))


def _combine_steps(tile_lo, n_rows):
    n_tiles = tile_lo.shape[0] - 1
    n_steps_max = n_rows // COMBINE_ROWS + 2 * n_tiles + n_tiles // 16 + 1
    lo = (tile_lo[:-1] // 8) * 8
    per_tile = jnp.maximum((tile_lo[1:] - lo + COMBINE_ROWS - 1) // COMBINE_ROWS, 1)
    ends = jnp.cumsum(per_tile)
    s = jnp.arange(n_steps_max, dtype=jnp.int32)
    valid = s < ends[-1]
    tile = jnp.minimum(jnp.sum(ends[None, :] <= s[:, None], axis=1).astype(jnp.int32), n_tiles - 1)
    mine = tile[:, None] == jnp.arange(n_tiles, dtype=jnp.int32)[None, :]
    of_tile = lambda a: jnp.sum(jnp.where(mine, a[None, :], 0), axis=1)
    k = s - of_tile(ends - per_tile)
    want = of_tile(lo) + k * COMBINE_ROWS
    start = jnp.minimum(want, n_rows - COMBINE_ROWS)
    last_start = jnp.sum(jnp.where(s == ends[-1] - 1, start, 0))
    start = jnp.where(valid, start, last_start)
    flags = jnp.where(valid, (k == 0).astype(jnp.int32), 2)
    return tile, start // 8, want, flags, n_steps_max


def _combine(x, rows, z, tile_lo, tt, tile0):
    tile, start, want, flags, n_steps = _combine_steps(tile_lo, z.shape[0])
    return pl.pallas_call(
        _combine_kernel,
        grid_spec=pltpu.PrefetchScalarGridSpec(
            num_scalar_prefetch=4,
            grid=(n_steps,),
            in_specs=[pl.BlockSpec((tt, D_MODEL), lambda s, tile, *_: (tile0 + tile[s], 0)),
                      pl.BlockSpec((tt, LANES), lambda s, tile, *_: (tile0 + tile[s], D_MODEL // LANES)),
                      pl.BlockSpec((pl.Element(COMBINE_ROWS), pl.Element(D_MODEL)),
                                   lambda s, tile, start, *_: (start[s] * 8, 0))],
            out_specs=pl.BlockSpec((tt, D_MODEL), lambda s, tile, *_: (tile0 + tile[s], 0)),
        ),
        out_shape=jax.ShapeDtypeStruct(x.shape, jnp.float32),
        input_output_aliases={4: 0},
        compiler_params=pltpu.CompilerParams(dimension_semantics=("arbitrary",), vmem_limit_bytes=VMEM_LIMIT),
        name="expert_combine",
    )(tile, start, want, flags, x, rows, z)


def _final_norm_kernel(x_ref, g_ref, o_ref):
    o_ref[...] = _rms(x_ref[...], g_ref[...])


def _final_norm(x, g, tm):
    n = x.shape[0]
    return pl.pallas_call(
        _final_norm_kernel,
        grid=(n // tm,),
        in_specs=[pl.BlockSpec((tm, D_MODEL), lambda i: (i, 0)), pl.BlockSpec((1, D_MODEL), lambda i: (0, 0))],
        out_specs=pl.BlockSpec((tm, D_MODEL), lambda i: (i, 0)),
        out_shape=jax.ShapeDtypeStruct((n, D_MODEL), jnp.float32),
        name="final_norm",
    )(x, g)


def _forward(x_prompt, x_sample, g_attn, w_in, g_q_c, g_k_c, sink_b, g_out_a, g_out_b, g_out_c, w_out, g_ffn, w_router,
             w_gate, w_up, w_down, g_final, *, tm, tr, tf):
    f32, bf16 = jnp.float32, jnp.bfloat16
    shapes = (x_prompt.shape[:2], x_sample.shape[:2])
    seqs = tuple((b * l, l) for b, l in shapes)
    n_tok = [n for n, _ in seqs]
    row0 = (0, n_tok[0])
    assert all(r % l == 0 and l % tm == 0 for r, (_, l) in zip(row0, shapes))
    x = jnp.concatenate([x_prompt.reshape(-1, D_MODEL), x_sample.reshape(-1, D_MODEL)], axis=0)

    tables = _rope_tables(max(l for _, l in shapes))
    seg = jnp.asarray(np.kron(np.eye(LANES // HEAD_DIM), np.ones((HEAD_DIM, HEAD_DIM))), bf16)
    pb, pc, perm_out = _out_perms()
    w_in_p = w_in[:, :, _in_perm()].astype(bf16)
    w_out_p = w_out[:, perm_out, :].astype(bf16)
    sink_p = sink_b[:, np.asarray(B_Q_ORDER)]
    tile2 = lambda g: jnp.tile(g, (1, 2))[:, None, :]
    gq, gk = tile2(g_q_c), tile2(g_k_c)
    wr_t = jnp.swapaxes(w_router, 1, 2)
    caps = [CAPACITY_FACTOR * n // N_EXPERTS for n in n_tok]
    slots = sum(caps)
    tt = 256
    groups, c0, s0 = [], 0, 0
    for n, cap in zip(n_tok, caps):
        assert n % LANES == 0 and cap % LANES == 0
        groups.append((c0, n // LANES, cap, s0))
        c0, s0 = c0 + n // LANES, s0 + cap // LANES
    groups = tuple(groups)

    for l in range(DEPTH):
        proj, a4, a16 = _in_projection(x, g_attn[l][None], w_in_p[l], tables, gq[l], gk[l], seg, seqs, tm)
        oa, ob, oc = [], [], []
        for (b, s), r0 in zip(shapes, row0):
            oa.append(_mixer_a(proj, a4, a16, batch=b, seq=s, row0=r0).reshape(b * s, A_WIDTH))
            ob.append(_mixer_b(proj, sink_p[l], batch=b, seq=s, row0=r0).reshape(b * s, B_WIDTH))
            oc.append(_mixer_c(proj, batch=b, seq=s, row0=r0).reshape(b * s, C_WIDTH))
        x, rows, aff = _out_projection(oa, ob, oc, x, w_out_p[l], g_out_a[l][None], g_out_b[l][pb][None],
                                       g_out_c[l][pc][None], g_ffn[l][None], wr_t[l], tm)
        idx, stats = _select(aff, groups, slots)
        rows = _route_rows(aff, stats, rows, tm)
        xes = [_sc_gather(rows, idx[:, s0:s0 + cap // LANES].reshape(-1)).reshape(N_EXPERTS, cap, ROW_WIDTH)
               for _, _, cap, s0 in groups]
        ffn = [_expert_ffn(xe, w_gate, w_up, w_down, l, min(tr, xe.shape[1]), tf) for xe in xes]
        zs = [_sc_scatter(ye.reshape(-1, D_MODEL), dest.reshape(-1)) for ye, dest in ffn]
        for (c0, nc, cap, _), z in zip(groups, zs):
            first_slot = stats[c0:c0 + nc:tt // LANES, N_EXPERTS, 0].astype(jnp.int32)
            tile_lo = jnp.concatenate([first_slot, jnp.full((1,), N_EXPERTS * cap, jnp.int32)])
            x = _combine(x, rows, z, tile_lo, tt, c0 * LANES // tt)

    y = _final_norm(x, g_final[None], tm)
    return (y[:n_tok[0]].reshape(x_prompt.shape), y[n_tok[0]:].reshape(x_sample.shape))


def kernel(x_prompt, x_sample, g_attn, w_in, g_q_c, g_k_c, sink_b, g_out_a, g_out_b, g_out_c, w_out, g_ffn, w_router,
           w_gate, w_up, w_down, g_final):
    return _forward(x_prompt, x_sample, g_attn, w_in, g_q_c, g_k_c, sink_b, g_out_a, g_out_b, g_out_c, w_out, g_ffn,
                    w_router, w_gate, w_up, w_down, g_final, tm=512, tr=2048, tf=256)
```

```python
import functools

import jax
import jax.numpy as jnp
import numpy as np
from jax import lax
from jax.experimental import pallas as pl
from jax.experimental.pallas import tpu as pltpu
from jax.experimental.pallas import tpu_sc as plsc

D_MODEL = 1024
DEPTH = 4
HEAD_DIM = 64
A_HEADS = 6
A_PAIRS = ((128, 1), (512, 4), (2048, 16))
B_HEADS = 4
B_KV_HEADS = 2
B_HALF_WINDOW = 128
C_HEADS = 6
C_KV_HEADS = 2
GRID_W = 64
ROPE_THETA = 10000.0
N_EXPERTS = 16
CAPACITY_FACTOR = 2
D_FF = 2816
EPS = 1e-6
NEG_INF = -1e30

LANES = 128
A_WIDTH = A_HEADS * HEAD_DIM
B_WIDTH = B_HEADS * HEAD_DIM
C_WIDTH = C_HEADS * HEAD_DIM
IN_WIDTH = 3 * A_WIDTH + B_WIDTH + 2 * B_KV_HEADS * HEAD_DIM + C_WIDTH + 2 * C_KV_HEADS * HEAD_DIM
N_GROUPS = IN_WIDTH // LANES
A_GROUPS = 3 * A_WIDTH // LANES
QA, KA, VA, QB, KB, VB, QC, KC, VC = 0, 3, 6, 9, 11, 12, 13, 16, 17
ROPE_NONE, ROPE_1D, ROPE_AXIAL_Q, ROPE_AXIAL_K = 0, 1, 2, 3
GROUP_KIND = ([(ROPE_1D, True)] * 3 + [(ROPE_1D, False)] * 3 + [(ROPE_NONE, False)] * 3
              + [(ROPE_1D, True)] * 2 + [(ROPE_1D, False)] + [(ROPE_NONE, False)]
              + [(ROPE_AXIAL_Q, True)] * 3 + [(ROPE_AXIAL_K, False)] + [(ROPE_NONE, False)])
Q_SCALE = HEAD_DIM ** -0.5
A_DILATIONS = tuple(d for _, d in A_PAIRS)
A_HALF_WINDOW = A_PAIRS[0][0] // 2
assert all(w // 2 // d == A_HALF_WINDOW for w, d in A_PAIRS) and A_DILATIONS == (1, 4, 16)
BAND_TQ = 128
ROW_WIDTH = D_MODEL + LANES
GATE_LANE, DEST_LANE, OFF_LANE, MULT_LANE = 0, N_EXPERTS, 2 * N_EXPERTS, 2 * N_EXPERTS + 1
ONE_BITS = 0x3F800000
SC_WORKERS = 32

VMEM_LIMIT = 56 * 1024 * 1024

B_Q_ORDER = (0, 2, 1, 3)
C_Q_ORDER = (0, 3, 1, 4, 2, 5)


def _head_perm(order):
    return np.concatenate([np.arange(h * HEAD_DIM, (h + 1) * HEAD_DIM) for h in order])


def _in_perm():
    widths = [A_WIDTH] * 3 + [B_WIDTH, 128, 128, C_WIDTH, 128, 128]
    offs = np.concatenate([[0], np.cumsum(widths)])
    parts = [np.arange(offs[i], offs[i + 1]) for i in range(9)]
    parts[3] = offs[3] + _head_perm(B_Q_ORDER)
    parts[6] = offs[6] + _head_perm(C_Q_ORDER)
    return np.concatenate(parts)


def _out_perms():
    pb = _head_perm(B_Q_ORDER)
    pc = _head_perm(C_Q_ORDER)
    return pb, pc, np.concatenate([np.arange(A_WIDTH), A_WIDTH + pb, A_WIDTH + B_WIDTH + pc])


def _rope_tables(seq):
    pos = jnp.arange(seq, dtype=jnp.float32)
    inv1 = ROPE_THETA ** (-jnp.arange(0, HEAD_DIM, 2, dtype=jnp.float32) / HEAD_DIM)
    ang = pos[:, None] * inv1[None, :]
    c, s = jnp.cos(ang), jnp.sin(ang)
    cos1 = jnp.tile(jnp.concatenate([c, c], -1), (1, 2))
    sin1 = jnp.tile(jnp.concatenate([-s, s], -1), (1, 2))
    half = HEAD_DIM // 2
    inv2 = ROPE_THETA ** (-jnp.arange(0, half, 2, dtype=jnp.float32) / half)
    row = jnp.floor(pos / GRID_W)
    col = pos - row * GRID_W
    ar, ac = row[:, None] * inv2[None, :], col[:, None] * inv2[None, :]
    cr, sr, cc, sc = jnp.cos(ar), jnp.sin(ar), jnp.cos(ac), jnp.sin(ac)
    cos2 = jnp.tile(jnp.concatenate([cr, cr, cc, cc], -1), (1, 2))
    sin2 = jnp.tile(jnp.concatenate([-sr, sr, -sc, sc], -1), (1, 2))
    return cos1, sin1, cos2, sin2


def _swap_halves(x, block):
    half = block // 2
    lane = lax.broadcasted_iota(jnp.int32, x.shape, 1)
    return jnp.where(lane % block < half, pltpu.roll(x, LANES - half, 1), pltpu.roll(x, half, 1))


def _inproj_kernel(x_ref, g_ref, w_ref, cos1_ref, sin1_ref, cos2_ref, sin2_ref, gq_ref, gk_ref, seg_ref, o_ref,
                   a4_ref, a16_ref, rows_ref):
    x = x_ref[...]
    tm = x.shape[0]
    y = x * lax.rsqrt(jnp.mean(x * x, axis=-1, keepdims=True) + EPS)
    h = (y * g_ref[...]).astype(jnp.bfloat16)
    for c in range(N_GROUPS // 2):
        acc = jnp.dot(h, w_ref[:, c * 2 * LANES:(c + 1) * 2 * LANES], preferred_element_type=jnp.float32)
        for half in range(2):
            grp = 2 * c + half
            cols = slice(grp * LANES, (grp + 1) * LANES)
            a = acc[:, half * LANES:(half + 1) * LANES]
            kind, is_q = GROUP_KIND[grp]
            if kind == ROPE_1D:
                a = a * cos1_ref[...] + _swap_halves(a, HEAD_DIM) * sin1_ref[...]
            elif kind in (ROPE_AXIAL_Q, ROPE_AXIAL_K):
                gain = gq_ref[...] if kind == ROPE_AXIAL_Q else gk_ref[...]
                sq = a * a
                sq_hi = sq.astype(jnp.bfloat16)
                sq_lo = (sq - sq_hi.astype(jnp.float32)).astype(jnp.bfloat16)
                ss = (jnp.dot(sq_hi, seg_ref[...], preferred_element_type=jnp.float32)
                      + jnp.dot(sq_lo, seg_ref[...], preferred_element_type=jnp.float32))
                a = a * lax.rsqrt(ss * (1.0 / HEAD_DIM) + EPS) * gain
                a = a * cos2_ref[...] + _swap_halves(a, HEAD_DIM // 2) * sin2_ref[...]
            if is_q:
                a = a * Q_SCALE
            o_ref[:, cols] = a.astype(jnp.bfloat16)
            if grp < A_GROUPS:
                rows_ref[...] = a
                for d, ref in ((4, a4_ref), (16, a16_ref)):
                    for r in range(d):
                        ref[r, :, cols] = rows_ref[pl.ds(r, tm // d, stride=d), :].astype(jnp.bfloat16)


def _in_projection(x, g, w, tables, gq, gk, seg, seqs, tm):
    n = x.shape[0]
    (n0, l0), (n1, l1) = seqs
    t0 = n0 // tm

    def tab_map(i):
        return (jnp.where(i < t0, i % (l0 // tm), (i - t0) % (l1 // tm)), 0)

    tab_spec = pl.BlockSpec((tm, LANES), tab_map)
    const = lambda shape: pl.BlockSpec(shape, lambda i: (0, 0))
    wa = A_GROUPS * LANES
    return pl.pallas_call(
        _inproj_kernel,
        grid=(n // tm,),
        in_specs=[pl.BlockSpec((tm, D_MODEL), lambda i: (i, 0)), const((1, D_MODEL)), const((D_MODEL, IN_WIDTH)),
                  tab_spec, tab_spec, tab_spec, tab_spec, const((1, LANES)), const((1, LANES)), const((LANES, LANES))],
        out_specs=[pl.BlockSpec((tm, IN_WIDTH), lambda i: (i, 0)),
                   pl.BlockSpec((4, tm // 4, wa), lambda i: (0, i, 0)),
                   pl.BlockSpec((16, tm // 16, wa), lambda i: (0, i, 0))],
        out_shape=[jax.ShapeDtypeStruct((n, IN_WIDTH), jnp.bfloat16),
                   jax.ShapeDtypeStruct((4, n // 4, wa), jnp.bfloat16),
                   jax.ShapeDtypeStruct((16, n // 16, wa), jnp.bfloat16)],
        scratch_shapes=[pltpu.VMEM((tm, LANES), jnp.float32)],
        compiler_params=pltpu.CompilerParams(dimension_semantics=("arbitrary",), vmem_limit_bytes=VMEM_LIMIT),
        name="in_projection",
    )(x, g, w, *tables, gq, gk, seg)


def _stack_heads(q):
    lane = lax.broadcasted_iota(jnp.int32, q.shape, 1)
    zero = jnp.zeros_like(q)
    return jnp.concatenate([jnp.where(lane < HEAD_DIM, q, zero), jnp.where(lane >= HEAD_DIM, q, zero)], axis=0)


def _unstack_heads(x, tq):
    lane = lax.broadcasted_iota(jnp.int32, (tq, x.shape[1]), 1)
    return jnp.where(lane < HEAD_DIM, x[:tq], x[tq:])


def _unstack_column(col, tq):
    return _unstack_heads(jnp.broadcast_to(col, (2 * tq, LANES)), tq)


def _band_bias(tq, win, half_window):
    row = np.arange(2 * tq)[:, None] % tq
    col = np.arange(win)[None, :]
    kinds = [np.where(np.abs(row + off - col) <= half_window, 0.0, NEG_INF) for off in (0, half_window, 2 * half_window)]
    return jnp.asarray(np.stack(kinds), jnp.float32)


def _band_tile(q, kw, vw, bias, sink=None):
    s = lax.dot_general(_stack_heads(q), kw, (((1,), (1,)), ((), ())), preferred_element_type=jnp.float32) + bias
    m = jnp.max(s, axis=-1, keepdims=True)
    if sink is not None:
        m = jnp.maximum(m, sink)
    p = jnp.exp(s - m)
    den = jnp.sum(p, axis=-1, keepdims=True)
    if sink is not None:
        den = den + jnp.exp(sink - m)
    num = jnp.dot(p.astype(jnp.bfloat16), vw, preferred_element_type=jnp.float32)
    return num, m, den


def _tile_window(i, n_tiles, tq, win, half_window, seq):
    start = pl.multiple_of(jnp.clip(i * tq - half_window, 0, seq - win), 64)
    kind = jnp.where(i == 0, 0, jnp.where(i == n_tiles - 1, 2, 1))
    return start, kind


def _mixer_a_kernel(bias1_ref, bias4_ref, bias16_ref, q1_ref, k1_ref, v1_ref, q4_ref, k4_ref, v4_ref, q16_ref, k16_ref, v16_ref,
                    o_ref, m_scr, l_scr, n_scr, *, seq):
    tq, hw = BAND_TQ, A_HALF_WINDOW

    def run_tile(q_ref, k_ref, v_ref, b_ref, lead, i, ls):
        tqc = min(tq, ls)
        win = min(tqc + 2 * hw, ls)
        n_tiles = ls // tqc
        start, kind = _tile_window(i, n_tiles, tqc, win, hw, ls)
        num, m, den = _band_tile(q_ref[lead, pl.ds(i * tqc, tqc), :], k_ref[lead, pl.ds(start, win), :],
                                 v_ref[lead, pl.ds(start, win), :], b_ref[kind])
        return _unstack_heads(num, tqc), _unstack_column(m, tqc), _unstack_column(den, tqc), tqc

    def tile1(i, carry):
        num, m, den, _ = run_tile(q1_ref, k1_ref, v1_ref, bias1_ref, 0, i, seq)
        rows = pl.ds(pl.multiple_of(i * tq, tq), tq)
        m_scr[rows, :] = m
        l_scr[rows, :] = den
        n_scr[rows, :] = num
        return carry

    lax.fori_loop(0, seq // tq, tile1, 0, unroll=8)

    def merge(tiles):
        old = [(m_scr[rows, :], l_scr[rows, :], n_scr[rows, :]) for rows, _, _, _ in tiles]
        for (rows, num, m, den), (m_old, l_old, n_old) in zip(tiles, old):
            m_new = jnp.maximum(m_old, m)
            a, b = jnp.exp(m_old - m_new), jnp.exp(m - m_new)
            m_scr[rows, :] = m_new
            l_scr[rows, :] = a * l_old + b * den
            n_scr[rows, :] = a * n_old + b * num

    ls4 = seq // 4

    def tile4(i, carry):
        tiles = []
        for r in range(4):
            num, m, den, tqc = run_tile(q4_ref, k4_ref, v4_ref, bias4_ref, r, i, ls4)
            tiles.append((pl.ds(i * (tqc * 4) + r, tqc, stride=4), num, m, den))
        merge(tiles)
        return carry

    lax.fori_loop(0, ls4 // min(tq, ls4), tile4, 0, unroll=2)

    ls16 = seq // 16

    def class16(r2, carry):
        tiles = []
        for r in (2 * r2, 2 * r2 + 1):
            for i in range(ls16 // min(tq, ls16)):
                num, m, den, tqc = run_tile(q16_ref, k16_ref, v16_ref, bias16_ref, r, i, ls16)
                tiles.append((pl.ds(i * (tqc * 16) + r, tqc, stride=16), num, m, den))
        merge(tiles)
        return carry

    lax.fori_loop(0, 8, class16, 0, unroll=2)
    o_ref[0] = (n_scr[...] * (1.0 / l_scr[...])).astype(o_ref.dtype)


def _mixer_a(proj, a4, a16, *, batch, seq, row0):
    n = proj.shape[0]
    b0 = row0 // seq
    hw = A_HALF_WINDOW

    def class_bias(ls):
        tq = min(BAND_TQ, ls)
        return _band_bias(tq, min(tq + 2 * hw, ls), hw)

    biases = [class_bias(seq // d) for d in A_DILATIONS]
    view = proj.reshape(n // seq, seq, IN_WIDTH)
    nat = lambda off: pl.BlockSpec((1, seq, LANES), lambda b, g: (b0 + b, 0, off + g))
    cls = lambda d, off: pl.BlockSpec((d, seq // d, LANES), lambda b, g: (0, b0 + b, off + g))
    full = lambda a: pl.BlockSpec(a.shape, lambda b, g: (0, 0, 0))
    return pl.pallas_call(
        functools.partial(_mixer_a_kernel, seq=seq),
        grid=(batch, A_WIDTH // LANES),
        in_specs=[full(biases[0]), full(biases[1]), full(biases[2]), nat(QA), nat(KA), nat(VA), cls(4, QA), cls(4, KA), cls(4, VA),
                  cls(16, QA), cls(16, KA), cls(16, VA)],
        out_specs=pl.BlockSpec((1, seq, LANES), lambda b, g: (b, 0, g)),
        out_shape=jax.ShapeDtypeStruct((batch, seq, A_WIDTH), jnp.bfloat16),
        scratch_shapes=[pltpu.VMEM((seq, LANES), jnp.float32)] * 3,
        compiler_params=pltpu.CompilerParams(dimension_semantics=("arbitrary",) * 2, vmem_limit_bytes=VMEM_LIMIT),
        name="mixer_a",
    )(*biases, view, view, view, a4, a4, a4, a16, a16, a16)


def _mixer_b_kernel(sink_ref, bias_ref, q_ref, k_ref, v_ref, o_ref, *, seq):
    tq, hw = BAND_TQ, B_HALF_WINDOW
    win = tq + 2 * hw
    n_tiles = seq // tq
    g = pl.program_id(1)
    row = lax.broadcasted_iota(jnp.int32, (2 * tq, 1), 0)
    sink = jnp.where(row < tq, sink_ref[g], sink_ref[g + B_KV_HEADS])

    def tile(i, carry):
        start, kind = _tile_window(i, n_tiles, tq, win, hw, seq)
        rows = pl.ds(pl.multiple_of(i * tq, tq), tq)
        num, _, den = _band_tile(q_ref[0, rows, :], k_ref[0, pl.ds(start, win), :], v_ref[0, pl.ds(start, win), :],
                                 bias_ref[kind], sink)
        o_ref[0, rows, :] = _unstack_heads(num * (1.0 / den), tq).astype(o_ref.dtype)
        return carry

    lax.fori_loop(0, n_tiles, tile, 0, unroll=8)


def _mixer_b(proj, sink, *, batch, seq, row0):
    n = proj.shape[0]
    b0 = row0 // seq
    bias = _band_bias(BAND_TQ, BAND_TQ + 2 * B_HALF_WINDOW, B_HALF_WINDOW)
    view = proj.reshape(n // seq, seq, IN_WIDTH)
    return pl.pallas_call(
        functools.partial(_mixer_b_kernel, seq=seq),
        grid=(batch, B_WIDTH // LANES),
        in_specs=[pl.BlockSpec(memory_space=pltpu.SMEM), pl.BlockSpec(bias.shape, lambda b, g: (0, 0, 0)),
                  pl.BlockSpec((1, seq, LANES), lambda b, g: (b0 + b, 0, QB + g)),
                  pl.BlockSpec((1, seq, LANES), lambda b, g: (b0 + b, 0, KB)),
                  pl.BlockSpec((1, seq, LANES), lambda b, g: (b0 + b, 0, VB))],
        out_specs=pl.BlockSpec((1, seq, LANES), lambda b, g: (b, 0, g)),
        out_shape=jax.ShapeDtypeStruct((batch, seq, B_WIDTH), jnp.bfloat16),
        compiler_params=pltpu.CompilerParams(dimension_semantics=("arbitrary",) * 2, vmem_limit_bytes=VMEM_LIMIT),
        name="mixer_b",
    )(sink, bias, view, view, view)


def _mixer_c_kernel(q_ref, k_ref, v_ref, o_ref, *, tq, chunk):
    lhs = _stack_heads(q_ref[0])
    seq = k_ref.shape[1]
    m = den = acc = None
    for c in range(seq // chunk):
        keys = slice(c * chunk, (c + 1) * chunk)
        s = lax.dot_general(lhs, k_ref[0, keys, :], (((1,), (1,)), ((), ())), preferred_element_type=jnp.float32)
        m_c = jnp.max(s, axis=-1, keepdims=True)
        m_new = m_c if m is None else jnp.maximum(m, m_c)
        p = jnp.exp(s - m_new)
        den_c = jnp.sum(p, axis=-1, keepdims=True)
        acc_c = jnp.dot(p.astype(jnp.bfloat16), v_ref[0, keys, :], preferred_element_type=jnp.float32)
        if m is None:
            den, acc = den_c, acc_c
        else:
            alpha = jnp.exp(m - m_new)
            den, acc = alpha * den + den_c, alpha * acc + acc_c
        m = m_new
    o_ref[0] = _unstack_heads(acc * (1.0 / den), tq).astype(o_ref.dtype)


def _mixer_c(proj, *, batch, seq, row0, tq=512, chunk=1024):
    n = proj.shape[0]
    tq, chunk = min(tq, seq), min(chunk, seq)
    b0 = row0 // seq
    view = proj.reshape(n // seq, seq, IN_WIDTH)
    return pl.pallas_call(
        functools.partial(_mixer_c_kernel, tq=tq, chunk=chunk),
        grid=(batch, seq // tq, C_WIDTH // LANES),
        in_specs=[
            pl.BlockSpec((1, tq, LANES), lambda b, i, g: (b0 + b, i, QC + g)),
            pl.BlockSpec((1, seq, LANES), lambda b, i, g: (b0 + b, 0, KC)),
            pl.BlockSpec((1, seq, LANES), lambda b, i, g: (b0 + b, 0, VC)),
        ],
        out_specs=pl.BlockSpec((1, tq, LANES), lambda b, i, g: (b, i, g)),
        out_shape=jax.ShapeDtypeStruct((batch, seq, C_WIDTH), jnp.bfloat16),
        compiler_params=pltpu.CompilerParams(dimension_semantics=("arbitrary",) * 3, vmem_limit_bytes=VMEM_LIMIT),
        name="mixer_c",
    )(view, view, view)


def _rms(x, g):
    return x * lax.rsqrt(jnp.mean(x * x, axis=-1, keepdims=True) + EPS) * g


def _outproj_kernel(oa0_ref, oa1_ref, ob0_ref, ob1_ref, oc0_ref, oc1_ref, x_ref, w_ref, ga_ref, gb_ref, gc_ref, gf_ref,
                    wr_ref, xo_ref, h_ref, aff_ref, *, t0):
    f32 = jnp.float32
    first = pl.program_id(0) < t0
    pick = lambda r0, r1: jnp.where(first, r0[...], r1[...]).astype(f32)
    merged = jnp.concatenate([_rms(pick(oa0_ref, oa1_ref), ga_ref[...]), _rms(pick(ob0_ref, ob1_ref), gb_ref[...]),
                              _rms(pick(oc0_ref, oc1_ref), gc_ref[...])], axis=-1).astype(jnp.bfloat16)
    xn = x_ref[...] + jnp.dot(merged, w_ref[...], preferred_element_type=f32)
    xo_ref[...] = xn
    h = _rms(xn, gf_ref[...])
    h_ref[:, :D_MODEL] = h
    h_ref[:, D_MODEL:] = jnp.zeros((h.shape[0], LANES), f32)
    logits = lax.dot_general(wr_ref[...], h, (((1,), (1,)), ((), ())), preferred_element_type=f32,
                             precision=lax.Precision.HIGHEST)
    z = jnp.exp(logits - jnp.max(logits, axis=0, keepdims=True))
    aff = z / jnp.sum(z, axis=0, keepdims=True)
    for c in range(aff.shape[1] // LANES):
        aff_ref[c] = aff[:, c * LANES:(c + 1) * LANES]


def _out_projection(oa, ob, oc, x, w, ga, gb, gc, gf, wr_t, tm):
    n = x.shape[0]
    t0 = oa[0].shape[0] // tm
    t1 = oa[1].shape[0] // tm
    rows = lambda width: pl.BlockSpec((tm, width), lambda i: (i, 0))
    rows0 = lambda width: pl.BlockSpec((tm, width), lambda i: (jnp.minimum(i, t0 - 1), 0))
    rows1 = lambda width: pl.BlockSpec((tm, width), lambda i: (jnp.clip(i - t0, 0, t1 - 1), 0))
    const = lambda shape: pl.BlockSpec(shape, lambda i: (0, 0))
    return pl.pallas_call(
        functools.partial(_outproj_kernel, t0=t0),
        grid=(n // tm,),
        in_specs=[rows0(A_WIDTH), rows1(A_WIDTH), rows0(B_WIDTH), rows1(B_WIDTH), rows0(C_WIDTH), rows1(C_WIDTH),
                  rows(D_MODEL), const((D_MODEL, D_MODEL)), const((1, A_WIDTH)), const((1, B_WIDTH)),
                  const((1, C_WIDTH)), const((1, D_MODEL)), const((N_EXPERTS, D_MODEL))],
        out_specs=[rows(D_MODEL), rows(ROW_WIDTH), pl.BlockSpec((tm // LANES, N_EXPERTS, LANES), lambda i: (i, 0, 0))],
        out_shape=[jax.ShapeDtypeStruct((n, D_MODEL), jnp.float32), jax.ShapeDtypeStruct((n, ROW_WIDTH), jnp.float32),
                   jax.ShapeDtypeStruct((n // LANES, N_EXPERTS, LANES), jnp.float32)],
        compiler_params=pltpu.CompilerParams(dimension_semantics=("arbitrary",), vmem_limit_bytes=VMEM_LIMIT),
        name="out_projection",
    )(oa[0], oa[1], ob[0], ob[1], oc[0], oc[1], x, w, ga, gb, gc, gf, wr_t)


def _ffn_kernel(x_ref, wg_ref, wu_ref, wd_ref, o_ref, dest_ref, xb_scr, gate_scr):
    e = pl.program_id(0)
    j = pl.program_id(2)

    @pl.when(j == 0)
    def _():
        xb_scr[...] = x_ref[0, :, :D_MODEL].astype(jnp.bfloat16)
        route = x_ref[0, :, D_MODEL:]
        lane = lax.broadcasted_iota(jnp.int32, route.shape, 1)
        pick = lambda k: jnp.sum(jnp.where(lane == k + e, route, 0.0), axis=-1, keepdims=True)
        gate_scr[...] = pick(GATE_LANE)
        dest_ref[0] = pick(DEST_LANE).astype(jnp.int32)
        o_ref[0] = jnp.zeros(o_ref.shape[1:], o_ref.dtype)

    x = xb_scr[...]
    hg = jnp.dot(x, wg_ref[...].astype(jnp.bfloat16), preferred_element_type=jnp.float32)
    hu = jnp.dot(x, wu_ref[...].astype(jnp.bfloat16), preferred_element_type=jnp.float32)
    act = (hg * jax.nn.sigmoid(hg) * hu).astype(jnp.bfloat16)
    o_ref[0] += jnp.dot(act, wd_ref[...].astype(jnp.bfloat16), preferred_element_type=jnp.float32)

    @pl.when(j == pl.num_programs(2) - 1)
    def _():
        o_ref[0] = o_ref[0] * gate_scr[...]


def _expert_ffn(xe, w_gate, w_up, w_down, layer, tr, tf):
    n_e, rows, _ = xe.shape
    d = D_MODEL
    d_ff = w_gate.shape[-1]
    return pl.pallas_call(
        _ffn_kernel,
        grid=(n_e, rows // tr, d_ff // tf),
        in_specs=[
            pl.BlockSpec((1, tr, ROW_WIDTH), lambda e, c, j: (e, c, 0)),
            pl.BlockSpec((None, None, d, tf), lambda e, c, j: (layer, e, 0, j)),
            pl.BlockSpec((None, None, d, tf), lambda e, c, j: (layer, e, 0, j)),
            pl.BlockSpec((None, None, tf, d), lambda e, c, j: (layer, e, j, 0)),
        ],
        out_specs=[pl.BlockSpec((1, tr, d), lambda e, c, j: (e, c, 0)),
                   pl.BlockSpec((1, tr, 1), lambda e, c, j: (e, c, 0))],
        out_shape=[jax.ShapeDtypeStruct((n_e, rows, d), jnp.float32),
                   jax.ShapeDtypeStruct((n_e, rows, 1), jnp.int32)],
        scratch_shapes=[pltpu.VMEM((tr, d), jnp.bfloat16), pltpu.VMEM((tr, 1), jnp.float32)],
        compiler_params=pltpu.CompilerParams(dimension_semantics=("arbitrary",) * 3, vmem_limit_bytes=VMEM_LIMIT),
        name="expert_ffn",
    )(xe, w_gate, w_up, w_down)


def _lane_cumsum(m, tri):
    nc, r, _ = m.shape
    flat = m.reshape(nc * r, LANES).astype(jnp.bfloat16)
    return jnp.dot(flat, tri, preferred_element_type=jnp.float32).reshape(nc, r, LANES)


def _lead_cumsum_exclusive(t):
    n = t.shape[0]
    inc, k = t, 1
    while k < n:
        inc = inc + jnp.concatenate([jnp.zeros((k,) + t.shape[1:], t.dtype), inc[:n - k]], axis=0)
        k *= 2
    return inc - t


def _token_cumsum(m, tri):
    inside = _lane_cumsum(m, tri)
    total = inside[:, :, LANES - 1:]
    return _lead_cumsum_exclusive(total), inside, total


def _select_kernel(aff_ref, tri_ref, idx_ref, stats_ref, split_scr, before_scr, through_scr, *, groups):
    f32 = jnp.float32
    tri = tri_ref[...]
    for c0, nc, cap, s0 in groups:
        aff = aff_ref[c0:c0 + nc]
        bits = pltpu.bitcast(aff, jnp.int32)
        count = lambda mask: jnp.sum(jnp.sum(mask, axis=0, keepdims=True), axis=2, keepdims=True)

        def bisect(_, carry):
            lo, hi = carry
            mid = lo + ((hi - lo) >> 1)
            ok = count(jnp.where(bits >= mid, 1.0, 0.0)) >= cap
            return jnp.where(ok, mid, lo), jnp.where(ok, hi, mid)

        shape = (1, N_EXPERTS, 1)
        thr, _ = lax.fori_loop(0, 31, bisect, (jnp.zeros(shape, jnp.int32), jnp.full(shape, ONE_BITS + 1, jnp.int32)))
        above, tie = bits > thr, bits == thr
        tie_f = jnp.where(tie, 1.0, 0.0)
        need = cap - count(jnp.where(above, 1.0, 0.0))
        before, inside, _ = _token_cumsum(tie_f, tri)
        chosen = jnp.where(above | (tie & (before + inside - tie_f < need)), 1.0, 0.0)

        before, inside, total = _token_cumsum(chosen, tri)
        through = before + inside
        mult = jnp.sum(chosen, axis=1, keepdims=True)
        m_before, m_inside, _ = _token_cumsum(mult, tri)
        stats_ref[c0:c0 + nc, 0:N_EXPERTS, :] = chosen
        stats_ref[c0:c0 + nc, N_EXPERTS:N_EXPERTS + 1, :] = m_before + m_inside - mult
        stats_ref[c0:c0 + nc, N_EXPERTS + 1:N_EXPERTS + 2, :] = mult
        stats_ref[c0:c0 + nc, N_EXPERTS + 2:, :] = jnp.zeros((nc, 6, LANES), f32)

        for e in range(N_EXPERTS):
            t_e = through[:, e, :]
            hi_digit = jnp.floor(t_e * (1.0 / 64))
            split_scr[e, 0:nc, 0:LANES] = hi_digit.astype(jnp.bfloat16)
            split_scr[e, 0:nc, LANES:] = (t_e - 64.0 * hi_digit).astype(jnp.bfloat16)
            before_scr[e, 0:nc, :] = jnp.broadcast_to(before[:, e, :], (nc, LANES))
            through_scr[e, 0:nc, :] = jnp.broadcast_to((before + total)[:, e, :], (nc, LANES))

        chunk_id = lax.broadcasted_iota(jnp.int32, (1, nc), 1).astype(f32)
        lane_id = lax.broadcasted_iota(jnp.int32, (1, LANES), 1)
        row_id = lax.broadcasted_iota(jnp.int32, (LANES, 1), 0)

        def compact(it, carry):
            e, s = it // (cap // LANES), it % (cap // LANES)
            slot_row = (s * LANES + lane_id).astype(f32)
            slot_col = (s * LANES + row_id).astype(f32)
            holds = (before_scr[e, 0:nc, :] <= slot_row) & (slot_row < through_scr[e, 0:nc, :])
            onehot = jnp.where(holds, 1.0, 0.0).T
            digits = jnp.dot(onehot.astype(jnp.bfloat16), split_scr[e, 0:nc, :], preferred_element_type=f32)
            counts = 64.0 * digits[:, :LANES] + digits[:, LANES:]
            inside_pos = jnp.sum(jnp.where(counts <= slot_col, 1.0, 0.0), axis=-1, keepdims=True)
            chunk = jnp.sum(onehot * chunk_id, axis=-1, keepdims=True)
            token = (c0 + chunk) * LANES + inside_pos
            idx_ref[e, pl.ds(s0 + s, 1), :] = jnp.broadcast_to(token, (LANES, LANES)).T[0:1, :].astype(jnp.int32)
            return carry

        lax.fori_loop(0, N_EXPERTS * (cap // LANES), compact, 0, unroll=4)


def _select(aff, groups, slots):
    n_chunks = aff.shape[0]
    nc_max = max(nc for _, nc, _, _ in groups)
    tri = jnp.asarray(np.triu(np.ones((LANES, LANES))), jnp.bfloat16)
    return pl.pallas_call(
        functools.partial(_select_kernel, groups=groups),
        out_shape=[jax.ShapeDtypeStruct((N_EXPERTS, slots // LANES, LANES), jnp.int32),
                   jax.ShapeDtypeStruct((n_chunks, 24, LANES), jnp.float32)],
        scratch_shapes=[pltpu.VMEM((N_EXPERTS, nc_max, 2 * LANES), jnp.bfloat16),
                        pltpu.VMEM((N_EXPERTS, nc_max, LANES), jnp.float32),
                        pltpu.VMEM((N_EXPERTS, nc_max, LANES), jnp.float32)],
        compiler_params=pltpu.CompilerParams(vmem_limit_bytes=VMEM_LIMIT),
        name="expert_select",
    )(aff, tri)


def _route_rows_kernel(aff_ref, stats_ref, below_ref, rows_in_ref, o_ref):
    del rows_in_ref
    n = aff_ref.shape[0]
    pad = jnp.zeros((LANES - N_EXPERTS - stats_ref.shape[1], LANES), jnp.float32)
    lane = lax.broadcasted_iota(jnp.int32, (LANES, LANES), 1)
    for c in range(n):
        t = jnp.concatenate([aff_ref[c], stats_ref[c], pad], axis=0).T
        rank = jnp.dot(t.astype(jnp.bfloat16), below_ref[...], preferred_element_type=jnp.float32)
        first = t[:, OFF_LANE:OFF_LANE + 1]
        o_ref[c * LANES:(c + 1) * LANES, :] = jnp.where((lane >= DEST_LANE) & (lane < OFF_LANE), first + rank, t)


def _route_rows(aff, stats, rows, tm):
    n = rows.shape[0]
    k = tm // LANES
    below = np.zeros((LANES, LANES))
    below[DEST_LANE:OFF_LANE, DEST_LANE:OFF_LANE] = np.triu(np.ones((N_EXPERTS, N_EXPERTS)), 1)
    return pl.pallas_call(
        _route_rows_kernel,
        grid=(n // tm,),
        in_specs=[pl.BlockSpec((k, N_EXPERTS, LANES), lambda i: (i, 0, 0)),
                  pl.BlockSpec((k, stats.shape[1], LANES), lambda i: (i, 0, 0)),
                  pl.BlockSpec((LANES, LANES), lambda i: (0, 0)),
                  pl.BlockSpec(memory_space=pl.ANY)],
        out_specs=pl.BlockSpec((tm, LANES), lambda i: (i, D_MODEL // LANES)),
        out_shape=jax.ShapeDtypeStruct(rows.shape, rows.dtype),
        input_output_aliases={3: 0},
        compiler_params=pltpu.CompilerParams(dimension_semantics=("arbitrary",)),
        name="route_rows",
    )(aff, stats, jnp.asarray(below, jnp.bfloat16), rows)


SC_ROWS = 32


def _sc_mesh():
    return plsc.VectorSubcoreMesh(core_axis_name="core", subcore_axis_name="subcore")


def _sc_gather(table, idx):
    m, w = idx.shape[0], table.shape[1]
    per = m // (SC_WORKERS * SC_ROWS)
    assert per * SC_WORKERS * SC_ROWS == m

    @functools.partial(pl.kernel, out_type=jax.ShapeDtypeStruct((m, w), table.dtype), mesh=_sc_mesh(),
                       scratch_types=[pltpu.VMEM((1, SC_ROWS), jnp.int32), pltpu.VMEM((SC_ROWS, w), table.dtype)])
    def gather(table_hbm, idx_hbm, out_hbm, idx_v, buf):
        worker = lax.axis_index("core") * (SC_WORKERS // 2) + lax.axis_index("subcore")

        @pl.loop(0, per)
        def _(b):
            blk = worker * per + b
            pltpu.sync_copy(idx_hbm.at[pl.ds(blk, 1)], idx_v)
            pltpu.sync_copy(table_hbm.at[idx_v.at[0]], buf)
            pltpu.sync_copy(buf, out_hbm.at[pl.ds(blk * SC_ROWS, SC_ROWS)])

    return gather(table, idx.reshape(m // SC_ROWS, SC_ROWS))


def _sc_scatter(rows, dest):
    m, w = rows.shape
    per = m // (SC_WORKERS * SC_ROWS)
    assert per * SC_WORKERS * SC_ROWS == m

    @functools.partial(pl.kernel, out_type=jax.ShapeDtypeStruct((m, w), rows.dtype), mesh=_sc_mesh(),
                       scratch_types=[pltpu.VMEM((1, SC_ROWS), jnp.int32), pltpu.VMEM((SC_ROWS, w), rows.dtype)])
    def scatter(rows_hbm, dest_hbm, out_hbm, dest_v, buf):
        worker = lax.axis_index("core") * (SC_WORKERS // 2) + lax.axis_index("subcore")

        @pl.loop(0, per)
        def _(b):
            blk = worker * per + b
            pltpu.sync_copy(dest_hbm.at[pl.ds(blk, 1)], dest_v)
            pltpu.sync_copy(rows_hbm.at[pl.ds(blk * SC_ROWS, SC_ROWS)], buf)
            pltpu.sync_copy(buf, out_hbm.at[dest_v.at[0]])

    return scatter(rows, dest.reshape(m // SC_ROWS, SC_ROWS))


COMBINE_ROWS = 256


def _combine_kernel(tile_ref, start_ref, want_ref, flags_ref, x_ref, route_ref, z_ref, o_ref):
    s = pl.program_id(0)
    f32, bf16 = jnp.float32, jnp.bfloat16

    @pl.when(flags_ref[s] == 1)
    def _():
        o_ref[...] = x_ref[...]

    @pl.when(flags_ref[s] != 2)
    def _():
        first = route_ref[:, OFF_LANE:OFF_LANE + 1]
        last = first + route_ref[:, MULT_LANE:MULT_LANE + 1]
        row = start_ref[s] * 8 + lax.broadcasted_iota(jnp.int32, (1, COMBINE_ROWS), 1)
        rowf = row.astype(f32)
        own = jnp.where((first <= rowf) & (rowf < last) & (row >= want_ref[s]), 1.0, 0.0).astype(bf16)
        z = z_ref[...]
        z1 = z.astype(bf16)
        r1 = z - z1.astype(f32)
        z2 = r1.astype(bf16)
        z3 = (r1 - z2.astype(f32)).astype(bf16)
        o_ref[...] += (jnp.dot(own, z1, preferred_element_type=f32) + jnp.dot(own, z2, preferred_element_type=f32)
                       + jnp.dot(own, z3, preferred_element_type=f32))


def _combine_steps(tile_lo, n_rows):
    n_tiles = tile_lo.shape[0] - 1
    n_steps_max = n_rows // COMBINE_ROWS + 2 * n_tiles + n_tiles // 16 + 1
    lo = (tile_lo[:-1] // 8) * 8
    per_tile = jnp.maximum((tile_lo[1:] - lo + COMBINE_ROWS - 1) // COMBINE_ROWS, 1)
    ends = jnp.cumsum(per_tile)
    s = jnp.arange(n_steps_max, dtype=jnp.int32)
    valid = s < ends[-1]
    tile = jnp.minimum(jnp.sum(ends[None, :] <= s[:, None], axis=1).astype(jnp.int32), n_tiles - 1)
    mine = tile[:, None] == jnp.arange(n_tiles, dtype=jnp.int32)[None, :]
    of_tile = lambda a: jnp.sum(jnp.where(mine, a[None, :], 0), axis=1)
    k = s - of_tile(ends - per_tile)
    want = of_tile(lo) + k * COMBINE_ROWS
    start = jnp.minimum(want, n_rows - COMBINE_ROWS)
    last_start = jnp.sum(jnp.where(s == ends[-1] - 1, start, 0))
    start = jnp.where(valid, start, last_start)
    flags = jnp.where(valid, (k == 0).astype(jnp.int32), 2)
    return tile, start // 8, want, flags, n_steps_max


def _combine(x, rows, z, tile_lo, tt, tile0):
    tile, start, want, flags, n_steps = _combine_steps(tile_lo, z.shape[0])
    return pl.pallas_call(
        _combine_kernel,
        grid_spec=pltpu.PrefetchScalarGridSpec(
            num_scalar_prefetch=4,
            grid=(n_steps,),
            in_specs=[pl.BlockSpec((tt, D_MODEL), lambda s, tile, *_: (tile0 + tile[s], 0)),
                      pl.BlockSpec((tt, LANES), lambda s, tile, *_: (tile0 + tile[s], D_MODEL // LANES)),
                      pl.BlockSpec((pl.Element(COMBINE_ROWS), pl.Element(D_MODEL)),
                                   lambda s, tile, start, *_: (start[s] * 8, 0))],
            out_specs=pl.BlockSpec((tt, D_MODEL), lambda s, tile, *_: (tile0 + tile[s], 0)),
        ),
        out_shape=jax.ShapeDtypeStruct(x.shape, jnp.float32),
        input_output_aliases={4: 0},
        compiler_params=pltpu.CompilerParams(dimension_semantics=("arbitrary",), vmem_limit_bytes=VMEM_LIMIT),
        name="expert_combine",
    )(tile, start, want, flags, x, rows, z)


def _final_norm_kernel(x_ref, g_ref, o0_ref, o1_ref, *, t0):
    y = _rms(x_ref[...], g_ref[...])
    i = pl.program_id(0)

    @pl.when(i < t0)
    def _():
        o0_ref[...] = y

    @pl.when(i >= t0)
    def _():
        o1_ref[...] = y


def _final_norm(x, g, n0, tm):
    n = x.shape[0]
    t0, t1 = n0 // tm, (n - n0) // tm
    return pl.pallas_call(
        functools.partial(_final_norm_kernel, t0=t0),
        grid=(n // tm,),
        in_specs=[pl.BlockSpec((tm, D_MODEL), lambda i: (i, 0)), pl.BlockSpec((1, D_MODEL), lambda i: (0, 0))],
        out_specs=[pl.BlockSpec((tm, D_MODEL), lambda i: (jnp.minimum(i, t0 - 1), 0)),
                   pl.BlockSpec((tm, D_MODEL), lambda i: (jnp.clip(i - t0, 0, t1 - 1), 0))],
        out_shape=[jax.ShapeDtypeStruct((n0, D_MODEL), jnp.float32), jax.ShapeDtypeStruct((n - n0, D_MODEL), jnp.float32)],
        compiler_params=pltpu.CompilerParams(dimension_semantics=("arbitrary",)),
        name="final_norm",
    )(x, g)


def _forward(x_prompt, x_sample, g_attn, w_in, g_q_c, g_k_c, sink_b, g_out_a, g_out_b, g_out_c, w_out, g_ffn, w_router,
             w_gate, w_up, w_down, g_final, *, tm, tr, tf):
    f32, bf16 = jnp.float32, jnp.bfloat16
    shapes = (x_prompt.shape[:2], x_sample.shape[:2])
    seqs = tuple((b * l, l) for b, l in shapes)
    n_tok = [n for n, _ in seqs]
    row0 = (0, n_tok[0])
    assert all(r % l == 0 and l % tm == 0 for r, (_, l) in zip(row0, shapes))
    x = jnp.concatenate([x_prompt.reshape(-1, D_MODEL), x_sample.reshape(-1, D_MODEL)], axis=0)

    tables = _rope_tables(max(l for _, l in shapes))
    seg = jnp.asarray(np.kron(np.eye(LANES // HEAD_DIM), np.ones((HEAD_DIM, HEAD_DIM))), bf16)
    pb, pc, perm_out = _out_perms()
    w_in_p = w_in[:, :, _in_perm()].astype(bf16)
    w_out_p = w_out[:, perm_out, :].astype(bf16)
    sink_p = sink_b[:, np.asarray(B_Q_ORDER)]
    tile2 = lambda g: jnp.tile(g, (1, 2))[:, None, :]
    gq, gk = tile2(g_q_c), tile2(g_k_c)
    wr_t = jnp.swapaxes(w_router, 1, 2)
    caps = [CAPACITY_FACTOR * n // N_EXPERTS for n in n_tok]
    slots = sum(caps)
    tt = 256
    groups, c0, s0 = [], 0, 0
    for n, cap in zip(n_tok, caps):
        assert n % LANES == 0 and cap % LANES == 0
        groups.append((c0, n // LANES, cap, s0))
        c0, s0 = c0 + n // LANES, s0 + cap // LANES
    groups = tuple(groups)

    for l in range(DEPTH):
        proj, a4, a16 = _in_projection(x, g_attn[l][None], w_in_p[l], tables, gq[l], gk[l], seg, seqs, tm)
        oa, ob, oc = [], [], []
        for (b, s), r0 in zip(shapes, row0):
            oa.append(_mixer_a(proj, a4, a16, batch=b, seq=s, row0=r0).reshape(b * s, A_WIDTH))
            ob.append(_mixer_b(proj, sink_p[l], batch=b, seq=s, row0=r0).reshape(b * s, B_WIDTH))
            oc.append(_mixer_c(proj, batch=b, seq=s, row0=r0).reshape(b * s, C_WIDTH))
        x, rows, aff = _out_projection(oa, ob, oc, x, w_out_p[l], g_out_a[l][None], g_out_b[l][pb][None],
                                       g_out_c[l][pc][None], g_ffn[l][None], wr_t[l], tm)
        idx, stats = _select(aff, groups, slots)
        rows = _route_rows(aff, stats, rows, tm)
        xes = [_sc_gather(rows, idx[:, s0:s0 + cap // LANES].reshape(-1)).reshape(N_EXPERTS, cap, ROW_WIDTH)
               for _, _, cap, s0 in groups]
        ffn = [_expert_ffn(xe, w_gate, w_up, w_down, l, min(tr, xe.shape[1]), tf) for xe in xes]
        zs = [_sc_scatter(ye.reshape(-1, D_MODEL), dest.reshape(-1)) for ye, dest in ffn]
        for (c0, nc, cap, _), z in zip(groups, zs):
            first_slot = stats[c0:c0 + nc:tt // LANES, N_EXPERTS, 0].astype(jnp.int32)
            tile_lo = jnp.concatenate([first_slot, jnp.full((1,), N_EXPERTS * cap, jnp.int32)])
            x = _combine(x, rows, z, tile_lo, tt, c0 * LANES // tt)

    y0, y1 = _final_norm(x, g_final[None], n_tok[0], tm)
    return (y0.reshape(x_prompt.shape), y1.reshape(x_sample.shape))


def kernel(x_prompt, x_sample, g_attn, w_in, g_q_c, g_k_c, sink_b, g_out_a, g_out_b, g_out_c, w_out, g_ffn, w_router,
           w_gate, w_up, w_down, g_final):
    return _forward(x_prompt, x_sample, g_attn, w_in, g_q_c, g_k_c, sink_b, g_out_a, g_out_b, g_out_c, w_out, g_ffn,
                    w_router, w_gate, w_up, w_down, g_final, tm=512, tr=2048, tf=256)
```

```python
import functools

import jax
import jax.numpy as jnp
import numpy as np
from jax import lax
from jax.experimental import pallas as pl
from jax.experimental.pallas import tpu as pltpu
from jax.experimental.pallas import tpu_sc as plsc

D_MODEL = 1024
DEPTH = 4
HEAD_DIM = 64
A_HEADS = 6
A_PAIRS = ((128, 1), (512, 4), (2048, 16))
B_HEADS = 4
B_KV_HEADS = 2
B_HALF_WINDOW = 128
C_HEADS = 6
C_KV_HEADS = 2
GRID_W = 64
ROPE_THETA = 10000.0
N_EXPERTS = 16
CAPACITY_FACTOR = 2
D_FF = 2816
EPS = 1e-6
NEG_INF = -1e30

LANES = 128
A_WIDTH = A_HEADS * HEAD_DIM
B_WIDTH = B_HEADS * HEAD_DIM
C_WIDTH = C_HEADS * HEAD_DIM
IN_WIDTH = 3 * A_WIDTH + B_WIDTH + 2 * B_KV_HEADS * HEAD_DIM + C_WIDTH + 2 * C_KV_HEADS * HEAD_DIM
N_GROUPS = IN_WIDTH // LANES
A_GROUPS = 3 * A_WIDTH // LANES
QA, KA, VA, QB, KB, VB, QC, KC, VC = 0, 3, 6, 9, 11, 12, 13, 16, 17
ROPE_NONE, ROPE_1D, ROPE_AXIAL_Q, ROPE_AXIAL_K = 0, 1, 2, 3
GROUP_KIND = ([(ROPE_1D, True)] * 3 + [(ROPE_1D, False)] * 3 + [(ROPE_NONE, False)] * 3
              + [(ROPE_1D, True)] * 2 + [(ROPE_1D, False)] + [(ROPE_NONE, False)]
              + [(ROPE_AXIAL_Q, True)] * 3 + [(ROPE_AXIAL_K, False)] + [(ROPE_NONE, False)])
Q_SCALE = HEAD_DIM ** -0.5
A_DILATIONS = tuple(d for _, d in A_PAIRS)
A_HALF_WINDOW = A_PAIRS[0][0] // 2
assert all(w // 2 // d == A_HALF_WINDOW for w, d in A_PAIRS) and A_DILATIONS == (1, 4, 16)
BAND_TQ = 128
ROW_WIDTH = D_MODEL + LANES
GATE_LANE, DEST_LANE, OFF_LANE, MULT_LANE = 0, N_EXPERTS, 2 * N_EXPERTS, 2 * N_EXPERTS + 1
ONE_BITS = 0x3F800000
SC_WORKERS = 32

VMEM_LIMIT = 56 * 1024 * 1024

B_Q_ORDER = (0, 2, 1, 3)
C_Q_ORDER = (0, 3, 1, 4, 2, 5)


def _head_perm(order):
    return np.concatenate([np.arange(h * HEAD_DIM, (h + 1) * HEAD_DIM) for h in order])


def _in_perm():
    widths = [A_WIDTH] * 3 + [B_WIDTH, 128, 128, C_WIDTH, 128, 128]
    offs = np.concatenate([[0], np.cumsum(widths)])
    parts = [np.arange(offs[i], offs[i + 1]) for i in range(9)]
    parts[3] = offs[3] + _head_perm(B_Q_ORDER)
    parts[6] = offs[6] + _head_perm(C_Q_ORDER)
    return np.concatenate(parts)


def _out_perms():
    pb = _head_perm(B_Q_ORDER)
    pc = _head_perm(C_Q_ORDER)
    return pb, pc, np.concatenate([np.arange(A_WIDTH), A_WIDTH + pb, A_WIDTH + B_WIDTH + pc])


def _rope_tables(seq):
    pos = jnp.arange(seq, dtype=jnp.float32)
    inv1 = ROPE_THETA ** (-jnp.arange(0, HEAD_DIM, 2, dtype=jnp.float32) / HEAD_DIM)
    ang = pos[:, None] * inv1[None, :]
    c, s = jnp.cos(ang), jnp.sin(ang)
    cos1 = jnp.tile(jnp.concatenate([c, c], -1), (1, 2))
    sin1 = jnp.tile(jnp.concatenate([-s, s], -1), (1, 2))
    half = HEAD_DIM // 2
    inv2 = ROPE_THETA ** (-jnp.arange(0, half, 2, dtype=jnp.float32) / half)
    row = jnp.floor(pos / GRID_W)
    col = pos - row * GRID_W
    ar, ac = row[:, None] * inv2[None, :], col[:, None] * inv2[None, :]
    cr, sr, cc, sc = jnp.cos(ar), jnp.sin(ar), jnp.cos(ac), jnp.sin(ac)
    cos2 = jnp.tile(jnp.concatenate([cr, cr, cc, cc], -1), (1, 2))
    sin2 = jnp.tile(jnp.concatenate([-sr, sr, -sc, sc], -1), (1, 2))
    return cos1, sin1, cos2, sin2


def _swap_halves(x, block):
    half = block // 2
    lane = lax.broadcasted_iota(jnp.int32, x.shape, 1)
    return jnp.where(lane % block < half, pltpu.roll(x, LANES - half, 1), pltpu.roll(x, half, 1))


def _inproj_kernel(x_ref, g_ref, w_ref, cos1_ref, sin1_ref, cos2_ref, sin2_ref, gq_ref, gk_ref, seg_ref, o_ref,
                   a4_ref, a16_ref, rows_ref):
    x = x_ref[...]
    tm = x.shape[0]
    y = x * lax.rsqrt(jnp.mean(x * x, axis=-1, keepdims=True) + EPS)
    h = (y * g_ref[...]).astype(jnp.bfloat16)
    for c in range(N_GROUPS // 2):
        acc = jnp.dot(h, w_ref[:, c * 2 * LANES:(c + 1) * 2 * LANES], preferred_element_type=jnp.float32)
        for half in range(2):
            grp = 2 * c + half
            cols = slice(grp * LANES, (grp + 1) * LANES)
            a = acc[:, half * LANES:(half + 1) * LANES]
            kind, is_q = GROUP_KIND[grp]
            if kind == ROPE_1D:
                a = a * cos1_ref[...] + _swap_halves(a, HEAD_DIM) * sin1_ref[...]
            elif kind in (ROPE_AXIAL_Q, ROPE_AXIAL_K):
                gain = gq_ref[...] if kind == ROPE_AXIAL_Q else gk_ref[...]
                sq = a * a
                sq_hi = sq.astype(jnp.bfloat16)
                sq_lo = (sq - sq_hi.astype(jnp.float32)).astype(jnp.bfloat16)
                ss = (jnp.dot(sq_hi, seg_ref[...], preferred_element_type=jnp.float32)
                      + jnp.dot(sq_lo, seg_ref[...], preferred_element_type=jnp.float32))
                a = a * lax.rsqrt(ss * (1.0 / HEAD_DIM) + EPS) * gain
                a = a * cos2_ref[...] + _swap_halves(a, HEAD_DIM // 2) * sin2_ref[...]
            if is_q:
                a = a * Q_SCALE
            o_ref[:, cols] = a.astype(jnp.bfloat16)
            if grp < A_GROUPS:
                rows_ref[grp] = a
    for d, ref in ((4, a4_ref), (16, a16_ref)):
        for r in range(d):
            for grp in range(A_GROUPS):
                ref[r, :, grp * LANES:(grp + 1) * LANES] = (
                    rows_ref[grp, pl.ds(r, tm // d, stride=d), :].astype(jnp.bfloat16))


def _in_projection(x, g, w, tables, gq, gk, seg, seq, tm):
    n = x.shape[0]
    tab_spec = pl.BlockSpec((tm, LANES), lambda i: (i % (seq // tm), 0))
    const = lambda shape: pl.BlockSpec(shape, lambda i: (0, 0))
    wa = A_GROUPS * LANES
    return pl.pallas_call(
        _inproj_kernel,
        grid=(n // tm,),
        in_specs=[pl.BlockSpec((tm, D_MODEL), lambda i: (i, 0)), const((1, D_MODEL)), const((D_MODEL, IN_WIDTH)),
                  tab_spec, tab_spec, tab_spec, tab_spec, const((1, LANES)), const((1, LANES)), const((LANES, LANES))],
        out_specs=[pl.BlockSpec((tm, IN_WIDTH), lambda i: (i, 0)),
                   pl.BlockSpec((4, tm // 4, wa), lambda i: (0, i, 0)),
                   pl.BlockSpec((16, tm // 16, wa), lambda i: (0, i, 0))],
        out_shape=[jax.ShapeDtypeStruct((n, IN_WIDTH), jnp.bfloat16),
                   jax.ShapeDtypeStruct((4, n // 4, wa), jnp.bfloat16),
                   jax.ShapeDtypeStruct((16, n // 16, wa), jnp.bfloat16)],
        scratch_shapes=[pltpu.VMEM((A_GROUPS, tm, LANES), jnp.float32)],
        compiler_params=pltpu.CompilerParams(dimension_semantics=("arbitrary",), vmem_limit_bytes=VMEM_LIMIT),
        name="in_projection",
    )(x, g, w, *tables, gq, gk, seg)


def _stack_heads(q):
    lane = lax.broadcasted_iota(jnp.int32, q.shape, 1)
    zero = jnp.zeros_like(q)
    return jnp.concatenate([jnp.where(lane < HEAD_DIM, q, zero), jnp.where(lane >= HEAD_DIM, q, zero)], axis=0)


def _unstack_heads(x, tq):
    lane = lax.broadcasted_iota(jnp.int32, (tq, x.shape[1]), 1)
    return jnp.where(lane < HEAD_DIM, x[:tq], x[tq:])


def _unstack_column(col, tq):
    return _unstack_heads(jnp.broadcast_to(col, (2 * tq, LANES)), tq)


def _band_bias(tq, win, half_window):
    row = np.arange(2 * tq)[:, None] % tq
    col = np.arange(win)[None, :]
    kinds = [np.where(np.abs(row + off - col) <= half_window, 0.0, NEG_INF) for off in (0, half_window, 2 * half_window)]
    return jnp.asarray(np.stack(kinds), jnp.float32)


def _band_tile(q, kw, vw, bias, sink=None):
    s = lax.dot_general(_stack_heads(q), kw, (((1,), (1,)), ((), ())), preferred_element_type=jnp.float32) + bias
    m = jnp.max(s, axis=-1, keepdims=True)
    if sink is not None:
        m = jnp.maximum(m, sink)
    p = jnp.exp(s - m)
    den = jnp.sum(p, axis=-1, keepdims=True)
    if sink is not None:
        den = den + jnp.exp(sink - m)
    num = jnp.dot(p.astype(jnp.bfloat16), vw, preferred_element_type=jnp.float32)
    return num, m, den


def _tile_window(i, n_tiles, tq, win, half_window, seq):
    start = pl.multiple_of(jnp.clip(i * tq - half_window, 0, seq - win), 64)
    kind = jnp.where(i == 0, 0, jnp.where(i == n_tiles - 1, 2, 1))
    return start, kind


def _mixer_a_kernel(bias1_ref, bias4_ref, bias16_ref, q1_ref, k1_ref, v1_ref, q4_ref, k4_ref, v4_ref, q16_ref, k16_ref, v16_ref,
                    o_ref, m_scr, l_scr, n_scr, *, seq):
    tq, hw = BAND_TQ, A_HALF_WINDOW

    def run_tile(q_ref, k_ref, v_ref, b_ref, lead, i, ls):
        tqc = min(tq, ls)
        win = min(tqc + 2 * hw, ls)
        n_tiles = ls // tqc
        start, kind = _tile_window(i, n_tiles, tqc, win, hw, ls)
        num, m, den = _band_tile(q_ref[lead, pl.ds(i * tqc, tqc), :], k_ref[lead, pl.ds(start, win), :],
                                 v_ref[lead, pl.ds(start, win), :], b_ref[kind])
        return _unstack_heads(num, tqc), _unstack_column(m, tqc), _unstack_column(den, tqc), tqc

    def tile1(i, carry):
        num, m, den, _ = run_tile(q1_ref, k1_ref, v1_ref, bias1_ref, 0, i, seq)
        rows = pl.ds(pl.multiple_of(i * tq, tq), tq)
        m_scr[rows, :] = m
        l_scr[rows, :] = den
        n_scr[rows, :] = num
        return carry

    lax.fori_loop(0, seq // tq, tile1, 0, unroll=8)

    def merge(tiles):
        old = [(m_scr[rows, :], l_scr[rows, :], n_scr[rows, :]) for rows, _, _, _ in tiles]
        for (rows, num, m, den), (m_old, l_old, n_old) in zip(tiles, old):
            m_new = jnp.maximum(m_old, m)
            a, b = jnp.exp(m_old - m_new), jnp.exp(m - m_new)
            m_scr[rows, :] = m_new
            l_scr[rows, :] = a * l_old + b * den
            n_scr[rows, :] = a * n_old + b * num

    ls4 = seq // 4

    def tile4(i, carry):
        tiles = []
        for r in range(4):
            num, m, den, tqc = run_tile(q4_ref, k4_ref, v4_ref, bias4_ref, r, i, ls4)
            tiles.append((pl.ds(i * (tqc * 4) + r, tqc, stride=4), num, m, den))
        merge(tiles)
        return carry

    lax.fori_loop(0, ls4 // min(tq, ls4), tile4, 0, unroll=2)

    ls16 = seq // 16

    def class16(r2, carry):
        tiles = []
        for r in (2 * r2, 2 * r2 + 1):
            for i in range(ls16 // min(tq, ls16)):
                num, m, den, tqc = run_tile(q16_ref, k16_ref, v16_ref, bias16_ref, r, i, ls16)
                tiles.append((pl.ds(i * (tqc * 16) + r, tqc, stride=16), num, m, den))
        merge(tiles)
        return carry

    lax.fori_loop(0, 8, class16, 0, unroll=2)
    o_ref[0] = (n_scr[...] * (1.0 / l_scr[...])).astype(o_ref.dtype)


def _mixer_a(proj, a4, a16, *, batch, seq, row0):
    n = proj.shape[0]
    b0 = row0 // seq
    hw = A_HALF_WINDOW

    def class_bias(ls):
        tq = min(BAND_TQ, ls)
        return _band_bias(tq, min(tq + 2 * hw, ls), hw)

    biases = [class_bias(seq // d) for d in A_DILATIONS]
    view = proj.reshape(n // seq, seq, IN_WIDTH)
    nat = lambda off: pl.BlockSpec((1, seq, LANES), lambda b, g: (b0 + b, 0, off + g))
    cls = lambda d, off: pl.BlockSpec((d, seq // d, LANES), lambda b, g: (0, b0 + b, off + g))
    full = lambda a: pl.BlockSpec(a.shape, lambda b, g: (0, 0, 0))
    return pl.pallas_call(
        functools.partial(_mixer_a_kernel, seq=seq),
        grid=(batch, A_WIDTH // LANES),
        in_specs=[full(biases[0]), full(biases[1]), full(biases[2]), nat(QA), nat(KA), nat(VA), cls(4, QA), cls(4, KA), cls(4, VA),
                  cls(16, QA), cls(16, KA), cls(16, VA)],
        out_specs=pl.BlockSpec((1, seq, LANES), lambda b, g: (b, 0, g)),
        out_shape=jax.ShapeDtypeStruct((batch, seq, A_WIDTH), jnp.bfloat16),
        scratch_shapes=[pltpu.VMEM((seq, LANES), jnp.float32)] * 3,
        compiler_params=pltpu.CompilerParams(dimension_semantics=("arbitrary",) * 2, vmem_limit_bytes=VMEM_LIMIT),
        name="mixer_a",
    )(*biases, view, view, view, a4, a4, a4, a16, a16, a16)


def _mixer_b_kernel(sink_ref, bias_ref, q_ref, k_ref, v_ref, o_ref, *, seq):
    tq, hw = BAND_TQ, B_HALF_WINDOW
    win = tq + 2 * hw
    n_tiles = seq // tq
    g = pl.program_id(1)
    row = lax.broadcasted_iota(jnp.int32, (2 * tq, 1), 0)
    sink = jnp.where(row < tq, sink_ref[g], sink_ref[g + B_KV_HEADS])

    def tile(i, carry):
        start, kind = _tile_window(i, n_tiles, tq, win, hw, seq)
        rows = pl.ds(pl.multiple_of(i * tq, tq), tq)
        num, _, den = _band_tile(q_ref[0, rows, :], k_ref[0, pl.ds(start, win), :], v_ref[0, pl.ds(start, win), :],
                                 bias_ref[kind], sink)
        o_ref[0, rows, :] = _unstack_heads(num * (1.0 / den), tq).astype(o_ref.dtype)
        return carry

    lax.fori_loop(0, n_tiles, tile, 0, unroll=8)


def _mixer_b(proj, sink, *, batch, seq, row0):
    n = proj.shape[0]
    b0 = row0 // seq
    bias = _band_bias(BAND_TQ, BAND_TQ + 2 * B_HALF_WINDOW, B_HALF_WINDOW)
    view = proj.reshape(n // seq, seq, IN_WIDTH)
    return pl.pallas_call(
        functools.partial(_mixer_b_kernel, seq=seq),
        grid=(batch, B_WIDTH // LANES),
        in_specs=[pl.BlockSpec(memory_space=pltpu.SMEM), pl.BlockSpec(bias.shape, lambda b, g: (0, 0, 0)),
                  pl.BlockSpec((1, seq, LANES), lambda b, g: (b0 + b, 0, QB + g)),
                  pl.BlockSpec((1, seq, LANES), lambda b, g: (b0 + b, 0, KB)),
                  pl.BlockSpec((1, seq, LANES), lambda b, g: (b0 + b, 0, VB))],
        out_specs=pl.BlockSpec((1, seq, LANES), lambda b, g: (b, 0, g)),
        out_shape=jax.ShapeDtypeStruct((batch, seq, B_WIDTH), jnp.bfloat16),
        compiler_params=pltpu.CompilerParams(dimension_semantics=("arbitrary",) * 2, vmem_limit_bytes=VMEM_LIMIT),
        name="mixer_b",
    )(sink, bias, view, view, view)


def _mixer_c_kernel(q_ref, k_ref, v_ref, o_ref, *, tq, chunk):
    lhs = _stack_heads(q_ref[0])
    seq = k_ref.shape[1]
    m = den = acc = None
    for c in range(seq // chunk):
        keys = slice(c * chunk, (c + 1) * chunk)
        s = lax.dot_general(lhs, k_ref[0, keys, :], (((1,), (1,)), ((), ())), preferred_element_type=jnp.float32)
        m_c = jnp.max(s, axis=-1, keepdims=True)
        m_new = m_c if m is None else jnp.maximum(m, m_c)
        p = jnp.exp(s - m_new)
        den_c = jnp.sum(p, axis=-1, keepdims=True)
        acc_c = jnp.dot(p.astype(jnp.bfloat16), v_ref[0, keys, :], preferred_element_type=jnp.float32)
        if m is None:
            den, acc = den_c, acc_c
        else:
            alpha = jnp.exp(m - m_new)
            den, acc = alpha * den + den_c, alpha * acc + acc_c
        m = m_new
    o_ref[0] = _unstack_heads(acc * (1.0 / den), tq).astype(o_ref.dtype)


def _mixer_c(proj, *, batch, seq, row0, tq=512, chunk=1024):
    n = proj.shape[0]
    tq, chunk = min(tq, seq), min(chunk, seq)
    b0 = row0 // seq
    view = proj.reshape(n // seq, seq, IN_WIDTH)
    return pl.pallas_call(
        functools.partial(_mixer_c_kernel, tq=tq, chunk=chunk),
        grid=(batch, seq // tq, C_WIDTH // LANES),
        in_specs=[
            pl.BlockSpec((1, tq, LANES), lambda b, i, g: (b0 + b, i, QC + g)),
            pl.BlockSpec((1, seq, LANES), lambda b, i, g: (b0 + b, 0, KC)),
            pl.BlockSpec((1, seq, LANES), lambda b, i, g: (b0 + b, 0, VC)),
        ],
        out_specs=pl.BlockSpec((1, tq, LANES), lambda b, i, g: (b, i, g)),
        out_shape=jax.ShapeDtypeStruct((batch, seq, C_WIDTH), jnp.bfloat16),
        compiler_params=pltpu.CompilerParams(dimension_semantics=("arbitrary",) * 3, vmem_limit_bytes=VMEM_LIMIT),
        name="mixer_c",
    )(view, view, view)


def _rms(x, g):
    return x * lax.rsqrt(jnp.mean(x * x, axis=-1, keepdims=True) + EPS) * g


def _outproj_kernel(oa_ref, ob_ref, oc_ref, x_ref, w_ref, ga_ref, gb_ref, gc_ref, gf_ref, wr_ref, xo_ref, h_ref,
                    aff_ref):
    f32 = jnp.float32
    merged = jnp.concatenate([_rms(oa_ref[...].astype(f32), ga_ref[...]), _rms(ob_ref[...].astype(f32), gb_ref[...]),
                              _rms(oc_ref[...].astype(f32), gc_ref[...])], axis=-1).astype(jnp.bfloat16)
    xn = x_ref[...] + jnp.dot(merged, w_ref[...], preferred_element_type=f32)
    xo_ref[...] = xn
    h = _rms(xn, gf_ref[...])
    h_ref[:, :D_MODEL] = h
    h_ref[:, D_MODEL:] = jnp.zeros((h.shape[0], LANES), f32)
    logits = lax.dot_general(wr_ref[...], h, (((1,), (1,)), ((), ())), preferred_element_type=f32,
                             precision=lax.Precision.HIGHEST)
    z = jnp.exp(logits - jnp.max(logits, axis=0, keepdims=True))
    aff = z / jnp.sum(z, axis=0, keepdims=True)
    for c in range(aff.shape[1] // LANES):
        aff_ref[c] = aff[:, c * LANES:(c + 1) * LANES]


def _out_projection(oa, ob, oc, x, w, ga, gb, gc, gf, wr_t, tm):
    n = x.shape[0]
    rows = lambda width: pl.BlockSpec((tm, width), lambda i: (i, 0))
    const = lambda shape: pl.BlockSpec(shape, lambda i: (0, 0))
    return pl.pallas_call(
        _outproj_kernel,
        grid=(n // tm,),
        in_specs=[rows(A_WIDTH), rows(B_WIDTH), rows(C_WIDTH),
                  rows(D_MODEL), const((D_MODEL, D_MODEL)), const((1, A_WIDTH)), const((1, B_WIDTH)),
                  const((1, C_WIDTH)), const((1, D_MODEL)), const((N_EXPERTS, D_MODEL))],
        out_specs=[rows(D_MODEL), rows(ROW_WIDTH), pl.BlockSpec((tm // LANES, N_EXPERTS, LANES), lambda i: (i, 0, 0))],
        out_shape=[jax.ShapeDtypeStruct((n, D_MODEL), jnp.float32), jax.ShapeDtypeStruct((n, ROW_WIDTH), jnp.float32),
                   jax.ShapeDtypeStruct((n // LANES, N_EXPERTS, LANES), jnp.float32)],
        compiler_params=pltpu.CompilerParams(dimension_semantics=("arbitrary",), vmem_limit_bytes=VMEM_LIMIT),
        name="out_projection",
    )(oa, ob, oc, x, w, ga, gb, gc, gf, wr_t)


def _ffn_kernel(x_ref, wg_ref, wu_ref, wd_ref, o_ref, dest_ref, xb_scr, gate_scr):
    e = pl.program_id(0)
    j = pl.program_id(2)

    @pl.when(j == 0)
    def _():
        xb_scr[...] = x_ref[0, :, :D_MODEL].astype(jnp.bfloat16)
        route = x_ref[0, :, D_MODEL:]
        lane = lax.broadcasted_iota(jnp.int32, route.shape, 1)
        pick = lambda k: jnp.sum(jnp.where(lane == k + e, route, 0.0), axis=-1, keepdims=True)
        gate_scr[...] = pick(GATE_LANE)
        dest_ref[0] = pick(DEST_LANE).astype(jnp.int32)
        o_ref[0] = jnp.zeros(o_ref.shape[1:], o_ref.dtype)

    x = xb_scr[...]
    hg = jnp.dot(x, wg_ref[...].astype(jnp.bfloat16), preferred_element_type=jnp.float32)
    hu = jnp.dot(x, wu_ref[...].astype(jnp.bfloat16), preferred_element_type=jnp.float32)
    act = (hg * jax.nn.sigmoid(hg) * hu).astype(jnp.bfloat16)
    o_ref[0] += jnp.dot(act, wd_ref[...].astype(jnp.bfloat16), preferred_element_type=jnp.float32)

    @pl.when(j == pl.num_programs(2) - 1)
    def _():
        o_ref[0] = o_ref[0] * gate_scr[...]


def _expert_ffn(xe, w_gate, w_up, w_down, layer, tr, tf):
    n_e, rows, _ = xe.shape
    d = D_MODEL
    d_ff = w_gate.shape[-1]
    return pl.pallas_call(
        _ffn_kernel,
        grid=(n_e, rows // tr, d_ff // tf),
        in_specs=[
            pl.BlockSpec((1, tr, ROW_WIDTH), lambda e, c, j: (e, c, 0)),
            pl.BlockSpec((None, None, d, tf), lambda e, c, j: (layer, e, 0, j)),
            pl.BlockSpec((None, None, d, tf), lambda e, c, j: (layer, e, 0, j)),
            pl.BlockSpec((None, None, tf, d), lambda e, c, j: (layer, e, j, 0)),
        ],
        out_specs=[pl.BlockSpec((1, tr, d), lambda e, c, j: (e, c, 0)),
                   pl.BlockSpec((1, tr, 1), lambda e, c, j: (e, c, 0))],
        out_shape=[jax.ShapeDtypeStruct((n_e, rows, d), jnp.float32),
                   jax.ShapeDtypeStruct((n_e, rows, 1), jnp.int32)],
        scratch_shapes=[pltpu.VMEM((tr, d), jnp.bfloat16), pltpu.VMEM((tr, 1), jnp.float32)],
        compiler_params=pltpu.CompilerParams(dimension_semantics=("arbitrary",) * 3, vmem_limit_bytes=VMEM_LIMIT),
        name="expert_ffn",
    )(xe, w_gate, w_up, w_down)


def _lane_cumsum(m, tri):
    nc, r, _ = m.shape
    flat = m.reshape(nc * r, LANES).astype(jnp.bfloat16)
    return jnp.dot(flat, tri, preferred_element_type=jnp.float32).reshape(nc, r, LANES)


def _lead_cumsum_exclusive(t):
    n = t.shape[0]
    inc, k = t, 1
    while k < n:
        inc = inc + jnp.concatenate([jnp.zeros((k,) + t.shape[1:], t.dtype), inc[:n - k]], axis=0)
        k *= 2
    return inc - t


def _token_cumsum(m, tri):
    inside = _lane_cumsum(m, tri)
    total = inside[:, :, LANES - 1:]
    return _lead_cumsum_exclusive(total), inside, total


def _select_kernel(aff_ref, tri_ref, idx_ref, stats_ref, split_scr, before_scr, through_scr, *, groups):
    f32 = jnp.float32
    tri = tri_ref[...]
    for c0, nc, cap, s0 in groups:
        aff = aff_ref[c0:c0 + nc]
        bits = pltpu.bitcast(aff, jnp.int32)
        count = lambda mask: jnp.sum(jnp.sum(mask, axis=0, keepdims=True), axis=2, keepdims=True)

        def bisect(_, carry):
            lo, hi = carry
            mid = lo + ((hi - lo) >> 1)
            ok = count(jnp.where(bits >= mid, 1.0, 0.0)) >= cap
            return jnp.where(ok, mid, lo), jnp.where(ok, hi, mid)

        shape = (1, N_EXPERTS, 1)
        thr, _ = lax.fori_loop(0, 31, bisect, (jnp.zeros(shape, jnp.int32), jnp.full(shape, ONE_BITS + 1, jnp.int32)))
        above, tie = bits > thr, bits == thr
        tie_f = jnp.where(tie, 1.0, 0.0)
        need = cap - count(jnp.where(above, 1.0, 0.0))
        before, inside, _ = _token_cumsum(tie_f, tri)
        chosen = jnp.where(above | (tie & (before + inside - tie_f < need)), 1.0, 0.0)

        before, inside, total = _token_cumsum(chosen, tri)
        through = before + inside
        mult = jnp.sum(chosen, axis=1, keepdims=True)
        m_before, m_inside, _ = _token_cumsum(mult, tri)
        stats_ref[c0:c0 + nc, 0:N_EXPERTS, :] = chosen
        stats_ref[c0:c0 + nc, N_EXPERTS:N_EXPERTS + 1, :] = m_before + m_inside - mult
        stats_ref[c0:c0 + nc, N_EXPERTS + 1:N_EXPERTS + 2, :] = mult
        stats_ref[c0:c0 + nc, N_EXPERTS + 2:, :] = jnp.zeros((nc, 6, LANES), f32)

        for e in range(N_EXPERTS):
            t_e = through[:, e, :]
            hi_digit = jnp.floor(t_e * (1.0 / 64))
            split_scr[e, 0:nc, 0:LANES] = hi_digit.astype(jnp.bfloat16)
            split_scr[e, 0:nc, LANES:] = (t_e - 64.0 * hi_digit).astype(jnp.bfloat16)
            before_scr[e, 0:nc, :] = jnp.broadcast_to(before[:, e, :], (nc, LANES))
            through_scr[e, 0:nc, :] = jnp.broadcast_to((before + total)[:, e, :], (nc, LANES))

        chunk_id = lax.broadcasted_iota(jnp.int32, (1, nc), 1).astype(f32)
        lane_id = lax.broadcasted_iota(jnp.int32, (1, LANES), 1)
        row_id = lax.broadcasted_iota(jnp.int32, (LANES, 1), 0)

        def compact(it, carry):
            e, s = it // (cap // LANES), it % (cap // LANES)
            slot_row = (s * LANES + lane_id).astype(f32)
            slot_col = (s * LANES + row_id).astype(f32)
            holds = (before_scr[e, 0:nc, :] <= slot_row) & (slot_row < through_scr[e, 0:nc, :])
            onehot = jnp.where(holds, 1.0, 0.0).T
            digits = jnp.dot(onehot.astype(jnp.bfloat16), split_scr[e, 0:nc, :], preferred_element_type=f32)
            counts = 64.0 * digits[:, :LANES] + digits[:, LANES:]
            inside_pos = jnp.sum(jnp.where(counts <= slot_col, 1.0, 0.0), axis=-1, keepdims=True)
            chunk = jnp.sum(onehot * chunk_id, axis=-1, keepdims=True)
            token = (c0 + chunk) * LANES + inside_pos
            idx_ref[e, pl.ds(s0 + s, 1), :] = jnp.broadcast_to(token, (LANES, LANES)).T[0:1, :].astype(jnp.int32)
            return carry

        lax.fori_loop(0, N_EXPERTS * (cap // LANES), compact, 0, unroll=4)


def _select(aff, groups, slots):
    n_chunks = aff.shape[0]
    nc_max = max(nc for _, nc, _, _ in groups)
    tri = jnp.asarray(np.triu(np.ones((LANES, LANES))), jnp.bfloat16)
    return pl.pallas_call(
        functools.partial(_select_kernel, groups=groups),
        out_shape=[jax.ShapeDtypeStruct((N_EXPERTS, slots // LANES, LANES), jnp.int32),
                   jax.ShapeDtypeStruct((n_chunks, 24, LANES), jnp.float32)],
        scratch_shapes=[pltpu.VMEM((N_EXPERTS, nc_max, 2 * LANES), jnp.bfloat16),
                        pltpu.VMEM((N_EXPERTS, nc_max, LANES), jnp.float32),
                        pltpu.VMEM((N_EXPERTS, nc_max, LANES), jnp.float32)],
        compiler_params=pltpu.CompilerParams(vmem_limit_bytes=VMEM_LIMIT),
        name="expert_select",
    )(aff, tri)


def _route_rows_kernel(aff_ref, stats_ref, below_ref, rows_in_ref, o_ref):
    del rows_in_ref
    n = aff_ref.shape[0]
    pad = jnp.zeros((LANES - N_EXPERTS - stats_ref.shape[1], LANES), jnp.float32)
    lane = lax.broadcasted_iota(jnp.int32, (LANES, LANES), 1)
    for c in range(n):
        t = jnp.concatenate([aff_ref[c], stats_ref[c], pad], axis=0).T
        rank = jnp.dot(t.astype(jnp.bfloat16), below_ref[...], preferred_element_type=jnp.float32)
        first = t[:, OFF_LANE:OFF_LANE + 1]
        o_ref[c * LANES:(c + 1) * LANES, :] = jnp.where((lane >= DEST_LANE) & (lane < OFF_LANE), first + rank, t)


def _route_rows(aff, stats, rows, tm):
    n = rows.shape[0]
    k = tm // LANES
    below = np.zeros((LANES, LANES))
    below[DEST_LANE:OFF_LANE, DEST_LANE:OFF_LANE] = np.triu(np.ones((N_EXPERTS, N_EXPERTS)), 1)
    return pl.pallas_call(
        _route_rows_kernel,
        grid=(n // tm,),
        in_specs=[pl.BlockSpec((k, N_EXPERTS, LANES), lambda i: (i, 0, 0)),
                  pl.BlockSpec((k, stats.shape[1], LANES), lambda i: (i, 0, 0)),
                  pl.BlockSpec((LANES, LANES), lambda i: (0, 0)),
                  pl.BlockSpec(memory_space=pl.ANY)],
        out_specs=pl.BlockSpec((tm, LANES), lambda i: (i, D_MODEL // LANES)),
        out_shape=jax.ShapeDtypeStruct(rows.shape, rows.dtype),
        input_output_aliases={3: 0},
        compiler_params=pltpu.CompilerParams(dimension_semantics=("arbitrary",)),
        name="route_rows",
    )(aff, stats, jnp.asarray(below, jnp.bfloat16), rows)


SC_ROWS = 32


def _sc_mesh():
    return plsc.VectorSubcoreMesh(core_axis_name="core", subcore_axis_name="subcore")


def _sc_gather(table, idx):
    m, w = idx.shape[0], table.shape[1]
    per = m // (SC_WORKERS * SC_ROWS)
    assert per * SC_WORKERS * SC_ROWS == m

    @functools.partial(pl.kernel, out_type=jax.ShapeDtypeStruct((m, w), table.dtype), mesh=_sc_mesh(),
                       scratch_types=[pltpu.VMEM((1, SC_ROWS), jnp.int32), pltpu.VMEM((SC_ROWS, w), table.dtype)])
    def gather(table_hbm, idx_hbm, out_hbm, idx_v, buf):
        worker = lax.axis_index("core") * (SC_WORKERS // 2) + lax.axis_index("subcore")

        @pl.loop(0, per)
        def _(b):
            blk = worker * per + b
            pltpu.sync_copy(idx_hbm.at[pl.ds(blk, 1)], idx_v)
            pltpu.sync_copy(table_hbm.at[idx_v.at[0]], buf)
            pltpu.sync_copy(buf, out_hbm.at[pl.ds(blk * SC_ROWS, SC_ROWS)])

    return gather(table, idx.reshape(m // SC_ROWS, SC_ROWS))


def _sc_scatter(rows, dest):
    m, w = rows.shape
    per = m // (SC_WORKERS * SC_ROWS)
    assert per * SC_WORKERS * SC_ROWS == m

    @functools.partial(pl.kernel, out_type=jax.ShapeDtypeStruct((m, w), rows.dtype), mesh=_sc_mesh(),
                       scratch_types=[pltpu.VMEM((1, SC_ROWS), jnp.int32), pltpu.VMEM((SC_ROWS, w), rows.dtype)])
    def scatter(rows_hbm, dest_hbm, out_hbm, dest_v, buf):
        worker = lax.axis_index("core") * (SC_WORKERS // 2) + lax.axis_index("subcore")

        @pl.loop(0, per)
        def _(b):
            blk = worker * per + b
            pltpu.sync_copy(dest_hbm.at[pl.ds(blk, 1)], dest_v)
            pltpu.sync_copy(rows_hbm.at[pl.ds(blk * SC_ROWS, SC_ROWS)], buf)
            pltpu.sync_copy(buf, out_hbm.at[dest_v.at[0]])

    return scatter(rows, dest.reshape(m // SC_ROWS, SC_ROWS))


COMBINE_ROWS = 256


def _combine_kernel(tile_ref, start_ref, want_ref, flags_ref, x_ref, route_ref, z_ref, o_ref):
    s = pl.program_id(0)
    f32, bf16 = jnp.float32, jnp.bfloat16

    @pl.when(flags_ref[s] == 1)
    def _():
        o_ref[...] = x_ref[...]

    @pl.when(flags_ref[s] != 2)
    def _():
        first = route_ref[:, OFF_LANE:OFF_LANE + 1]
        last = first + route_ref[:, MULT_LANE:MULT_LANE + 1]
        row = start_ref[s] * 8 + lax.broadcasted_iota(jnp.int32, (1, COMBINE_ROWS), 1)
        rowf = row.astype(f32)
        own = jnp.where((first <= rowf) & (rowf < last) & (row >= want_ref[s]), 1.0, 0.0).astype(bf16)
        z = z_ref[...]
        z1 = z.astype(bf16)
        r1 = z - z1.astype(f32)
        z2 = r1.astype(bf16)
        z3 = (r1 - z2.astype(f32)).astype(bf16)
        o_ref[...] += (jnp.dot(own, z1, preferred_element_type=f32) + jnp.dot(own, z2, preferred_element_type=f32)
                       + jnp.dot(own, z3, preferred_element_type=f32))


def _combine_steps(tile_lo, n_rows):
    n_tiles = tile_lo.shape[0] - 1
    n_steps_max = n_rows // COMBINE_ROWS + 2 * n_tiles + n_tiles // 16 + 1
    lo = (tile_lo[:-1] // 8) * 8
    per_tile = jnp.maximum((tile_lo[1:] - lo + COMBINE_ROWS - 1) // COMBINE_ROWS, 1)
    ends = jnp.cumsum(per_tile)
    s = jnp.arange(n_steps_max, dtype=jnp.int32)
    valid = s < ends[-1]
    tile = jnp.minimum(jnp.sum(ends[None, :] <= s[:, None], axis=1).astype(jnp.int32), n_tiles - 1)
    mine = tile[:, None] == jnp.arange(n_tiles, dtype=jnp.int32)[None, :]
    of_tile = lambda a: jnp.sum(jnp.where(mine, a[None, :], 0), axis=1)
    k = s - of_tile(ends - per_tile)
    want = of_tile(lo) + k * COMBINE_ROWS
    start = jnp.minimum(want, n_rows - COMBINE_ROWS)
    last_start = jnp.sum(jnp.where(s == ends[-1] - 1, start, 0))
    start = jnp.where(valid, start, last_start)
    flags = jnp.where(valid, (k == 0).astype(jnp.int32), 2)
    return tile, start // 8, want, flags, n_steps_max


def _combine(x, rows, z, tile_lo, tt):
    tile, start, want, flags, n_steps = _combine_steps(tile_lo, z.shape[0])
    return pl.pallas_call(
        _combine_kernel,
        grid_spec=pltpu.PrefetchScalarGridSpec(
            num_scalar_prefetch=4,
            grid=(n_steps,),
            in_specs=[pl.BlockSpec((tt, D_MODEL), lambda s, tile, *_: (tile[s], 0)),
                      pl.BlockSpec((tt, LANES), lambda s, tile, *_: (tile[s], D_MODEL // LANES)),
                      pl.BlockSpec((pl.Element(COMBINE_ROWS), pl.Element(D_MODEL)),
                                   lambda s, tile, start, *_: (start[s] * 8, 0))],
            out_specs=pl.BlockSpec((tt, D_MODEL), lambda s, tile, *_: (tile[s], 0)),
        ),
        out_shape=jax.ShapeDtypeStruct(x.shape, jnp.float32),
        compiler_params=pltpu.CompilerParams(dimension_semantics=("arbitrary",), vmem_limit_bytes=VMEM_LIMIT),
        name="expert_combine",
    )(tile, start, want, flags, x, rows, z)


def _final_norm_kernel(x_ref, g_ref, o_ref):
    o_ref[...] = _rms(x_ref[...], g_ref[...])


def _final_norm(x, g, tm):
    n = x.shape[0]
    return pl.pallas_call(
        _final_norm_kernel,
        grid=(n // tm,),
        in_specs=[pl.BlockSpec((tm, D_MODEL), lambda i: (i, 0)), pl.BlockSpec((1, D_MODEL), lambda i: (0, 0))],
        out_specs=pl.BlockSpec((tm, D_MODEL), lambda i: (i, 0)),
        out_shape=jax.ShapeDtypeStruct((n, D_MODEL), jnp.float32),
        name="final_norm",
    )(x, g)


def _forward(x_prompt, x_sample, g_attn, w_in, g_q_c, g_k_c, sink_b, g_out_a, g_out_b, g_out_c, w_out, g_ffn, w_router,
             w_gate, w_up, w_down, g_final, *, tm, tr, tf):
    bf16 = jnp.bfloat16
    shapes = (x_prompt.shape[:2], x_sample.shape[:2])
    assert all(l % tm == 0 and (b * l) % LANES == 0 for b, l in shapes)
    xs = [x_prompt.reshape(-1, D_MODEL), x_sample.reshape(-1, D_MODEL)]

    tables = _rope_tables(max(l for _, l in shapes))
    seg = jnp.asarray(np.kron(np.eye(LANES // HEAD_DIM), np.ones((HEAD_DIM, HEAD_DIM))), bf16)
    pb, pc, perm_out = _out_perms()
    w_in_p = w_in[:, :, _in_perm()].astype(bf16)
    w_out_p = w_out[:, perm_out, :].astype(bf16)
    sink_p = sink_b[:, np.asarray(B_Q_ORDER)]
    tile2 = lambda g: jnp.tile(g, (1, 2))[:, None, :]
    gq, gk = tile2(g_q_c), tile2(g_k_c)
    wr_t = jnp.swapaxes(w_router, 1, 2)
    tt = 256

    def layer(x, l, batch, seq):
        n = batch * seq
        cap = CAPACITY_FACTOR * n // N_EXPERTS
        assert cap % LANES == 0
        proj, a4, a16 = _in_projection(x, g_attn[l][None], w_in_p[l], tables, gq[l], gk[l], seg, seq, tm)
        oa = _mixer_a(proj, a4, a16, batch=batch, seq=seq, row0=0).reshape(n, A_WIDTH)
        ob = _mixer_b(proj, sink_p[l], batch=batch, seq=seq, row0=0).reshape(n, B_WIDTH)
        oc = _mixer_c(proj, batch=batch, seq=seq, row0=0).reshape(n, C_WIDTH)
        x, rows, aff = _out_projection(oa, ob, oc, x, w_out_p[l], g_out_a[l][None], g_out_b[l][pb][None],
                                       g_out_c[l][pc][None], g_ffn[l][None], wr_t[l], tm)
        idx, stats = _select(aff, ((0, n // LANES, cap, 0),), cap)
        rows = _route_rows(aff, stats, rows, tm)
        xe = _sc_gather(rows, idx.reshape(-1)).reshape(N_EXPERTS, cap, ROW_WIDTH)
        ye, dest = _expert_ffn(xe, w_gate, w_up, w_down, l, min(tr, cap), tf)
        z = _sc_scatter(ye.reshape(-1, D_MODEL), dest.reshape(-1))
        first_slot = stats[::tt // LANES, N_EXPERTS, 0].astype(jnp.int32)
        tile_lo = jnp.concatenate([first_slot, jnp.full((1,), N_EXPERTS * cap, jnp.int32)])
        return _combine(x, rows, z, tile_lo, tt)

    for l in range(DEPTH):
        xs = [layer(x, l, b, s) for x, (b, s) in zip(xs, shapes)]
    return tuple(_final_norm(x, g_final[None], tm).reshape(b, s, D_MODEL) for x, (b, s) in zip(xs, shapes))


def kernel(x_prompt, x_sample, g_attn, w_in, g_q_c, g_k_c, sink_b, g_out_a, g_out_b, g_out_c, w_out, g_ffn, w_router,
           w_gate, w_up, w_down, g_final):
    return _forward(x_prompt, x_sample, g_attn, w_in, g_q_c, g_k_c, sink_b, g_out_a, g_out_b, g_out_c, w_out, g_ffn,
                    w_router, w_gate, w_up, w_down, g_final, tm=512, tr=2048, tf=256)
```

```python
import functools

import jax
import jax.numpy as jnp
import numpy as np
from jax import lax
from jax.experimental import pallas as pl
from jax.experimental.pallas import tpu as pltpu
from jax.experimental.pallas import tpu_sc as plsc

D_MODEL = 1024
DEPTH = 4
HEAD_DIM = 64
A_HEADS = 6
A_PAIRS = ((128, 1), (512, 4), (2048, 16))
B_HEADS = 4
B_KV_HEADS = 2
B_HALF_WINDOW = 128
C_HEADS = 6
C_KV_HEADS = 2
GRID_W = 64
ROPE_THETA = 10000.0
N_EXPERTS = 16
CAPACITY_FACTOR = 2
D_FF = 2816
EPS = 1e-6
NEG_INF = -1e30

LANES = 128
A_WIDTH = A_HEADS * HEAD_DIM
B_WIDTH = B_HEADS * HEAD_DIM
C_WIDTH = C_HEADS * HEAD_DIM
IN_WIDTH = 3 * A_WIDTH + B_WIDTH + 2 * B_KV_HEADS * HEAD_DIM + C_WIDTH + 2 * C_KV_HEADS * HEAD_DIM
N_GROUPS = IN_WIDTH // LANES
A_GROUPS = 3 * A_WIDTH // LANES
QA, KA, VA, QB, KB, VB, QC, KC, VC = 0, 3, 6, 9, 11, 12, 13, 16, 17
ROPE_NONE, ROPE_1D, ROPE_AXIAL_Q, ROPE_AXIAL_K = 0, 1, 2, 3
GROUP_KIND = ([(ROPE_1D, True)] * 3 + [(ROPE_1D, False)] * 3 + [(ROPE_NONE, False)] * 3
              + [(ROPE_1D, True)] * 2 + [(ROPE_1D, False)] + [(ROPE_NONE, False)]
              + [(ROPE_AXIAL_Q, True)] * 3 + [(ROPE_AXIAL_K, False)] + [(ROPE_NONE, False)])
Q_SCALE = HEAD_DIM ** -0.5
A_DILATIONS = tuple(d for _, d in A_PAIRS)
A_HALF_WINDOW = A_PAIRS[0][0] // 2
assert all(w // 2 // d == A_HALF_WINDOW for w, d in A_PAIRS) and A_DILATIONS == (1, 4, 16)
BAND_TQ = 128
ROW_WIDTH = D_MODEL + LANES
GATE_LANE, DEST_LANE, OFF_LANE, MULT_LANE = 0, N_EXPERTS, 2 * N_EXPERTS, 2 * N_EXPERTS + 1
ONE_BITS = 0x3F800000
SC_WORKERS = 32

VMEM_LIMIT = 56 * 1024 * 1024

B_Q_ORDER = (0, 2, 1, 3)
C_Q_ORDER = (0, 3, 1, 4, 2, 5)


def _head_perm(order):
    return np.concatenate([np.arange(h * HEAD_DIM, (h + 1) * HEAD_DIM) for h in order])


def _in_perm():
    widths = [A_WIDTH] * 3 + [B_WIDTH, 128, 128, C_WIDTH, 128, 128]
    offs = np.concatenate([[0], np.cumsum(widths)])
    parts = [np.arange(offs[i], offs[i + 1]) for i in range(9)]
    parts[3] = offs[3] + _head_perm(B_Q_ORDER)
    parts[6] = offs[6] + _head_perm(C_Q_ORDER)
    return np.concatenate(parts)


def _out_perms():
    pb = _head_perm(B_Q_ORDER)
    pc = _head_perm(C_Q_ORDER)
    return pb, pc, np.concatenate([np.arange(A_WIDTH), A_WIDTH + pb, A_WIDTH + B_WIDTH + pc])


def _rope_tables(seq):
    pos = jnp.arange(seq, dtype=jnp.float32)
    inv1 = ROPE_THETA ** (-jnp.arange(0, HEAD_DIM, 2, dtype=jnp.float32) / HEAD_DIM)
    ang = pos[:, None] * inv1[None, :]
    c, s = jnp.cos(ang), jnp.sin(ang)
    cos1 = jnp.tile(jnp.concatenate([c, c], -1), (1, 2))
    sin1 = jnp.tile(jnp.concatenate([-s, s], -1), (1, 2))
    half = HEAD_DIM // 2
    inv2 = ROPE_THETA ** (-jnp.arange(0, half, 2, dtype=jnp.float32) / half)
    row = jnp.floor(pos / GRID_W)
    col = pos - row * GRID_W
    ar, ac = row[:, None] * inv2[None, :], col[:, None] * inv2[None, :]
    cr, sr, cc, sc = jnp.cos(ar), jnp.sin(ar), jnp.cos(ac), jnp.sin(ac)
    cos2 = jnp.tile(jnp.concatenate([cr, cr, cc, cc], -1), (1, 2))
    sin2 = jnp.tile(jnp.concatenate([-sr, sr, -sc, sc], -1), (1, 2))
    return cos1, sin1, cos2, sin2


def _swap_halves(x, block):
    half = block // 2
    lane = lax.broadcasted_iota(jnp.int32, x.shape, 1)
    return jnp.where(lane % block < half, pltpu.roll(x, LANES - half, 1), pltpu.roll(x, half, 1))


def _inproj_kernel(x_ref, g_ref, w_ref, cos1_ref, sin1_ref, cos2_ref, sin2_ref, gq_ref, gk_ref, seg_ref, o_ref,
                   a4_ref, a16_ref, rows_ref):
    x = x_ref[...]
    tm = x.shape[0]
    y = x * lax.rsqrt(jnp.mean(x * x, axis=-1, keepdims=True) + EPS)
    h = (y * g_ref[...]).astype(jnp.bfloat16)
    for c in range(N_GROUPS // 2):
        acc = jnp.dot(h, w_ref[:, c * 2 * LANES:(c + 1) * 2 * LANES], preferred_element_type=jnp.float32)
        for half in range(2):
            grp = 2 * c + half
            cols = slice(grp * LANES, (grp + 1) * LANES)
            a = acc[:, half * LANES:(half + 1) * LANES]
            kind, is_q = GROUP_KIND[grp]
            if kind == ROPE_1D:
                a = a * cos1_ref[...] + _swap_halves(a, HEAD_DIM) * sin1_ref[...]
            elif kind in (ROPE_AXIAL_Q, ROPE_AXIAL_K):
                gain = gq_ref[...] if kind == ROPE_AXIAL_Q else gk_ref[...]
                sq = a * a
                sq_hi = sq.astype(jnp.bfloat16)
                sq_lo = (sq - sq_hi.astype(jnp.float32)).astype(jnp.bfloat16)
                ss = (jnp.dot(sq_hi, seg_ref[...], preferred_element_type=jnp.float32)
                      + jnp.dot(sq_lo, seg_ref[...], preferred_element_type=jnp.float32))
                a = a * lax.rsqrt(ss * (1.0 / HEAD_DIM) + EPS) * gain
                a = a * cos2_ref[...] + _swap_halves(a, HEAD_DIM // 2) * sin2_ref[...]
            if is_q:
                a = a * Q_SCALE
            o_ref[:, cols] = a.astype(jnp.bfloat16)
            if grp < A_GROUPS:
                rows_ref[grp] = a
        if c == (A_GROUPS - 1) // 2:
            for d, ref in ((4, a4_ref), (16, a16_ref)):
                for r in range(d):
                    for grp in range(A_GROUPS):
                        ref[r, :, grp * LANES:(grp + 1) * LANES] = (
                            rows_ref[grp, pl.ds(r, tm // d, stride=d), :].astype(jnp.bfloat16))


def _in_projection(x, g, w, tables, gq, gk, seg, seq, tm):
    n = x.shape[0]
    tab_spec = pl.BlockSpec((tm, LANES), lambda i: (i % (seq // tm), 0))
    const = lambda shape: pl.BlockSpec(shape, lambda i: (0, 0))
    wa = A_GROUPS * LANES
    return pl.pallas_call(
        _inproj_kernel,
        grid=(n // tm,),
        in_specs=[pl.BlockSpec((tm, D_MODEL), lambda i: (i, 0)), const((1, D_MODEL)), const((D_MODEL, IN_WIDTH)),
                  tab_spec, tab_spec, tab_spec, tab_spec, const((1, LANES)), const((1, LANES)), const((LANES, LANES))],
        out_specs=[pl.BlockSpec((tm, IN_WIDTH), lambda i: (i, 0)),
                   pl.BlockSpec((4, tm // 4, wa), lambda i: (0, i, 0)),
                   pl.BlockSpec((16, tm // 16, wa), lambda i: (0, i, 0))],
        out_shape=[jax.ShapeDtypeStruct((n, IN_WIDTH), jnp.bfloat16),
                   jax.ShapeDtypeStruct((4, n // 4, wa), jnp.bfloat16),
                   jax.ShapeDtypeStruct((16, n // 16, wa), jnp.bfloat16)],
        scratch_shapes=[pltpu.VMEM((A_GROUPS, tm, LANES), jnp.float32)],
        compiler_params=pltpu.CompilerParams(dimension_semantics=("arbitrary",), vmem_limit_bytes=VMEM_LIMIT),
        name="in_projection",
    )(x, g, w, *tables, gq, gk, seg)


def _stack_heads(q):
    lane = lax.broadcasted_iota(jnp.int32, q.shape, 1)
    zero = jnp.zeros_like(q)
    return jnp.concatenate([jnp.where(lane < HEAD_DIM, q, zero), jnp.where(lane >= HEAD_DIM, q, zero)], axis=0)


def _unstack_heads(x, tq):
    lane = lax.broadcasted_iota(jnp.int32, (tq, x.shape[1]), 1)
    return jnp.where(lane < HEAD_DIM, x[:tq], x[tq:])


def _unstack_column(col, tq):
    return _unstack_heads(jnp.broadcast_to(col, (2 * tq, LANES)), tq)


def _band_bias(tq, win, half_window):
    row = np.arange(2 * tq)[:, None] % tq
    col = np.arange(win)[None, :]
    kinds = [np.where(np.abs(row + off - col) <= half_window, 0.0, NEG_INF) for off in (0, half_window, 2 * half_window)]
    return jnp.asarray(np.stack(kinds), jnp.float32)


def _band_tile(q, kw, vw, bias, sink=None):
    s = lax.dot_general(_stack_heads(q), kw, (((1,), (1,)), ((), ())), preferred_element_type=jnp.float32) + bias
    m = jnp.max(s, axis=-1, keepdims=True)
    if sink is not None:
        m = jnp.maximum(m, sink)
    p = jnp.exp(s - m)
    den = jnp.sum(p, axis=-1, keepdims=True)
    if sink is not None:
        den = den + jnp.exp(sink - m)
    num = jnp.dot(p.astype(jnp.bfloat16), vw, preferred_element_type=jnp.float32)
    return num, m, den


def _tile_window(i, n_tiles, tq, win, half_window, seq):
    start = pl.multiple_of(jnp.clip(i * tq - half_window, 0, seq - win), 64)
    kind = jnp.where(i == 0, 0, jnp.where(i == n_tiles - 1, 2, 1))
    return start, kind


def _mixer_a_kernel(bias1_ref, bias4_ref, bias16_ref, q1_ref, k1_ref, v1_ref, q4_ref, k4_ref, v4_ref, q16_ref, k16_ref, v16_ref,
                    o_ref, m_scr, l_scr, n_scr, *, seq):
    tq, hw = BAND_TQ, A_HALF_WINDOW

    def run_tile(q_ref, k_ref, v_ref, b_ref, lead, i, ls):
        tqc = min(tq, ls)
        win = min(tqc + 2 * hw, ls)
        n_tiles = ls // tqc
        start, kind = _tile_window(i, n_tiles, tqc, win, hw, ls)
        num, m, den = _band_tile(q_ref[lead, pl.ds(i * tqc, tqc), :], k_ref[lead, pl.ds(start, win), :],
                                 v_ref[lead, pl.ds(start, win), :], b_ref[kind])
        return _unstack_heads(num, tqc), _unstack_column(m, tqc), _unstack_column(den, tqc), tqc

    def tile1(i, carry):
        num, m, den, _ = run_tile(q1_ref, k1_ref, v1_ref, bias1_ref, 0, i, seq)
        rows = pl.ds(pl.multiple_of(i * tq, tq), tq)
        m_scr[rows, :] = m
        l_scr[rows, :] = den
        n_scr[rows, :] = num
        return carry

    lax.fori_loop(0, seq // tq, tile1, 0, unroll=8)

    def merge(tiles):
        old = [(m_scr[rows, :], l_scr[rows, :], n_scr[rows, :]) for rows, _, _, _ in tiles]
        for (rows, num, m, den), (m_old, l_old, n_old) in zip(tiles, old):
            m_new = jnp.maximum(m_old, m)
            a, b = jnp.exp(m_old - m_new), jnp.exp(m - m_new)
            m_scr[rows, :] = m_new
            l_scr[rows, :] = a * l_old + b * den
            n_scr[rows, :] = a * n_old + b * num

    ls4 = seq // 4

    def tile4(i, carry):
        tiles = []
        for r in range(4):
            num, m, den, tqc = run_tile(q4_ref, k4_ref, v4_ref, bias4_ref, r, i, ls4)
            tiles.append((pl.ds(i * (tqc * 4) + r, tqc, stride=4), num, m, den))
        merge(tiles)
        return carry

    lax.fori_loop(0, ls4 // min(tq, ls4), tile4, 0, unroll=2)

    ls16 = seq // 16

    def class16(r2, carry):
        tiles = []
        for r in (2 * r2, 2 * r2 + 1):
            for i in range(ls16 // min(tq, ls16)):
                num, m, den, tqc = run_tile(q16_ref, k16_ref, v16_ref, bias16_ref, r, i, ls16)
                tiles.append((pl.ds(i * (tqc * 16) + r, tqc, stride=16), num, m, den))
        merge(tiles)
        return carry

    lax.fori_loop(0, 8, class16, 0, unroll=2)
    o_ref[0] = (n_scr[...] * (1.0 / l_scr[...])).astype(o_ref.dtype)


def _mixer_a(proj, a4, a16, *, batch, seq, row0):
    n = proj.shape[0]
    b0 = row0 // seq
    hw = A_HALF_WINDOW

    def class_bias(ls):
        tq = min(BAND_TQ, ls)
        return _band_bias(tq, min(tq + 2 * hw, ls), hw)

    biases = [class_bias(seq // d) for d in A_DILATIONS]
    view = proj.reshape(n // seq, seq, IN_WIDTH)
    nat = lambda off: pl.BlockSpec((1, seq, LANES), lambda b, g: (b0 + b, 0, off + g))
    cls = lambda d, off: pl.BlockSpec((d, seq // d, LANES), lambda b, g: (0, b0 + b, off + g))
    full = lambda a: pl.BlockSpec(a.shape, lambda b, g: (0, 0, 0))
    return pl.pallas_call(
        functools.partial(_mixer_a_kernel, seq=seq),
        grid=(batch, A_WIDTH // LANES),
        in_specs=[full(biases[0]), full(biases[1]), full(biases[2]), nat(QA), nat(KA), nat(VA), cls(4, QA), cls(4, KA), cls(4, VA),
                  cls(16, QA), cls(16, KA), cls(16, VA)],
        out_specs=pl.BlockSpec((1, seq, LANES), lambda b, g: (b, 0, g)),
        out_shape=jax.ShapeDtypeStruct((batch, seq, A_WIDTH), jnp.bfloat16),
        scratch_shapes=[pltpu.VMEM((seq, LANES), jnp.float32)] * 3,
        compiler_params=pltpu.CompilerParams(dimension_semantics=("arbitrary",) * 2, vmem_limit_bytes=VMEM_LIMIT),
        name="mixer_a",
    )(*biases, view, view, view, a4, a4, a4, a16, a16, a16)


def _mixer_b_kernel(sink_ref, bias_ref, q_ref, k_ref, v_ref, o_ref, *, seq):
    tq, hw = BAND_TQ, B_HALF_WINDOW
    win = tq + 2 * hw
    n_tiles = seq // tq
    g = pl.program_id(1)
    row = lax.broadcasted_iota(jnp.int32, (2 * tq, 1), 0)
    sink = jnp.where(row < tq, sink_ref[g], sink_ref[g + B_KV_HEADS])

    def tile(i, carry):
        start, kind = _tile_window(i, n_tiles, tq, win, hw, seq)
        rows = pl.ds(pl.multiple_of(i * tq, tq), tq)
        num, _, den = _band_tile(q_ref[0, rows, :], k_ref[0, pl.ds(start, win), :], v_ref[0, pl.ds(start, win), :],
                                 bias_ref[kind], sink)
        o_ref[0, rows, :] = _unstack_heads(num * (1.0 / den), tq).astype(o_ref.dtype)
        return carry

    lax.fori_loop(0, n_tiles, tile, 0, unroll=8)


def _mixer_b(proj, sink, *, batch, seq, row0):
    n = proj.shape[0]
    b0 = row0 // seq
    bias = _band_bias(BAND_TQ, BAND_TQ + 2 * B_HALF_WINDOW, B_HALF_WINDOW)
    view = proj.reshape(n // seq, seq, IN_WIDTH)
    return pl.pallas_call(
        functools.partial(_mixer_b_kernel, seq=seq),
        grid=(batch, B_WIDTH // LANES),
        in_specs=[pl.BlockSpec(memory_space=pltpu.SMEM), pl.BlockSpec(bias.shape, lambda b, g: (0, 0, 0)),
                  pl.BlockSpec((1, seq, LANES), lambda b, g: (b0 + b, 0, QB + g)),
                  pl.BlockSpec((1, seq, LANES), lambda b, g: (b0 + b, 0, KB)),
                  pl.BlockSpec((1, seq, LANES), lambda b, g: (b0 + b, 0, VB))],
        out_specs=pl.BlockSpec((1, seq, LANES), lambda b, g: (b, 0, g)),
        out_shape=jax.ShapeDtypeStruct((batch, seq, B_WIDTH), jnp.bfloat16),
        compiler_params=pltpu.CompilerParams(dimension_semantics=("arbitrary",) * 2, vmem_limit_bytes=VMEM_LIMIT),
        name="mixer_b",
    )(sink, bias, view, view, view)


def _mixer_c_kernel(q_ref, k_ref, v_ref, o_ref, *, tq, chunk):
    lhs = _stack_heads(q_ref[0])
    seq = k_ref.shape[1]
    m = den = acc = None
    for c in range(seq // chunk):
        keys = slice(c * chunk, (c + 1) * chunk)
        s = lax.dot_general(lhs, k_ref[0, keys, :], (((1,), (1,)), ((), ())), preferred_element_type=jnp.float32)
        m_c = jnp.max(s, axis=-1, keepdims=True)
        m_new = m_c if m is None else jnp.maximum(m, m_c)
        p = jnp.exp(s - m_new)
        den_c = jnp.sum(p, axis=-1, keepdims=True)
        acc_c = jnp.dot(p.astype(jnp.bfloat16), v_ref[0, keys, :], preferred_element_type=jnp.float32)
        if m is None:
            den, acc = den_c, acc_c
        else:
            alpha = jnp.exp(m - m_new)
            den, acc = alpha * den + den_c, alpha * acc + acc_c
        m = m_new
    o_ref[0] = _unstack_heads(acc * (1.0 / den), tq).astype(o_ref.dtype)


def _mixer_c(proj, *, batch, seq, row0, tq=512, chunk=1024):
    n = proj.shape[0]
    tq, chunk = min(tq, seq), min(chunk, seq)
    b0 = row0 // seq
    view = proj.reshape(n // seq, seq, IN_WIDTH)
    return pl.pallas_call(
        functools.partial(_mixer_c_kernel, tq=tq, chunk=chunk),
        grid=(batch, seq // tq, C_WIDTH // LANES),
        in_specs=[
            pl.BlockSpec((1, tq, LANES), lambda b, i, g: (b0 + b, i, QC + g)),
            pl.BlockSpec((1, seq, LANES), lambda b, i, g: (b0 + b, 0, KC)),
            pl.BlockSpec((1, seq, LANES), lambda b, i, g: (b0 + b, 0, VC)),
        ],
        out_specs=pl.BlockSpec((1, tq, LANES), lambda b, i, g: (b, i, g)),
        out_shape=jax.ShapeDtypeStruct((batch, seq, C_WIDTH), jnp.bfloat16),
        compiler_params=pltpu.CompilerParams(dimension_semantics=("arbitrary",) * 3, vmem_limit_bytes=VMEM_LIMIT),
        name="mixer_c",
    )(view, view, view)


def _rms(x, g):
    return x * lax.rsqrt(jnp.mean(x * x, axis=-1, keepdims=True) + EPS) * g


def _outproj_kernel(oa_ref, ob_ref, oc_ref, x_ref, w_ref, ga_ref, gb_ref, gc_ref, gf_ref, wr_ref, xo_ref, h_ref,
                    aff_ref):
    f32 = jnp.float32
    merged = jnp.concatenate([_rms(oa_ref[...].astype(f32), ga_ref[...]), _rms(ob_ref[...].astype(f32), gb_ref[...]),
                              _rms(oc_ref[...].astype(f32), gc_ref[...])], axis=-1).astype(jnp.bfloat16)
    xn = x_ref[...] + jnp.dot(merged, w_ref[...], preferred_element_type=f32)
    xo_ref[...] = xn
    h = _rms(xn, gf_ref[...])
    h_ref[:, :D_MODEL] = h
    h_ref[:, D_MODEL:] = jnp.zeros((h.shape[0], LANES), f32)
    logits = lax.dot_general(wr_ref[...], h, (((1,), (1,)), ((), ())), preferred_element_type=f32,
                             precision=lax.Precision.HIGHEST)
    z = jnp.exp(logits - jnp.max(logits, axis=0, keepdims=True))
    aff = z / jnp.sum(z, axis=0, keepdims=True)
    for c in range(aff.shape[1] // LANES):
        aff_ref[c] = aff[:, c * LANES:(c + 1) * LANES]


def _out_projection(oa, ob, oc, x, w, ga, gb, gc, gf, wr_t, tm):
    n = x.shape[0]
    rows = lambda width: pl.BlockSpec((tm, width), lambda i: (i, 0))
    const = lambda shape: pl.BlockSpec(shape, lambda i: (0, 0))
    return pl.pallas_call(
        _outproj_kernel,
        grid=(n // tm,),
        in_specs=[rows(A_WIDTH), rows(B_WIDTH), rows(C_WIDTH),
                  rows(D_MODEL), const((D_MODEL, D_MODEL)), const((1, A_WIDTH)), const((1, B_WIDTH)),
                  const((1, C_WIDTH)), const((1, D_MODEL)), const((N_EXPERTS, D_MODEL))],
        out_specs=[rows(D_MODEL), rows(ROW_WIDTH), pl.BlockSpec((tm // LANES, N_EXPERTS, LANES), lambda i: (i, 0, 0))],
        out_shape=[jax.ShapeDtypeStruct((n, D_MODEL), jnp.float32), jax.ShapeDtypeStruct((n, ROW_WIDTH), jnp.float32),
                   jax.ShapeDtypeStruct((n // LANES, N_EXPERTS, LANES), jnp.float32)],
        compiler_params=pltpu.CompilerParams(dimension_semantics=("arbitrary",), vmem_limit_bytes=VMEM_LIMIT),
        name="out_projection",
    )(oa, ob, oc, x, w, ga, gb, gc, gf, wr_t)


def _ffn_kernel(x_ref, wg_ref, wu_ref, wd_ref, o_ref, dest_ref, xb_scr, gate_scr):
    e = pl.program_id(0)
    j = pl.program_id(2)

    @pl.when(j == 0)
    def _():
        xb_scr[...] = x_ref[0, :, :D_MODEL].astype(jnp.bfloat16)
        route = x_ref[0, :, D_MODEL:]
        lane = lax.broadcasted_iota(jnp.int32, route.shape, 1)
        pick = lambda k: jnp.sum(jnp.where(lane == k + e, route, 0.0), axis=-1, keepdims=True)
        gate_scr[...] = pick(GATE_LANE)
        dest_ref[0] = pick(DEST_LANE).astype(jnp.int32)
        o_ref[0] = jnp.zeros(o_ref.shape[1:], o_ref.dtype)

    x = xb_scr[...]
    hg = jnp.dot(x, wg_ref[...].astype(jnp.bfloat16), preferred_element_type=jnp.float32)
    hu = jnp.dot(x, wu_ref[...].astype(jnp.bfloat16), preferred_element_type=jnp.float32)
    act = (hg * jax.nn.sigmoid(hg) * hu).astype(jnp.bfloat16)
    o_ref[0] += jnp.dot(act, wd_ref[...].astype(jnp.bfloat16), preferred_element_type=jnp.float32)

    @pl.when(j == pl.num_programs(2) - 1)
    def _():
        o_ref[0] = o_ref[0] * gate_scr[...]


def _expert_ffn(xe, w_gate, w_up, w_down, layer, tr, tf):
    n_e, rows, _ = xe.shape
    d = D_MODEL
    d_ff = w_gate.shape[-1]
    return pl.pallas_call(
        _ffn_kernel,
        grid=(n_e, rows // tr, d_ff // tf),
        in_specs=[
            pl.BlockSpec((1, tr, ROW_WIDTH), lambda e, c, j: (e, c, 0)),
            pl.BlockSpec((None, None, d, tf), lambda e, c, j: (layer, e, 0, j)),
            pl.BlockSpec((None, None, d, tf), lambda e, c, j: (layer, e, 0, j)),
            pl.BlockSpec((None, None, tf, d), lambda e, c, j: (layer, e, j, 0)),
        ],
        out_specs=[pl.BlockSpec((1, tr, d), lambda e, c, j: (e, c, 0)),
                   pl.BlockSpec((1, tr, 1), lambda e, c, j: (e, c, 0))],
        out_shape=[jax.ShapeDtypeStruct((n_e, rows, d), jnp.float32),
                   jax.ShapeDtypeStruct((n_e, rows, 1), jnp.int32)],
        scratch_shapes=[pltpu.VMEM((tr, d), jnp.bfloat16), pltpu.VMEM((tr, 1), jnp.float32)],
        compiler_params=pltpu.CompilerParams(dimension_semantics=("arbitrary",) * 3, vmem_limit_bytes=VMEM_LIMIT),
        name="expert_ffn",
    )(xe, w_gate, w_up, w_down)


def _lane_cumsum(m, tri):
    nc, r, _ = m.shape
    flat = m.reshape(nc * r, LANES).astype(jnp.bfloat16)
    return jnp.dot(flat, tri, preferred_element_type=jnp.float32).reshape(nc, r, LANES)


def _lead_cumsum_exclusive(t):
    n = t.shape[0]
    inc, k = t, 1
    while k < n:
        inc = inc + jnp.concatenate([jnp.zeros((k,) + t.shape[1:], t.dtype), inc[:n - k]], axis=0)
        k *= 2
    return inc - t


def _token_cumsum(m, tri):
    inside = _lane_cumsum(m, tri)
    total = inside[:, :, LANES - 1:]
    return _lead_cumsum_exclusive(total), inside, total


def _select_kernel(aff_ref, tri_ref, idx_ref, stats_ref, split_scr, before_scr, through_scr, *, groups):
    f32 = jnp.float32
    tri = tri_ref[...]
    for c0, nc, cap, s0 in groups:
        aff = aff_ref[c0:c0 + nc]
        bits = pltpu.bitcast(aff, jnp.int32)
        count = lambda mask: jnp.sum(jnp.sum(mask, axis=0, keepdims=True), axis=2, keepdims=True)

        def bisect(_, carry):
            lo, hi = carry
            mid = lo + ((hi - lo) >> 1)
            ok = count(jnp.where(bits >= mid, 1.0, 0.0)) >= cap
            return jnp.where(ok, mid, lo), jnp.where(ok, hi, mid)

        shape = (1, N_EXPERTS, 1)
        thr, _ = lax.fori_loop(0, 31, bisect, (jnp.zeros(shape, jnp.int32), jnp.full(shape, ONE_BITS + 1, jnp.int32)))
        above, tie = bits > thr, bits == thr
        tie_f = jnp.where(tie, 1.0, 0.0)
        need = cap - count(jnp.where(above, 1.0, 0.0))
        before, inside, _ = _token_cumsum(tie_f, tri)
        chosen = jnp.where(above | (tie & (before + inside - tie_f < need)), 1.0, 0.0)

        before, inside, total = _token_cumsum(chosen, tri)
        through = before + inside
        mult = jnp.sum(chosen, axis=1, keepdims=True)
        m_before, m_inside, _ = _token_cumsum(mult, tri)
        stats_ref[c0:c0 + nc, 0:N_EXPERTS, :] = chosen
        stats_ref[c0:c0 + nc, N_EXPERTS:N_EXPERTS + 1, :] = m_before + m_inside - mult
        stats_ref[c0:c0 + nc, N_EXPERTS + 1:N_EXPERTS + 2, :] = mult
        stats_ref[c0:c0 + nc, N_EXPERTS + 2:, :] = jnp.zeros((nc, 6, LANES), f32)

        for e in range(N_EXPERTS):
            t_e = through[:, e, :]
            hi_digit = jnp.floor(t_e * (1.0 / 64))
            split_scr[e, 0:nc, 0:LANES] = hi_digit.astype(jnp.bfloat16)
            split_scr[e, 0:nc, LANES:] = (t_e - 64.0 * hi_digit).astype(jnp.bfloat16)
            before_scr[e, 0:nc, :] = jnp.broadcast_to(before[:, e, :], (nc, LANES))
            through_scr[e, 0:nc, :] = jnp.broadcast_to((before + total)[:, e, :], (nc, LANES))

        chunk_id = lax.broadcasted_iota(jnp.int32, (1, nc), 1).astype(f32)
        lane_id = lax.broadcasted_iota(jnp.int32, (1, LANES), 1)
        row_id = lax.broadcasted_iota(jnp.int32, (LANES, 1), 0)

        def compact(it, carry):
            e, s = it // (cap // LANES), it % (cap // LANES)
            slot_row = (s * LANES + lane_id).astype(f32)
            slot_col = (s * LANES + row_id).astype(f32)
            holds = (before_scr[e, 0:nc, :] <= slot_row) & (slot_row < through_scr[e, 0:nc, :])
            onehot = jnp.where(holds, 1.0, 0.0).T
            digits = jnp.dot(onehot.astype(jnp.bfloat16), split_scr[e, 0:nc, :], preferred_element_type=f32)
            counts = 64.0 * digits[:, :LANES] + digits[:, LANES:]
            inside_pos = jnp.sum(jnp.where(counts <= slot_col, 1.0, 0.0), axis=-1, keepdims=True)
            chunk = jnp.sum(onehot * chunk_id, axis=-1, keepdims=True)
            token = (c0 + chunk) * LANES + inside_pos
            idx_ref[e, pl.ds(s0 + s, 1), :] = jnp.broadcast_to(token, (LANES, LANES)).T[0:1, :].astype(jnp.int32)
            return carry

        lax.fori_loop(0, N_EXPERTS * (cap // LANES), compact, 0, unroll=4)


def _select(aff, groups, slots):
    n_chunks = aff.shape[0]
    nc_max = max(nc for _, nc, _, _ in groups)
    tri = jnp.asarray(np.triu(np.ones((LANES, LANES))), jnp.bfloat16)
    return pl.pallas_call(
        functools.partial(_select_kernel, groups=groups),
        out_shape=[jax.ShapeDtypeStruct((N_EXPERTS, slots // LANES, LANES), jnp.int32),
                   jax.ShapeDtypeStruct((n_chunks, 24, LANES), jnp.float32)],
        scratch_shapes=[pltpu.VMEM((N_EXPERTS, nc_max, 2 * LANES), jnp.bfloat16),
                        pltpu.VMEM((N_EXPERTS, nc_max, LANES), jnp.float32),
                        pltpu.VMEM((N_EXPERTS, nc_max, LANES), jnp.float32)],
        compiler_params=pltpu.CompilerParams(vmem_limit_bytes=VMEM_LIMIT),
        name="expert_select",
    )(aff, tri)


def _route_rows_kernel(aff_ref, stats_ref, below_ref, rows_in_ref, o_ref):
    del rows_in_ref
    n = aff_ref.shape[0]
    pad = jnp.zeros((LANES - N_EXPERTS - stats_ref.shape[1], LANES), jnp.float32)
    lane = lax.broadcasted_iota(jnp.int32, (LANES, LANES), 1)
    for c in range(n):
        t = jnp.concatenate([aff_ref[c], stats_ref[c], pad], axis=0).T
        rank = jnp.dot(t.astype(jnp.bfloat16), below_ref[...], preferred_element_type=jnp.float32)
        first = t[:, OFF_LANE:OFF_LANE + 1]
        o_ref[c * LANES:(c + 1) * LANES, :] = jnp.where((lane >= DEST_LANE) & (lane < OFF_LANE), first + rank, t)


def _route_rows(aff, stats, rows, tm):
    n = rows.shape[0]
    k = tm // LANES
    below = np.zeros((LANES, LANES))
    below[DEST_LANE:OFF_LANE, DEST_LANE:OFF_LANE] = np.triu(np.ones((N_EXPERTS, N_EXPERTS)), 1)
    return pl.pallas_call(
        _route_rows_kernel,
        grid=(n // tm,),
        in_specs=[pl.BlockSpec((k, N_EXPERTS, LANES), lambda i: (i, 0, 0)),
                  pl.BlockSpec((k, stats.shape[1], LANES), lambda i: (i, 0, 0)),
                  pl.BlockSpec((LANES, LANES), lambda i: (0, 0)),
                  pl.BlockSpec(memory_space=pl.ANY)],
        out_specs=pl.BlockSpec((tm, LANES), lambda i: (i, D_MODEL // LANES)),
        out_shape=jax.ShapeDtypeStruct(rows.shape, rows.dtype),
        input_output_aliases={3: 0},
        compiler_params=pltpu.CompilerParams(dimension_semantics=("arbitrary",)),
        name="route_rows",
    )(aff, stats, jnp.asarray(below, jnp.bfloat16), rows)


SC_ROWS = 32


def _sc_mesh():
    return plsc.VectorSubcoreMesh(core_axis_name="core", subcore_axis_name="subcore")


def _sc_gather(table, idx):
    m, w = idx.shape[0], table.shape[1]
    per = m // (SC_WORKERS * SC_ROWS)
    assert per * SC_WORKERS * SC_ROWS == m

    @functools.partial(pl.kernel, out_type=jax.ShapeDtypeStruct((m, w), table.dtype), mesh=_sc_mesh(),
                       scratch_types=[pltpu.VMEM((1, SC_ROWS), jnp.int32), pltpu.VMEM((SC_ROWS, w), table.dtype)])
    def gather(table_hbm, idx_hbm, out_hbm, idx_v, buf):
        worker = lax.axis_index("core") * (SC_WORKERS // 2) + lax.axis_index("subcore")

        @pl.loop(0, per)
        def _(b):
            blk = worker * per + b
            pltpu.sync_copy(idx_hbm.at[pl.ds(blk, 1)], idx_v)
            pltpu.sync_copy(table_hbm.at[idx_v.at[0]], buf)
            pltpu.sync_copy(buf, out_hbm.at[pl.ds(blk * SC_ROWS, SC_ROWS)])

    return gather(table, idx.reshape(m // SC_ROWS, SC_ROWS))


def _sc_scatter(rows, dest):
    m, w = rows.shape
    per = m // (SC_WORKERS * SC_ROWS)
    assert per * SC_WORKERS * SC_ROWS == m

    @functools.partial(pl.kernel, out_type=jax.ShapeDtypeStruct((m, w), rows.dtype), mesh=_sc_mesh(),
                       scratch_types=[pltpu.VMEM((1, SC_ROWS), jnp.int32), pltpu.VMEM((SC_ROWS, w), rows.dtype)])
    def scatter(rows_hbm, dest_hbm, out_hbm, dest_v, buf):
        worker = lax.axis_index("core") * (SC_WORKERS // 2) + lax.axis_index("subcore")

        @pl.loop(0, per)
        def _(b):
            blk = worker * per + b
            pltpu.sync_copy(dest_hbm.at[pl.ds(blk, 1)], dest_v)
            pltpu.sync_copy(rows_hbm.at[pl.ds(blk * SC_ROWS, SC_ROWS)], buf)
            pltpu.sync_copy(buf, out_hbm.at[dest_v.at[0]])

    return scatter(rows, dest.reshape(m // SC_ROWS, SC_ROWS))


COMBINE_ROWS = 512


def _combine_kernel(tile_ref, start_ref, want_ref, flags_ref, x_ref, route_ref, z_ref, o_ref):
    s = pl.program_id(0)
    f32, bf16 = jnp.float32, jnp.bfloat16

    @pl.when(flags_ref[s] == 1)
    def _():
        o_ref[...] = x_ref[...]

    @pl.when(flags_ref[s] != 2)
    def _():
        first = route_ref[:, OFF_LANE:OFF_LANE + 1]
        last = first + route_ref[:, MULT_LANE:MULT_LANE + 1]
        row = start_ref[s] * 8 + lax.broadcasted_iota(jnp.int32, (1, COMBINE_ROWS), 1)
        rowf = row.astype(f32)
        own = jnp.where((first <= rowf) & (rowf < last) & (row >= want_ref[s]), 1.0, 0.0).astype(bf16)
        z = z_ref[...]
        z1 = z.astype(bf16)
        r1 = z - z1.astype(f32)
        z2 = r1.astype(bf16)
        z3 = (r1 - z2.astype(f32)).astype(bf16)
        o_ref[...] += (jnp.dot(own, z1, preferred_element_type=f32) + jnp.dot(own, z2, preferred_element_type=f32)
                       + jnp.dot(own, z3, preferred_element_type=f32))


def _combine_steps(tile_lo, n_rows):
    n_tiles = tile_lo.shape[0] - 1
    n_steps_max = n_rows // COMBINE_ROWS + 2 * n_tiles + n_tiles // 16 + 1
    lo = (tile_lo[:-1] // 8) * 8
    per_tile = jnp.maximum((tile_lo[1:] - lo + COMBINE_ROWS - 1) // COMBINE_ROWS, 1)
    ends = jnp.cumsum(per_tile)
    s = jnp.arange(n_steps_max, dtype=jnp.int32)
    valid = s < ends[-1]
    tile = jnp.minimum(jnp.sum(ends[None, :] <= s[:, None], axis=1).astype(jnp.int32), n_tiles - 1)
    mine = tile[:, None] == jnp.arange(n_tiles, dtype=jnp.int32)[None, :]
    of_tile = lambda a: jnp.sum(jnp.where(mine, a[None, :], 0), axis=1)
    k = s - of_tile(ends - per_tile)
    want = of_tile(lo) + k * COMBINE_ROWS
    start = jnp.minimum(want, n_rows - COMBINE_ROWS)
    last_start = jnp.sum(jnp.where(s == ends[-1] - 1, start, 0))
    start = jnp.where(valid, start, last_start)
    flags = jnp.where(valid, (k == 0).astype(jnp.int32), 2)
    return tile, start // 8, want, flags, n_steps_max


def _combine(x, rows, z, tile_lo, tt):
    tile, start, want, flags, n_steps = _combine_steps(tile_lo, z.shape[0])
    return pl.pallas_call(
        _combine_kernel,
        grid_spec=pltpu.PrefetchScalarGridSpec(
            num_scalar_prefetch=4,
            grid=(n_steps,),
            in_specs=[pl.BlockSpec((tt, D_MODEL), lambda s, tile, *_: (tile[s], 0)),
                      pl.BlockSpec((tt, LANES), lambda s, tile, *_: (tile[s], D_MODEL // LANES)),
                      pl.BlockSpec((pl.Element(COMBINE_ROWS), pl.Element(D_MODEL)),
                                   lambda s, tile, start, *_: (start[s] * 8, 0))],
            out_specs=pl.BlockSpec((tt, D_MODEL), lambda s, tile, *_: (tile[s], 0)),
        ),
        out_shape=jax.ShapeDtypeStruct(x.shape, jnp.float32),
        compiler_params=pltpu.CompilerParams(dimension_semantics=("arbitrary",), vmem_limit_bytes=VMEM_LIMIT),
        name="expert_combine",
    )(tile, start, want, flags, x, rows, z)


def _final_norm_kernel(x_ref, g_ref, o_ref):
    o_ref[...] = _rms(x_ref[...], g_ref[...])


def _final_norm(x, g, tm):
    n = x.shape[0]
    return pl.pallas_call(
        _final_norm_kernel,
        grid=(n // tm,),
        in_specs=[pl.BlockSpec((tm, D_MODEL), lambda i: (i, 0)), pl.BlockSpec((1, D_MODEL), lambda i: (0, 0))],
        out_specs=pl.BlockSpec((tm, D_MODEL), lambda i: (i, 0)),
        out_shape=jax.ShapeDtypeStruct((n, D_MODEL), jnp.float32),
        name="final_norm",
    )(x, g)


def _forward(x_prompt, x_sample, g_attn, w_in, g_q_c, g_k_c, sink_b, g_out_a, g_out_b, g_out_c, w_out, g_ffn, w_router,
             w_gate, w_up, w_down, g_final, *, tm, tr, tf):
    bf16 = jnp.bfloat16
    shapes = (x_prompt.shape[:2], x_sample.shape[:2])
    assert all(l % tm == 0 and (b * l) % LANES == 0 for b, l in shapes)
    xs = [x_prompt.reshape(-1, D_MODEL), x_sample.reshape(-1, D_MODEL)]

    tables = _rope_tables(max(l for _, l in shapes))
    seg = jnp.asarray(np.kron(np.eye(LANES // HEAD_DIM), np.ones((HEAD_DIM, HEAD_DIM))), bf16)
    pb, pc, perm_out = _out_perms()
    w_in_p = w_in[:, :, _in_perm()].astype(bf16)
    w_out_p = w_out[:, perm_out, :].astype(bf16)
    sink_p = sink_b[:, np.asarray(B_Q_ORDER)]
    tile2 = lambda g: jnp.tile(g, (1, 2))[:, None, :]
    gq, gk = tile2(g_q_c), tile2(g_k_c)
    wr_t = jnp.swapaxes(w_router, 1, 2)
    tt = 256

    def layer(x, l, batch, seq):
        n = batch * seq
        cap = CAPACITY_FACTOR * n // N_EXPERTS
        assert cap % LANES == 0
        proj, a4, a16 = _in_projection(x, g_attn[l][None], w_in_p[l], tables, gq[l], gk[l], seg, seq, tm)
        oa = _mixer_a(proj, a4, a16, batch=batch, seq=seq, row0=0).reshape(n, A_WIDTH)
        ob = _mixer_b(proj, sink_p[l], batch=batch, seq=seq, row0=0).reshape(n, B_WIDTH)
        oc = _mixer_c(proj, batch=batch, seq=seq, row0=0).reshape(n, C_WIDTH)
        x, rows, aff = _out_projection(oa, ob, oc, x, w_out_p[l], g_out_a[l][None], g_out_b[l][pb][None],
                                       g_out_c[l][pc][None], g_ffn[l][None], wr_t[l], tm)
        idx, stats = _select(aff, ((0, n // LANES, cap, 0),), cap)
        rows = _route_rows(aff, stats, rows, tm)
        xe = _sc_gather(rows, idx.reshape(-1)).reshape(N_EXPERTS, cap, ROW_WIDTH)
        ye, dest = _expert_ffn(xe, w_gate, w_up, w_down, l, min(tr, cap), tf)
        z = _sc_scatter(ye.reshape(-1, D_MODEL), dest.reshape(-1))
        first_slot = stats[::tt // LANES, N_EXPERTS, 0].astype(jnp.int32)
        tile_lo = jnp.concatenate([first_slot, jnp.full((1,), N_EXPERTS * cap, jnp.int32)])
        return _combine(x, rows, z, tile_lo, tt)

    for l in range(DEPTH):
        xs = [layer(x, l, b, s) for x, (b, s) in zip(xs, shapes)]
    return tuple(_final_norm(x, g_final[None], tm).reshape(b, s, D_MODEL) for x, (b, s) in zip(xs, shapes))


def kernel(x_prompt, x_sample, g_attn, w_in, g_q_c, g_k_c, sink_b, g_out_a, g_out_b, g_out_c, w_out, g_ffn, w_router,
           w_gate, w_up, w_down, g_final):
    return _forward(x_prompt, x_sample, g_attn, w_in, g_q_c, g_k_c, sink_b, g_out_a, g_out_b, g_out_c, w_out, g_ffn,
                    w_router, w_gate, w_up, w_down, g_final, tm=512, tr=2048, tf=256)
```

```python
import functools

import jax
import jax.numpy as jnp
import numpy as np
from jax import lax
from jax.experimental import pallas as pl
from jax.experimental.pallas import tpu as pltpu
from jax.experimental.pallas import tpu_sc as plsc

D_MODEL = 1024
DEPTH = 4
HEAD_DIM = 64
A_HEADS = 6
A_PAIRS = ((128, 1), (512, 4), (2048, 16))
B_HEADS = 4
B_KV_HEADS = 2
B_HALF_WINDOW = 128
C_HEADS = 6
C_KV_HEADS = 2
GRID_W = 64
ROPE_THETA = 10000.0
N_EXPERTS = 16
CAPACITY_FACTOR = 2
D_FF = 2816
EPS = 1e-6
NEG_INF = -1e30

LANES = 128
A_WIDTH = A_HEADS * HEAD_DIM
B_WIDTH = B_HEADS * HEAD_DIM
C_WIDTH = C_HEADS * HEAD_DIM
IN_WIDTH = 3 * A_WIDTH + B_WIDTH + 2 * B_KV_HEADS * HEAD_DIM + C_WIDTH + 2 * C_KV_HEADS * HEAD_DIM
N_GROUPS = IN_WIDTH // LANES
A_GROUPS = 3 * A_WIDTH // LANES
QA, KA, VA, QB, KB, VB, QC, KC, VC = 0, 3, 6, 9, 11, 12, 13, 16, 17
ROPE_NONE, ROPE_1D, ROPE_AXIAL_Q, ROPE_AXIAL_K = 0, 1, 2, 3
GROUP_KIND = ([(ROPE_1D, True)] * 3 + [(ROPE_1D, False)] * 3 + [(ROPE_NONE, False)] * 3
              + [(ROPE_1D, True)] * 2 + [(ROPE_1D, False)] + [(ROPE_NONE, False)]
              + [(ROPE_AXIAL_Q, True)] * 3 + [(ROPE_AXIAL_K, False)] + [(ROPE_NONE, False)])
Q_SCALE = HEAD_DIM ** -0.5
A_DILATIONS = tuple(d for _, d in A_PAIRS)
A_HALF_WINDOW = A_PAIRS[0][0] // 2
assert all(w // 2 // d == A_HALF_WINDOW for w, d in A_PAIRS) and A_DILATIONS == (1, 4, 16)
BAND_TQ = 128
ROW_WIDTH = D_MODEL + LANES
GATE_LANE, DEST_LANE, OFF_LANE, MULT_LANE = 0, N_EXPERTS, 2 * N_EXPERTS, 2 * N_EXPERTS + 1
ONE_BITS = 0x3F800000
SC_WORKERS = 32

VMEM_LIMIT = 56 * 1024 * 1024

B_Q_ORDER = (0, 2, 1, 3)
C_Q_ORDER = (0, 3, 1, 4, 2, 5)


def _head_perm(order):
    return np.concatenate([np.arange(h * HEAD_DIM, (h + 1) * HEAD_DIM) for h in order])


def _in_perm():
    widths = [A_WIDTH] * 3 + [B_WIDTH, 128, 128, C_WIDTH, 128, 128]
    offs = np.concatenate([[0], np.cumsum(widths)])
    parts = [np.arange(offs[i], offs[i + 1]) for i in range(9)]
    parts[3] = offs[3] + _head_perm(B_Q_ORDER)
    parts[6] = offs[6] + _head_perm(C_Q_ORDER)
    return np.concatenate(parts)


def _out_perms():
    pb = _head_perm(B_Q_ORDER)
    pc = _head_perm(C_Q_ORDER)
    return pb, pc, np.concatenate([np.arange(A_WIDTH), A_WIDTH + pb, A_WIDTH + B_WIDTH + pc])


def _rope_tables(seq):
    pos = jnp.arange(seq, dtype=jnp.float32)
    inv1 = ROPE_THETA ** (-jnp.arange(0, HEAD_DIM, 2, dtype=jnp.float32) / HEAD_DIM)
    ang = pos[:, None] * inv1[None, :]
    c, s = jnp.cos(ang), jnp.sin(ang)
    cos1 = jnp.tile(jnp.concatenate([c, c], -1), (1, 2))
    sin1 = jnp.tile(jnp.concatenate([-s, s], -1), (1, 2))
    half = HEAD_DIM // 2
    inv2 = ROPE_THETA ** (-jnp.arange(0, half, 2, dtype=jnp.float32) / half)
    row = jnp.floor(pos / GRID_W)
    col = pos - row * GRID_W
    ar, ac = row[:, None] * inv2[None, :], col[:, None] * inv2[None, :]
    cr, sr, cc, sc = jnp.cos(ar), jnp.sin(ar), jnp.cos(ac), jnp.sin(ac)
    cos2 = jnp.tile(jnp.concatenate([cr, cr, cc, cc], -1), (1, 2))
    sin2 = jnp.tile(jnp.concatenate([-sr, sr, -sc, sc], -1), (1, 2))
    return cos1, sin1, cos2, sin2


def _swap_halves(x, block):
    half = block // 2
    lane = lax.broadcasted_iota(jnp.int32, x.shape, 1)
    return jnp.where(lane % block < half, pltpu.roll(x, LANES - half, 1), pltpu.roll(x, half, 1))


def _inproj_kernel(x_ref, g_ref, w_ref, cos1_ref, sin1_ref, cos2_ref, sin2_ref, gq_ref, gk_ref, seg_ref, o_ref,
                   a4_ref, a16_ref, rows_ref):
    x = x_ref[...]
    tm = x.shape[0]
    y = x * lax.rsqrt(jnp.mean(x * x, axis=-1, keepdims=True) + EPS)
    h = (y * g_ref[...]).astype(jnp.bfloat16)
    for c in range(N_GROUPS // 2):
        acc = jnp.dot(h, w_ref[:, c * 2 * LANES:(c + 1) * 2 * LANES], preferred_element_type=jnp.float32)
        for half in range(2):
            grp = 2 * c + half
            cols = slice(grp * LANES, (grp + 1) * LANES)
            a = acc[:, half * LANES:(half + 1) * LANES]
            kind, is_q = GROUP_KIND[grp]
            if kind == ROPE_1D:
                a = a * cos1_ref[...] + _swap_halves(a, HEAD_DIM) * sin1_ref[...]
            elif kind in (ROPE_AXIAL_Q, ROPE_AXIAL_K):
                gain = gq_ref[...] if kind == ROPE_AXIAL_Q else gk_ref[...]
                sq = a * a
                sq_hi = sq.astype(jnp.bfloat16)
                sq_lo = (sq - sq_hi.astype(jnp.float32)).astype(jnp.bfloat16)
                ss = (jnp.dot(sq_hi, seg_ref[...], preferred_element_type=jnp.float32)
                      + jnp.dot(sq_lo, seg_ref[...], preferred_element_type=jnp.float32))
                a = a * lax.rsqrt(ss * (1.0 / HEAD_DIM) + EPS) * gain
                a = a * cos2_ref[...] + _swap_halves(a, HEAD_DIM // 2) * sin2_ref[...]
            if is_q:
                a = a * Q_SCALE
            o_ref[:, cols] = a.astype(jnp.bfloat16)
            if grp < A_GROUPS:
                rows_ref[grp] = a
        if c == (A_GROUPS - 1) // 2:
            for d, ref in ((4, a4_ref), (16, a16_ref)):
                for r in range(d):
                    for grp in range(A_GROUPS):
                        ref[r, :, grp * LANES:(grp + 1) * LANES] = (
                            rows_ref[grp, pl.ds(r, tm // d, stride=d), :].astype(jnp.bfloat16))


def _in_projection(x, g, w, tables, gq, gk, seg, seq, tm):
    n = x.shape[0]
    tab_spec = pl.BlockSpec((tm, LANES), lambda i: (i % (seq // tm), 0))
    const = lambda shape: pl.BlockSpec(shape, lambda i: (0, 0))
    wa = A_GROUPS * LANES
    return pl.pallas_call(
        _inproj_kernel,
        grid=(n // tm,),
        in_specs=[pl.BlockSpec((tm, D_MODEL), lambda i: (i, 0)), const((1, D_MODEL)), const((D_MODEL, IN_WIDTH)),
                  tab_spec, tab_spec, tab_spec, tab_spec, const((1, LANES)), const((1, LANES)), const((LANES, LANES))],
        out_specs=[pl.BlockSpec((tm, IN_WIDTH), lambda i: (i, 0)),
                   pl.BlockSpec((4, tm // 4, wa), lambda i: (0, i, 0)),
                   pl.BlockSpec((16, tm // 16, wa), lambda i: (0, i, 0))],
        out_shape=[jax.ShapeDtypeStruct((n, IN_WIDTH), jnp.bfloat16),
                   jax.ShapeDtypeStruct((4, n // 4, wa), jnp.bfloat16),
                   jax.ShapeDtypeStruct((16, n // 16, wa), jnp.bfloat16)],
        scratch_shapes=[pltpu.VMEM((A_GROUPS, tm, LANES), jnp.float32)],
        compiler_params=pltpu.CompilerParams(dimension_semantics=("arbitrary",), vmem_limit_bytes=VMEM_LIMIT),
        name="in_projection",
    )(x, g, w, *tables, gq, gk, seg)


def _stack_heads(q):
    lane = lax.broadcasted_iota(jnp.int32, q.shape, 1)
    zero = jnp.zeros_like(q)
    return jnp.concatenate([jnp.where(lane < HEAD_DIM, q, zero), jnp.where(lane >= HEAD_DIM, q, zero)], axis=0)


def _unstack_heads(x, tq):
    lane = lax.broadcasted_iota(jnp.int32, (tq, x.shape[1]), 1)
    return jnp.where(lane < HEAD_DIM, x[:tq], x[tq:])


def _unstack_column(col, tq):
    return _unstack_heads(jnp.broadcast_to(col, (2 * tq, LANES)), tq)


def _band_bias(tq, win, half_window):
    row = np.arange(2 * tq)[:, None] % tq
    col = np.arange(win)[None, :]
    kinds = [np.where(np.abs(row + off - col) <= half_window, 0.0, NEG_INF) for off in (0, half_window, 2 * half_window)]
    return jnp.asarray(np.stack(kinds), jnp.float32)


def _band_tile(q, kw, vw, bias, sink=None):
    s = lax.dot_general(_stack_heads(q), kw, (((1,), (1,)), ((), ())), preferred_element_type=jnp.float32) + bias
    m = jnp.max(s, axis=-1, keepdims=True)
    if sink is not None:
        m = jnp.maximum(m, sink)
    p = jnp.exp(s - m)
    den = jnp.sum(p, axis=-1, keepdims=True)
    if sink is not None:
        den = den + jnp.exp(sink - m)
    num = jnp.dot(p.astype(jnp.bfloat16), vw, preferred_element_type=jnp.float32)
    return num, m, den


def _tile_window(i, n_tiles, tq, win, half_window, seq):
    start = pl.multiple_of(jnp.clip(i * tq - half_window, 0, seq - win), 64)
    kind = jnp.where(i == 0, 0, jnp.where(i == n_tiles - 1, 2, 1))
    return start, kind


def _mixer_a_kernel(bias1_ref, bias4_ref, bias16_ref, q1_ref, k1_ref, v1_ref, q4_ref, k4_ref, v4_ref, q16_ref, k16_ref, v16_ref,
                    o_ref, m_scr, l_scr, n_scr, *, seq):
    tq, hw = BAND_TQ, A_HALF_WINDOW

    def run_tile(q_ref, k_ref, v_ref, b_ref, lead, i, ls):
        tqc = min(tq, ls)
        win = min(tqc + 2 * hw, ls)
        n_tiles = ls // tqc
        start, kind = _tile_window(i, n_tiles, tqc, win, hw, ls)
        num, m, den = _band_tile(q_ref[lead, pl.ds(i * tqc, tqc), :], k_ref[lead, pl.ds(start, win), :],
                                 v_ref[lead, pl.ds(start, win), :], b_ref[kind])
        return _unstack_heads(num, tqc), _unstack_column(m, tqc), _unstack_column(den, tqc), tqc

    def tile1(i, carry):
        num, m, den, _ = run_tile(q1_ref, k1_ref, v1_ref, bias1_ref, 0, i, seq)
        rows = pl.ds(pl.multiple_of(i * tq, tq), tq)
        m_scr[rows, :] = m
        l_scr[rows, :] = den
        n_scr[rows, :] = num
        return carry

    lax.fori_loop(0, seq // tq, tile1, 0, unroll=8)

    def merge(tiles):
        old = [(m_scr[rows, :], l_scr[rows, :], n_scr[rows, :]) for rows, _, _, _ in tiles]
        for (rows, num, m, den), (m_old, l_old, n_old) in zip(tiles, old):
            m_new = jnp.maximum(m_old, m)
            a, b = jnp.exp(m_old - m_new), jnp.exp(m - m_new)
            m_scr[rows, :] = m_new
            l_scr[rows, :] = a * l_old + b * den
            n_scr[rows, :] = a * n_old + b * num

    ls4 = seq // 4

    def tile4(i, carry):
        tiles = []
        for r in range(4):
            num, m, den, tqc = run_tile(q4_ref, k4_ref, v4_ref, bias4_ref, r, i, ls4)
            tiles.append((pl.ds(i * (tqc * 4) + r, tqc, stride=4), num, m, den))
        merge(tiles)
        return carry

    lax.fori_loop(0, ls4 // min(tq, ls4), tile4, 0, unroll=2)

    ls16 = seq // 16

    def class16(r2, carry):
        tiles = []
        for r in (2 * r2, 2 * r2 + 1):
            for i in range(ls16 // min(tq, ls16)):
                num, m, den, tqc = run_tile(q16_ref, k16_ref, v16_ref, bias16_ref, r, i, ls16)
                tiles.append((pl.ds(i * (tqc * 16) + r, tqc, stride=16), num, m, den))
        merge(tiles)
        return carry

    lax.fori_loop(0, 8, class16, 0, unroll=2)
    o_ref[0] = (n_scr[...] * (1.0 / l_scr[...])).astype(o_ref.dtype)


def _mixer_a(proj, a4, a16, *, batch, seq, row0):
    n = proj.shape[0]
    b0 = row0 // seq
    hw = A_HALF_WINDOW

    def class_bias(ls):
        tq = min(BAND_TQ, ls)
        return _band_bias(tq, min(tq + 2 * hw, ls), hw)

    biases = [class_bias(seq // d) for d in A_DILATIONS]
    view = proj.reshape(n // seq, seq, IN_WIDTH)
    nat = lambda off: pl.BlockSpec((1, seq, LANES), lambda b, g: (b0 + b, 0, off + g))
    cls = lambda d, off: pl.BlockSpec((d, seq // d, LANES), lambda b, g: (0, b0 + b, off + g))
    full = lambda a: pl.BlockSpec(a.shape, lambda b, g: (0, 0, 0))
    return pl.pallas_call(
        functools.partial(_mixer_a_kernel, seq=seq),
        grid=(batch, A_WIDTH // LANES),
        in_specs=[full(biases[0]), full(biases[1]), full(biases[2]), nat(QA), nat(KA), nat(VA), cls(4, QA), cls(4, KA), cls(4, VA),
                  cls(16, QA), cls(16, KA), cls(16, VA)],
        out_specs=pl.BlockSpec((1, seq, LANES), lambda b, g: (b, 0, g)),
        out_shape=jax.ShapeDtypeStruct((batch, seq, A_WIDTH), jnp.bfloat16),
        scratch_shapes=[pltpu.VMEM((seq, LANES), jnp.float32)] * 3,
        compiler_params=pltpu.CompilerParams(dimension_semantics=("arbitrary",) * 2, vmem_limit_bytes=VMEM_LIMIT),
        name="mixer_a",
    )(*biases, view, view, view, a4, a4, a4, a16, a16, a16)


def _mixer_b_kernel(sink_ref, bias_ref, q_ref, k_ref, v_ref, o_ref, *, seq):
    tq, hw = BAND_TQ, B_HALF_WINDOW
    win = tq + 2 * hw
    n_tiles = seq // tq
    g = pl.program_id(1)
    row = lax.broadcasted_iota(jnp.int32, (2 * tq, 1), 0)
    sink = jnp.where(row < tq, sink_ref[g], sink_ref[g + B_KV_HEADS])

    def tile(i, carry):
        start, kind = _tile_window(i, n_tiles, tq, win, hw, seq)
        rows = pl.ds(pl.multiple_of(i * tq, tq), tq)
        num, _, den = _band_tile(q_ref[0, rows, :], k_ref[0, pl.ds(start, win), :], v_ref[0, pl.ds(start, win), :],
                                 bias_ref[kind], sink)
        o_ref[0, rows, :] = _unstack_heads(num * (1.0 / den), tq).astype(o_ref.dtype)
        return carry

    lax.fori_loop(0, n_tiles, tile, 0, unroll=8)


def _mixer_b(proj, sink, *, batch, seq, row0):
    n = proj.shape[0]
    b0 = row0 // seq
    bias = _band_bias(BAND_TQ, BAND_TQ + 2 * B_HALF_WINDOW, B_HALF_WINDOW)
    view = proj.reshape(n // seq, seq, IN_WIDTH)
    return pl.pallas_call(
        functools.partial(_mixer_b_kernel, seq=seq),
        grid=(batch, B_WIDTH // LANES),
        in_specs=[pl.BlockSpec(memory_space=pltpu.SMEM), pl.BlockSpec(bias.shape, lambda b, g: (0, 0, 0)),
                  pl.BlockSpec((1, seq, LANES), lambda b, g: (b0 + b, 0, QB + g)),
                  pl.BlockSpec((1, seq, LANES), lambda b, g: (b0 + b, 0, KB)),
                  pl.BlockSpec((1, seq, LANES), lambda b, g: (b0 + b, 0, VB))],
        out_specs=pl.BlockSpec((1, seq, LANES), lambda b, g: (b, 0, g)),
        out_shape=jax.ShapeDtypeStruct((batch, seq, B_WIDTH), jnp.bfloat16),
        compiler_params=pltpu.CompilerParams(dimension_semantics=("arbitrary",) * 2, vmem_limit_bytes=VMEM_LIMIT),
        name="mixer_b",
    )(sink, bias, view, view, view)


def _mixer_c_kernel(q_ref, k_ref, v_ref, o_ref, *, tq, chunk):
    lhs = _stack_heads(q_ref[0])
    seq = k_ref.shape[1]
    m = den = acc = None
    for c in range(seq // chunk):
        keys = slice(c * chunk, (c + 1) * chunk)
        s = lax.dot_general(lhs, k_ref[0, keys, :], (((1,), (1,)), ((), ())), preferred_element_type=jnp.float32)
        m_c = jnp.max(s, axis=-1, keepdims=True)
        m_new = m_c if m is None else jnp.maximum(m, m_c)
        p = jnp.exp(s - m_new)
        den_c = jnp.sum(p, axis=-1, keepdims=True)
        acc_c = jnp.dot(p.astype(jnp.bfloat16), v_ref[0, keys, :], preferred_element_type=jnp.float32)
        if m is None:
            den, acc = den_c, acc_c
        else:
            alpha = jnp.exp(m - m_new)
            den, acc = alpha * den + den_c, alpha * acc + acc_c
        m = m_new
    o_ref[0] = _unstack_heads(acc * (1.0 / den), tq).astype(o_ref.dtype)


def _mixer_c(proj, *, batch, seq, row0, tq=512, chunk=1024):
    n = proj.shape[0]
    tq, chunk = min(tq, seq), min(chunk, seq)
    b0 = row0 // seq
    view = proj.reshape(n // seq, seq, IN_WIDTH)
    return pl.pallas_call(
        functools.partial(_mixer_c_kernel, tq=tq, chunk=chunk),
        grid=(batch, seq // tq, C_WIDTH // LANES),
        in_specs=[
            pl.BlockSpec((1, tq, LANES), lambda b, i, g: (b0 + b, i, QC + g)),
            pl.BlockSpec((1, seq, LANES), lambda b, i, g: (b0 + b, 0, KC)),
            pl.BlockSpec((1, seq, LANES), lambda b, i, g: (b0 + b, 0, VC)),
        ],
        out_specs=pl.BlockSpec((1, tq, LANES), lambda b, i, g: (b, i, g)),
        out_shape=jax.ShapeDtypeStruct((batch, seq, C_WIDTH), jnp.bfloat16),
        compiler_params=pltpu.CompilerParams(dimension_semantics=("arbitrary",) * 3, vmem_limit_bytes=VMEM_LIMIT),
        name="mixer_c",
    )(view, view, view)


def _rms(x, g):
    return x * lax.rsqrt(jnp.mean(x * x, axis=-1, keepdims=True) + EPS) * g


def _outproj_kernel(oa_ref, ob_ref, oc_ref, x_ref, w_ref, ga_ref, gb_ref, gc_ref, gf_ref, wr_ref, xo_ref, h_ref,
                    aff_ref):
    f32 = jnp.float32
    merged = jnp.concatenate([_rms(oa_ref[...].astype(f32), ga_ref[...]), _rms(ob_ref[...].astype(f32), gb_ref[...]),
                              _rms(oc_ref[...].astype(f32), gc_ref[...])], axis=-1).astype(jnp.bfloat16)
    xn = x_ref[...] + jnp.dot(merged, w_ref[...], preferred_element_type=f32)
    xo_ref[...] = xn
    h = _rms(xn, gf_ref[...])
    h_ref[:, :D_MODEL] = h
    h_ref[:, D_MODEL:] = jnp.zeros((h.shape[0], LANES), f32)
    logits = lax.dot_general(wr_ref[...], h, (((1,), (1,)), ((), ())), preferred_element_type=f32,
                             precision=lax.Precision.HIGHEST)
    z = jnp.exp(logits - jnp.max(logits, axis=0, keepdims=True))
    aff = z / jnp.sum(z, axis=0, keepdims=True)
    for c in range(aff.shape[1] // LANES):
        aff_ref[c] = aff[:, c * LANES:(c + 1) * LANES]


def _out_projection(oa, ob, oc, x, w, ga, gb, gc, gf, wr_t, tm):
    n = x.shape[0]
    rows = lambda width: pl.BlockSpec((tm, width), lambda i: (i, 0))
    const = lambda shape: pl.BlockSpec(shape, lambda i: (0, 0))
    return pl.pallas_call(
        _outproj_kernel,
        grid=(n // tm,),
        in_specs=[rows(A_WIDTH), rows(B_WIDTH), rows(C_WIDTH),
                  rows(D_MODEL), const((D_MODEL, D_MODEL)), const((1, A_WIDTH)), const((1, B_WIDTH)),
                  const((1, C_WIDTH)), const((1, D_MODEL)), const((N_EXPERTS, D_MODEL))],
        out_specs=[rows(D_MODEL), rows(ROW_WIDTH), pl.BlockSpec((tm // LANES, N_EXPERTS, LANES), lambda i: (i, 0, 0))],
        out_shape=[jax.ShapeDtypeStruct((n, D_MODEL), jnp.float32), jax.ShapeDtypeStruct((n, ROW_WIDTH), jnp.float32),
                   jax.ShapeDtypeStruct((n // LANES, N_EXPERTS, LANES), jnp.float32)],
        compiler_params=pltpu.CompilerParams(dimension_semantics=("arbitrary",), vmem_limit_bytes=VMEM_LIMIT),
        name="out_projection",
    )(oa, ob, oc, x, w, ga, gb, gc, gf, wr_t)


def _ffn_kernel(x_ref, wg_ref, wu_ref, wd_ref, o_ref, dest_ref, xb_scr, gate_scr):
    e = pl.program_id(0)
    j = pl.program_id(2)

    @pl.when(j == 0)
    def _():
        xb_scr[...] = x_ref[0, :, :D_MODEL].astype(jnp.bfloat16)
        route = x_ref[0, :, D_MODEL:]
        lane = lax.broadcasted_iota(jnp.int32, route.shape, 1)
        pick = lambda k: jnp.sum(jnp.where(lane == k + e, route, 0.0), axis=-1, keepdims=True)
        gate_scr[...] = pick(GATE_LANE)
        dest_ref[0] = pick(DEST_LANE).astype(jnp.int32)
        o_ref[0] = jnp.zeros(o_ref.shape[1:], o_ref.dtype)

    x = xb_scr[...]
    hg = jnp.dot(x, wg_ref[...].astype(jnp.bfloat16), preferred_element_type=jnp.float32)
    hu = jnp.dot(x, wu_ref[...].astype(jnp.bfloat16), preferred_element_type=jnp.float32)
    act = (hg * jax.nn.sigmoid(hg) * hu).astype(jnp.bfloat16)
    o_ref[0] += jnp.dot(act, wd_ref[...].astype(jnp.bfloat16), preferred_element_type=jnp.float32)

    @pl.when(j == pl.num_programs(2) - 1)
    def _():
        o_ref[0] = o_ref[0] * gate_scr[...]


def _expert_ffn(xe, w_gate, w_up, w_down, layer, tr, tf):
    n_e, rows, _ = xe.shape
    d = D_MODEL
    d_ff = w_gate.shape[-1]
    return pl.pallas_call(
        _ffn_kernel,
        grid=(n_e, rows // tr, d_ff // tf),
        in_specs=[
            pl.BlockSpec((1, tr, ROW_WIDTH), lambda e, c, j: (e, c, 0)),
            pl.BlockSpec((None, None, d, tf), lambda e, c, j: (layer, e, 0, j)),
            pl.BlockSpec((None, None, d, tf), lambda e, c, j: (layer, e, 0, j)),
            pl.BlockSpec((None, None, tf, d), lambda e, c, j: (layer, e, j, 0)),
        ],
        out_specs=[pl.BlockSpec((1, tr, d), lambda e, c, j: (e, c, 0)),
                   pl.BlockSpec((1, tr, 1), lambda e, c, j: (e, c, 0))],
        out_shape=[jax.ShapeDtypeStruct((n_e, rows, d), jnp.float32),
                   jax.ShapeDtypeStruct((n_e, rows, 1), jnp.int32)],
        scratch_shapes=[pltpu.VMEM((tr, d), jnp.bfloat16), pltpu.VMEM((tr, 1), jnp.float32)],
        compiler_params=pltpu.CompilerParams(dimension_semantics=("arbitrary",) * 3, vmem_limit_bytes=VMEM_LIMIT),
        name="expert_ffn",
    )(xe, w_gate, w_up, w_down)


def _lane_cumsum(m, tri):
    nc, r, _ = m.shape
    flat = m.reshape(nc * r, LANES).astype(jnp.bfloat16)
    return jnp.dot(flat, tri, preferred_element_type=jnp.float32).reshape(nc, r, LANES)


def _lead_cumsum_exclusive(t):
    n = t.shape[0]
    inc, k = t, 1
    while k < n:
        inc = inc + jnp.concatenate([jnp.zeros((k,) + t.shape[1:], t.dtype), inc[:n - k]], axis=0)
        k *= 2
    return inc - t


def _token_cumsum(m, tri):
    inside = _lane_cumsum(m, tri)
    total = inside[:, :, LANES - 1:]
    return _lead_cumsum_exclusive(total), inside, total


def _select_kernel(aff_ref, tri_ref, idx_ref, stats_ref, split_scr, before_scr, through_scr, *, groups):
    f32 = jnp.float32
    tri = tri_ref[...]
    for c0, nc, cap, s0 in groups:
        aff = aff_ref[c0:c0 + nc]
        bits = pltpu.bitcast(aff, jnp.int32)
        count = lambda mask: jnp.sum(jnp.sum(mask, axis=0, keepdims=True), axis=2, keepdims=True)

        def bisect(_, carry):
            lo, hi = carry
            mid = lo + ((hi - lo) >> 1)
            ok = count(jnp.where(bits >= mid, 1.0, 0.0)) >= cap
            return jnp.where(ok, mid, lo), jnp.where(ok, hi, mid)

        shape = (1, N_EXPERTS, 1)
        thr, _ = lax.fori_loop(0, 31, bisect, (jnp.zeros(shape, jnp.int32), jnp.full(shape, ONE_BITS + 1, jnp.int32)))
        above, tie = bits > thr, bits == thr
        tie_f = jnp.where(tie, 1.0, 0.0)
        need = cap - count(jnp.where(above, 1.0, 0.0))
        before, inside, _ = _token_cumsum(tie_f, tri)
        chosen = jnp.where(above | (tie & (before + inside - tie_f < need)), 1.0, 0.0)

        before, inside, total = _token_cumsum(chosen, tri)
        through = before + inside
        mult = jnp.sum(chosen, axis=1, keepdims=True)
        m_before, m_inside, _ = _token_cumsum(mult, tri)
        stats_ref[c0:c0 + nc, 0:N_EXPERTS, :] = chosen
        stats_ref[c0:c0 + nc, N_EXPERTS:N_EXPERTS + 1, :] = m_before + m_inside - mult
        stats_ref[c0:c0 + nc, N_EXPERTS + 1:N_EXPERTS + 2, :] = mult
        stats_ref[c0:c0 + nc, N_EXPERTS + 2:, :] = jnp.zeros((nc, 6, LANES), f32)

        for e in range(N_EXPERTS):
            t_e = through[:, e, :]
            hi_digit = jnp.floor(t_e * (1.0 / 64))
            split_scr[e, 0:nc, 0:LANES] = hi_digit.astype(jnp.bfloat16)
            split_scr[e, 0:nc, LANES:] = (t_e - 64.0 * hi_digit).astype(jnp.bfloat16)
            before_scr[e, 0:nc, :] = jnp.broadcast_to(before[:, e, :], (nc, LANES))
            through_scr[e, 0:nc, :] = jnp.broadcast_to((before + total)[:, e, :], (nc, LANES))

        chunk_id = lax.broadcasted_iota(jnp.int32, (1, nc), 1).astype(f32)
        lane_id = lax.broadcasted_iota(jnp.int32, (1, LANES), 1)
        row_id = lax.broadcasted_iota(jnp.int32, (LANES, 1), 0)

        def compact(it, carry):
            e, s = it // (cap // LANES), it % (cap // LANES)
            slot_row = (s * LANES + lane_id).astype(f32)
            slot_col = (s * LANES + row_id).astype(f32)
            holds = (before_scr[e, 0:nc, :] <= slot_row) & (slot_row < through_scr[e, 0:nc, :])
            onehot = jnp.where(holds, 1.0, 0.0).T
            digits = jnp.dot(onehot.astype(jnp.bfloat16), split_scr[e, 0:nc, :], preferred_element_type=f32)
            counts = 64.0 * digits[:, :LANES] + digits[:, LANES:]
            inside_pos = jnp.sum(jnp.where(counts <= slot_col, 1.0, 0.0), axis=-1, keepdims=True)
            chunk = jnp.sum(onehot * chunk_id, axis=-1, keepdims=True)
            token = (c0 + chunk) * LANES + inside_pos
            idx_ref[e, pl.ds(s0 + s, 1), :] = jnp.broadcast_to(token, (LANES, LANES)).T[0:1, :].astype(jnp.int32)
            return carry

        lax.fori_loop(0, N_EXPERTS * (cap // LANES), compact, 0, unroll=4)


def _select(aff, groups, slots):
    n_chunks = aff.shape[0]
    nc_max = max(nc for _, nc, _, _ in groups)
    tri = jnp.asarray(np.triu(np.ones((LANES, LANES))), jnp.bfloat16)
    return pl.pallas_call(
        functools.partial(_select_kernel, groups=groups),
        out_shape=[jax.ShapeDtypeStruct((N_EXPERTS, slots // LANES, LANES), jnp.int32),
                   jax.ShapeDtypeStruct((n_chunks, 24, LANES), jnp.float32)],
        scratch_shapes=[pltpu.VMEM((N_EXPERTS, nc_max, 2 * LANES), jnp.bfloat16),
                        pltpu.VMEM((N_EXPERTS, nc_max, LANES), jnp.float32),
                        pltpu.VMEM((N_EXPERTS, nc_max, LANES), jnp.float32)],
        compiler_params=pltpu.CompilerParams(vmem_limit_bytes=VMEM_LIMIT),
        name="expert_select",
    )(aff, tri)


def _route_rows_kernel(aff_ref, stats_ref, below_ref, rows_in_ref, o_ref):
    del rows_in_ref
    n = aff_ref.shape[0]
    pad = jnp.zeros((LANES - N_EXPERTS - stats_ref.shape[1], LANES), jnp.float32)
    lane = lax.broadcasted_iota(jnp.int32, (LANES, LANES), 1)
    for c in range(n):
        t = jnp.concatenate([aff_ref[c], stats_ref[c], pad], axis=0).T
        rank = jnp.dot(t.astype(jnp.bfloat16), below_ref[...], preferred_element_type=jnp.float32)
        first = t[:, OFF_LANE:OFF_LANE + 1]
        o_ref[c * LANES:(c + 1) * LANES, :] = jnp.where((lane >= DEST_LANE) & (lane < OFF_LANE), first + rank, t)


def _route_rows(aff, stats, rows, tm):
    n = rows.shape[0]
    k = tm // LANES
    below = np.zeros((LANES, LANES))
    below[DEST_LANE:OFF_LANE, DEST_LANE:OFF_LANE] = np.triu(np.ones((N_EXPERTS, N_EXPERTS)), 1)
    return pl.pallas_call(
        _route_rows_kernel,
        grid=(n // tm,),
        in_specs=[pl.BlockSpec((k, N_EXPERTS, LANES), lambda i: (i, 0, 0)),
                  pl.BlockSpec((k, stats.shape[1], LANES), lambda i: (i, 0, 0)),
                  pl.BlockSpec((LANES, LANES), lambda i: (0, 0)),
                  pl.BlockSpec(memory_space=pl.ANY)],
        out_specs=pl.BlockSpec((tm, LANES), lambda i: (i, D_MODEL // LANES)),
        out_shape=jax.ShapeDtypeStruct(rows.shape, rows.dtype),
        input_output_aliases={3: 0},
        compiler_params=pltpu.CompilerParams(dimension_semantics=("arbitrary",)),
        name="route_rows",
    )(aff, stats, jnp.asarray(below, jnp.bfloat16), rows)


SC_ROWS = 32


def _sc_mesh():
    return plsc.VectorSubcoreMesh(core_axis_name="core", subcore_axis_name="subcore")


def _sc_gather(table, idx):
    m, w = idx.shape[0], table.shape[1]
    per = m // (SC_WORKERS * SC_ROWS)
    assert per * SC_WORKERS * SC_ROWS == m

    @functools.partial(pl.kernel, out_type=jax.ShapeDtypeStruct((m, w), table.dtype), mesh=_sc_mesh(),
                       scratch_types=[pltpu.VMEM((1, SC_ROWS), jnp.int32), pltpu.VMEM((SC_ROWS, w), table.dtype)])
    def gather(table_hbm, idx_hbm, out_hbm, idx_v, buf):
        worker = lax.axis_index("core") * (SC_WORKERS // 2) + lax.axis_index("subcore")

        @pl.loop(0, per)
        def _(b):
            blk = worker * per + b
            pltpu.sync_copy(idx_hbm.at[pl.ds(blk, 1)], idx_v)
            pltpu.sync_copy(table_hbm.at[idx_v.at[0]], buf)
            pltpu.sync_copy(buf, out_hbm.at[pl.ds(blk * SC_ROWS, SC_ROWS)])

    return gather(table, idx.reshape(m // SC_ROWS, SC_ROWS))


def _sc_scatter(rows, dest):
    m, w = rows.shape
    per = m // (SC_WORKERS * SC_ROWS)
    assert per * SC_WORKERS * SC_ROWS == m

    @functools.partial(pl.kernel, out_type=jax.ShapeDtypeStruct((m, w), rows.dtype), mesh=_sc_mesh(),
                       scratch_types=[pltpu.VMEM((1, SC_ROWS), jnp.int32), pltpu.VMEM((SC_ROWS, w), rows.dtype)])
    def scatter(rows_hbm, dest_hbm, out_hbm, dest_v, buf):
        worker = lax.axis_index("core") * (SC_WORKERS // 2) + lax.axis_index("subcore")

        @pl.loop(0, per)
        def _(b):
            blk = worker * per + b
            pltpu.sync_copy(dest_hbm.at[pl.ds(blk, 1)], dest_v)
            pltpu.sync_copy(rows_hbm.at[pl.ds(blk * SC_ROWS, SC_ROWS)], buf)
            pltpu.sync_copy(buf, out_hbm.at[dest_v.at[0]])

    return scatter(rows, dest.reshape(m // SC_ROWS, SC_ROWS))


COMBINE_ROWS = 512


def _combine_kernel(tile_ref, start_ref, want_ref, flags_ref, x_ref, route_ref, z_ref, o_ref):
    s = pl.program_id(0)
    f32, bf16 = jnp.float32, jnp.bfloat16

    @pl.when(flags_ref[s] == 1)
    def _():
        o_ref[...] = x_ref[...]

    @pl.when(flags_ref[s] != 2)
    def _():
        first = route_ref[:, OFF_LANE:OFF_LANE + 1]
        last = first + route_ref[:, MULT_LANE:MULT_LANE + 1]
        row = start_ref[s] * 8 + lax.broadcasted_iota(jnp.int32, (1, COMBINE_ROWS), 1)
        rowf = row.astype(f32)
        own = jnp.where((first <= rowf) & (rowf < last) & (row >= want_ref[s]), 1.0, 0.0).astype(bf16)
        z = z_ref[...]
        z1 = z.astype(bf16)
        r1 = z - z1.astype(f32)
        z2 = r1.astype(bf16)
        z3 = (r1 - z2.astype(f32)).astype(bf16)
        o_ref[...] += (jnp.dot(own, z1, preferred_element_type=f32) + jnp.dot(own, z2, preferred_element_type=f32)
                       + jnp.dot(own, z3, preferred_element_type=f32))


def _combine_steps(tile_lo, n_rows):
    n_tiles = tile_lo.shape[0] - 1
    n_steps_max = n_rows // COMBINE_ROWS + 2 * n_tiles + n_tiles // 16 + 1
    lo = (tile_lo[:-1] // 8) * 8
    per_tile = jnp.maximum((tile_lo[1:] - lo + COMBINE_ROWS - 1) // COMBINE_ROWS, 1)
    ends = jnp.cumsum(per_tile)
    s = jnp.arange(n_steps_max, dtype=jnp.int32)
    valid = s < ends[-1]
    tile = jnp.minimum(jnp.sum(ends[None, :] <= s[:, None], axis=1).astype(jnp.int32), n_tiles - 1)
    mine = tile[:, None] == jnp.arange(n_tiles, dtype=jnp.int32)[None, :]
    of_tile = lambda a: jnp.sum(jnp.where(mine, a[None, :], 0), axis=1)
    k = s - of_tile(ends - per_tile)
    want = of_tile(lo) + k * COMBINE_ROWS
    start = jnp.minimum(want, n_rows - COMBINE_ROWS)
    last_start = jnp.sum(jnp.where(s == ends[-1] - 1, start, 0))
    start = jnp.where(valid, start, last_start)
    flags = jnp.where(valid, (k == 0).astype(jnp.int32), 2)
    return tile, start // 8, want, flags, n_steps_max


def _combine(x, rows, z, tile_lo, tt):
    tile, start, want, flags, n_steps = _combine_steps(tile_lo, z.shape[0])
    return pl.pallas_call(
        _combine_kernel,
        grid_spec=pltpu.PrefetchScalarGridSpec(
            num_scalar_prefetch=4,
            grid=(n_steps,),
            in_specs=[pl.BlockSpec((tt, D_MODEL), lambda s, tile, *_: (tile[s], 0)),
                      pl.BlockSpec((tt, LANES), lambda s, tile, *_: (tile[s], D_MODEL // LANES)),
                      pl.BlockSpec((pl.Element(COMBINE_ROWS), pl.Element(D_MODEL)),
                                   lambda s, tile, start, *_: (start[s] * 8, 0))],
            out_specs=pl.BlockSpec((tt, D_MODEL), lambda s, tile, *_: (tile[s], 0)),
        ),
        out_shape=jax.ShapeDtypeStruct(x.shape, jnp.float32),
        compiler_params=pltpu.CompilerParams(dimension_semantics=("arbitrary",), vmem_limit_bytes=VMEM_LIMIT),
        name="expert_combine",
    )(tile, start, want, flags, x, rows, z)


def _final_norm_kernel(x_ref, g_ref, o_ref):
    o_ref[...] = _rms(x_ref[...], g_ref[...])


def _final_norm(x, g, tm):
    n = x.shape[0]
    return pl.pallas_call(
        _final_norm_kernel,
        grid=(n // tm,),
        in_specs=[pl.BlockSpec((tm, D_MODEL), lambda i: (i, 0)), pl.BlockSpec((1, D_MODEL), lambda i: (0, 0))],
        out_specs=pl.BlockSpec((tm, D_MODEL), lambda i: (i, 0)),
        out_shape=jax.ShapeDtypeStruct((n, D_MODEL), jnp.float32),
        name="final_norm",
    )(x, g)


def _forward(x_prompt, x_sample, g_attn, w_in, g_q_c, g_k_c, sink_b, g_out_a, g_out_b, g_out_c, w_out, g_ffn, w_router,
             w_gate, w_up, w_down, g_final, *, tm, tr, tf):
    bf16 = jnp.bfloat16
    shapes = (x_prompt.shape[:2], x_sample.shape[:2])
    assert all(l % tm == 0 and (b * l) % LANES == 0 for b, l in shapes)
    xs = [x_prompt.reshape(-1, D_MODEL), x_sample.reshape(-1, D_MODEL)]

    tables = _rope_tables(max(l for _, l in shapes))
    seg = jnp.asarray(np.kron(np.eye(LANES // HEAD_DIM), np.ones((HEAD_DIM, HEAD_DIM))), bf16)
    pb, pc, perm_out = _out_perms()
    w_in_p = w_in[:, :, _in_perm()].astype(bf16)
    w_out_p = w_out[:, perm_out, :].astype(bf16)
    sink_p = sink_b[:, np.asarray(B_Q_ORDER)]
    tile2 = lambda g: jnp.tile(g, (1, 2))[:, None, :]
    gq, gk = tile2(g_q_c), tile2(g_k_c)
    wr_t = jnp.swapaxes(w_router, 1, 2)
    tt = 256

    def layer(x, l, batch, seq):
        n = batch * seq
        cap = CAPACITY_FACTOR * n // N_EXPERTS
        assert cap % LANES == 0
        proj, a4, a16 = _in_projection(x, g_attn[l][None], w_in_p[l], tables, gq[l], gk[l], seg, seq, tm)
        oa = _mixer_a(proj, a4, a16, batch=batch, seq=seq, row0=0).reshape(n, A_WIDTH)
        ob = _mixer_b(proj, sink_p[l], batch=batch, seq=seq, row0=0).reshape(n, B_WIDTH)
        oc = _mixer_c(proj, batch=batch, seq=seq, row0=0).reshape(n, C_WIDTH)
        x, rows, aff = _out_projection(oa, ob, oc, x, w_out_p[l], g_out_a[l][None], g_out_b[l][pb][None],
                                       g_out_c[l][pc][None], g_ffn[l][None], wr_t[l], tm)
        idx, stats = _select(aff, ((0, n // LANES, cap, 0),), cap)
        rows = _route_rows(aff, stats, rows, tm)
        xe = _sc_gather(rows, idx.reshape(-1)).reshape(N_EXPERTS, cap, ROW_WIDTH)
        ye, dest = _expert_ffn(xe, w_gate, w_up, w_down, l, min(tr, cap), tf)
        z = _sc_scatter(ye.reshape(-1, D_MODEL), dest.reshape(-1))
        first_slot = stats[::tt // LANES, N_EXPERTS, 0].astype(jnp.int32)
        tile_lo = jnp.concatenate([first_slot, jnp.full((1,), N_EXPERTS * cap, jnp.int32)])
        return _combine(x, rows, z, tile_lo, tt)

    for l in range(DEPTH):
        xs = [layer(x, l, b, s) for x, (b, s) in zip(xs, shapes)]
    return tuple(_final_norm(x, g_final[None], tm).reshape(b, s, D_MODEL) for x, (b, s) in zip(xs, shapes))


def kernel(x_prompt, x_sample, g_attn, w_in, g_q_c, g_k_c, sink_b, g_out_a, g_out_b, g_out_c, w_out, g_ffn, w_router,
           w_gate, w_up, w_down, g_final):
    return _forward(x_prompt, x_sample, g_attn, w_in, g_q_c, g_k_c, sink_b, g_out_a, g_out_b, g_out_c, w_out, g_ffn,
                    w_router, w_gate, w_up, w_down, g_final, tm=1024, tr=2048, tf=256)
```

```python
import functools
import math

import jax
import jax.numpy as jnp
import numpy as np
from jax import lax
from jax.experimental import pallas as pl
from jax.experimental.pallas import tpu as pltpu
from jax.experimental.pallas import tpu_sc as plsc

D_MODEL = 1024
DEPTH = 4
HEAD_DIM = 64
A_HEADS = 6
A_PAIRS = ((128, 1), (512, 4), (2048, 16))
B_HEADS = 4
B_KV_HEADS = 2
B_HALF_WINDOW = 128
C_HEADS = 6
C_KV_HEADS = 2
GRID_W = 64
ROPE_THETA = 10000.0
N_EXPERTS = 16
CAPACITY_FACTOR = 2
D_FF = 2816
EPS = 1e-6
NEG_INF = -1e30

LANES = 128
SUBLANES = 8
A_WIDTH = A_HEADS * HEAD_DIM
B_WIDTH = B_HEADS * HEAD_DIM
C_WIDTH = C_HEADS * HEAD_DIM
IN_WIDTH = 3 * A_WIDTH + B_WIDTH + 2 * B_KV_HEADS * HEAD_DIM + C_WIDTH + 2 * C_KV_HEADS * HEAD_DIM
N_GROUPS = IN_WIDTH // LANES
A_GROUPS = 3 * A_WIDTH // LANES
QA, KA, VA, QB, KB, VB, QC, KC, VC = 0, 3, 6, 9, 11, 12, 13, 16, 17
ROPE_NONE, ROPE_1D, ROPE_AXIAL_Q, ROPE_AXIAL_K = 0, 1, 2, 3
GROUP_KIND = ([(ROPE_1D, True)] * 3 + [(ROPE_1D, False)] * 3 + [(ROPE_NONE, False)] * 3
              + [(ROPE_1D, True)] * 2 + [(ROPE_1D, False)] + [(ROPE_NONE, False)]
              + [(ROPE_AXIAL_Q, True)] * 3 + [(ROPE_AXIAL_K, False)] + [(ROPE_NONE, False)])
Q_SCALE = HEAD_DIM ** -0.5
A_DILATIONS = tuple(d for _, d in A_PAIRS)
A_HALF_WINDOW = A_PAIRS[0][0] // 2
assert all(w // 2 // d == A_HALF_WINDOW for w, d in A_PAIRS) and A_DILATIONS == (1, 4, 16)
BAND_TQ = 128
ROW_WIDTH = D_MODEL + LANES
GATE_LANE, DEST_LANE, OFF_LANE, MULT_LANE = 0, N_EXPERTS, 2 * N_EXPERTS, 2 * N_EXPERTS + 1
ONE_BITS = 0x3F800000
BISECT_STEPS = 31
STATS_ROWS = 24
COUNT_DIGIT = 64
SC_WORKERS = 32

VMEM_LIMIT = 56 * 1024 * 1024

B_Q_ORDER = (0, 2, 1, 3)
C_Q_ORDER = (0, 3, 1, 4, 2, 5)


def _head_perm(order):
    return np.concatenate([np.arange(h * HEAD_DIM, (h + 1) * HEAD_DIM) for h in order])


def _in_perm():
    widths = [A_WIDTH] * 3 + [B_WIDTH, 128, 128, C_WIDTH, 128, 128]
    offs = np.concatenate([[0], np.cumsum(widths)])
    parts = [np.arange(offs[i], offs[i + 1]) for i in range(9)]
    parts[3] = offs[3] + _head_perm(B_Q_ORDER)
    parts[6] = offs[6] + _head_perm(C_Q_ORDER)
    return np.concatenate(parts)


def _out_perms():
    pb = _head_perm(B_Q_ORDER)
    pc = _head_perm(C_Q_ORDER)
    return pb, pc, np.concatenate([np.arange(A_WIDTH), A_WIDTH + pb, A_WIDTH + B_WIDTH + pc])


def _rope_tables(seq):
    pos = jnp.arange(seq, dtype=jnp.float32)
    inv1 = ROPE_THETA ** (-jnp.arange(0, HEAD_DIM, 2, dtype=jnp.float32) / HEAD_DIM)
    ang = pos[:, None] * inv1[None, :]
    c, s = jnp.cos(ang), jnp.sin(ang)
    cos1 = jnp.tile(jnp.concatenate([c, c], -1), (1, 2))
    sin1 = jnp.tile(jnp.concatenate([-s, s], -1), (1, 2))
    half = HEAD_DIM // 2
    inv2 = ROPE_THETA ** (-jnp.arange(0, half, 2, dtype=jnp.float32) / half)
    row = jnp.floor(pos / GRID_W)
    col = pos - row * GRID_W
    ar, ac = row[:, None] * inv2[None, :], col[:, None] * inv2[None, :]
    cr, sr, cc, sc = jnp.cos(ar), jnp.sin(ar), jnp.cos(ac), jnp.sin(ac)
    cos2 = jnp.tile(jnp.concatenate([cr, cr, cc, cc], -1), (1, 2))
    sin2 = jnp.tile(jnp.concatenate([-sr, sr, -sc, sc], -1), (1, 2))
    return cos1, sin1, cos2, sin2


def _swap_halves(x, block):
    half = block // 2
    lane = lax.broadcasted_iota(jnp.int32, x.shape, 1)
    return jnp.where(lane % block < half, pltpu.roll(x, LANES - half, 1), pltpu.roll(x, half, 1))


def _inproj_kernel(x_ref, g_ref, w_ref, cos1_ref, sin1_ref, cos2_ref, sin2_ref, gq_ref, gk_ref, seg_ref, o_ref,
                   a4_ref, a16_ref, rows_ref):
    x = x_ref[...]
    tm = x.shape[0]
    y = x * lax.rsqrt(jnp.mean(x * x, axis=-1, keepdims=True) + EPS)
    h = (y * g_ref[...]).astype(jnp.bfloat16)
    for c in range(N_GROUPS // 2):
        acc = jnp.dot(h, w_ref[:, c * 2 * LANES:(c + 1) * 2 * LANES], preferred_element_type=jnp.float32)
        for half in range(2):
            grp = 2 * c + half
            cols = slice(grp * LANES, (grp + 1) * LANES)
            a = acc[:, half * LANES:(half + 1) * LANES]
            kind, is_q = GROUP_KIND[grp]
            if kind == ROPE_1D:
                a = a * cos1_ref[...] + _swap_halves(a, HEAD_DIM) * sin1_ref[...]
            elif kind in (ROPE_AXIAL_Q, ROPE_AXIAL_K):
                gain = gq_ref[...] if kind == ROPE_AXIAL_Q else gk_ref[...]
                sq = a * a
                sq_hi = sq.astype(jnp.bfloat16)
                sq_lo = (sq - sq_hi.astype(jnp.float32)).astype(jnp.bfloat16)
                ss = (jnp.dot(sq_hi, seg_ref[...], preferred_element_type=jnp.float32)
                      + jnp.dot(sq_lo, seg_ref[...], preferred_element_type=jnp.float32))
                a = a * lax.rsqrt(ss * (1.0 / HEAD_DIM) + EPS) * gain
                a = a * cos2_ref[...] + _swap_halves(a, HEAD_DIM // 2) * sin2_ref[...]
            if is_q:
                a = a * Q_SCALE
            o_ref[:, cols] = a.astype(jnp.bfloat16)
            if grp < A_GROUPS:
                rows_ref[grp] = a
        if c == (A_GROUPS - 1) // 2:
            for d, ref in ((4, a4_ref), (16, a16_ref)):
                for r in range(d):
                    for grp in range(A_GROUPS):
                        ref[r, :, grp * LANES:(grp + 1) * LANES] = (
                            rows_ref[grp, pl.ds(r, tm // d, stride=d), :].astype(jnp.bfloat16))


def _in_projection(x, g, w, tables, gq, gk, seg, seq, tm):
    n = x.shape[0]
    tab_spec = pl.BlockSpec((tm, LANES), lambda i: (i % (seq // tm), 0))
    const = lambda shape: pl.BlockSpec(shape, lambda i: (0, 0))
    wa = A_GROUPS * LANES
    return pl.pallas_call(
        _inproj_kernel,
        grid=(n // tm,),
        in_specs=[pl.BlockSpec((tm, D_MODEL), lambda i: (i, 0)), const((1, D_MODEL)), const((D_MODEL, IN_WIDTH)),
                  tab_spec, tab_spec, tab_spec, tab_spec, const((1, LANES)), const((1, LANES)), const((LANES, LANES))],
        out_specs=[pl.BlockSpec((tm, IN_WIDTH), lambda i: (i, 0)),
                   pl.BlockSpec((4, tm // 4, wa), lambda i: (0, i, 0)),
                   pl.BlockSpec((16, tm // 16, wa), lambda i: (0, i, 0))],
        out_shape=[jax.ShapeDtypeStruct((n, IN_WIDTH), jnp.bfloat16),
                   jax.ShapeDtypeStruct((4, n // 4, wa), jnp.bfloat16),
                   jax.ShapeDtypeStruct((16, n // 16, wa), jnp.bfloat16)],
        scratch_shapes=[pltpu.VMEM((A_GROUPS, tm, LANES), jnp.float32)],
        compiler_params=pltpu.CompilerParams(dimension_semantics=("arbitrary",), vmem_limit_bytes=VMEM_LIMIT),
        name="in_projection",
    )(x, g, w, *tables, gq, gk, seg)


def _stack_heads(q):
    lane = lax.broadcasted_iota(jnp.int32, q.shape, 1)
    zero = jnp.zeros_like(q)
    return jnp.concatenate([jnp.where(lane < HEAD_DIM, q, zero), jnp.where(lane >= HEAD_DIM, q, zero)], axis=0)


def _unstack_heads(x, tq):
    lane = lax.broadcasted_iota(jnp.int32, (tq, x.shape[1]), 1)
    return jnp.where(lane < HEAD_DIM, x[:tq], x[tq:])


def _unstack_column(col, tq):
    return _unstack_heads(jnp.broadcast_to(col, (2 * tq, LANES)), tq)


def _band_bias(tq, win, half_window):
    row = np.arange(2 * tq)[:, None] % tq
    col = np.arange(win)[None, :]
    kinds = [np.where(np.abs(row + off - col) <= half_window, 0.0, NEG_INF) for off in (0, half_window, 2 * half_window)]
    return jnp.asarray(np.stack(kinds), jnp.float32)


def _band_tile(q, kw, vw, bias, sink=None):
    s = lax.dot_general(_stack_heads(q), kw, (((1,), (1,)), ((), ())), preferred_element_type=jnp.float32) + bias
    m = jnp.max(s, axis=-1, keepdims=True)
    if sink is not None:
        m = jnp.maximum(m, sink)
    p = jnp.exp(s - m)
    den = jnp.sum(p, axis=-1, keepdims=True)
    if sink is not None:
        den = den + jnp.exp(sink - m)
    num = jnp.dot(p.astype(jnp.bfloat16), vw, preferred_element_type=jnp.float32)
    return num, m, den


def _tile_window(i, n_tiles, tq, win, half_window, seq):
    start = pl.multiple_of(jnp.clip(i * tq - half_window, 0, seq - win), math.gcd(tq, half_window, seq - win))
    kind = jnp.where(i == 0, 0, jnp.where(i == n_tiles - 1, 2, 1))
    return start, kind


def _mixer_a_kernel(bias1_ref, bias4_ref, bias16_ref, q1_ref, k1_ref, v1_ref, q4_ref, k4_ref, v4_ref, q16_ref, k16_ref, v16_ref,
                    o_ref, m_scr, l_scr, n_scr, *, seq):
    tq, hw = BAND_TQ, A_HALF_WINDOW

    def run_tile(q_ref, k_ref, v_ref, b_ref, lead, i, ls):
        tqc = min(tq, ls)
        win = min(tqc + 2 * hw, ls)
        n_tiles = ls // tqc
        start, kind = _tile_window(i, n_tiles, tqc, win, hw, ls)
        num, m, den = _band_tile(q_ref[lead, pl.ds(i * tqc, tqc), :], k_ref[lead, pl.ds(start, win), :],
                                 v_ref[lead, pl.ds(start, win), :], b_ref[kind])
        return _unstack_heads(num, tqc), _unstack_column(m, tqc), _unstack_column(den, tqc), tqc

    def tile1(i, carry):
        num, m, den, _ = run_tile(q1_ref, k1_ref, v1_ref, bias1_ref, 0, i, seq)
        rows = pl.ds(pl.multiple_of(i * tq, tq), tq)
        m_scr[rows, :] = m
        l_scr[rows, :] = den
        n_scr[rows, :] = num
        return carry

    lax.fori_loop(0, seq // tq, tile1, 0, unroll=8)

    def merge(tiles):
        old = [(m_scr[rows, :], l_scr[rows, :], n_scr[rows, :]) for rows, _, _, _ in tiles]
        for (rows, num, m, den), (m_old, l_old, n_old) in zip(tiles, old):
            m_new = jnp.maximum(m_old, m)
            a, b = jnp.exp(m_old - m_new), jnp.exp(m - m_new)
            m_scr[rows, :] = m_new
            l_scr[rows, :] = a * l_old + b * den
            n_scr[rows, :] = a * n_old + b * num

    ls4 = seq // 4

    def tile4(i, carry):
        tiles = []
        for r in range(4):
            num, m, den, tqc = run_tile(q4_ref, k4_ref, v4_ref, bias4_ref, r, i, ls4)
            tiles.append((pl.ds(i * (tqc * 4) + r, tqc, stride=4), num, m, den))
        merge(tiles)
        return carry

    lax.fori_loop(0, ls4 // min(tq, ls4), tile4, 0, unroll=2)

    ls16 = seq // 16

    def class16(r2, carry):
        tiles = []
        for r in (2 * r2, 2 * r2 + 1):
            for i in range(ls16 // min(tq, ls16)):
                num, m, den, tqc = run_tile(q16_ref, k16_ref, v16_ref, bias16_ref, r, i, ls16)
                tiles.append((pl.ds(i * (tqc * 16) + r, tqc, stride=16), num, m, den))
        merge(tiles)
        return carry

    lax.fori_loop(0, 8, class16, 0, unroll=2)
    o_ref[0] = (n_scr[...] * (1.0 / l_scr[...])).astype(o_ref.dtype)


def _mixer_a(proj, a4, a16, *, batch, seq):
    n = proj.shape[0]
    hw = A_HALF_WINDOW

    def class_bias(ls):
        tq = min(BAND_TQ, ls)
        return _band_bias(tq, min(tq + 2 * hw, ls), hw)

    biases = [class_bias(seq // d) for d in A_DILATIONS]
    view = proj.reshape(n // seq, seq, IN_WIDTH)
    nat = lambda off: pl.BlockSpec((1, seq, LANES), lambda b, g: (b, 0, off + g))
    cls = lambda d, off: pl.BlockSpec((d, seq // d, LANES), lambda b, g: (0, b, off + g))
    full = lambda a: pl.BlockSpec(a.shape, lambda b, g: (0, 0, 0))
    return pl.pallas_call(
        functools.partial(_mixer_a_kernel, seq=seq),
        grid=(batch, A_WIDTH // LANES),
        in_specs=[full(biases[0]), full(biases[1]), full(biases[2]), nat(QA), nat(KA), nat(VA), cls(4, QA), cls(4, KA), cls(4, VA),
                  cls(16, QA), cls(16, KA), cls(16, VA)],
        out_specs=pl.BlockSpec((1, seq, LANES), lambda b, g: (b, 0, g)),
        out_shape=jax.ShapeDtypeStruct((batch, seq, A_WIDTH), jnp.bfloat16),
        scratch_shapes=[pltpu.VMEM((seq, LANES), jnp.float32)] * 3,
        compiler_params=pltpu.CompilerParams(dimension_semantics=("arbitrary",) * 2, vmem_limit_bytes=VMEM_LIMIT),
        name="mixer_a",
    )(*biases, view, view, view, a4, a4, a4, a16, a16, a16)


def _mixer_b_kernel(sink_ref, bias_ref, q_ref, k_ref, v_ref, o_ref, *, seq):
    tq, hw = BAND_TQ, B_HALF_WINDOW
    win = tq + 2 * hw
    n_tiles = seq // tq
    g = pl.program_id(1)
    row = lax.broadcasted_iota(jnp.int32, (2 * tq, 1), 0)
    sink = jnp.where(row < tq, sink_ref[g], sink_ref[g + B_KV_HEADS])

    def tile(i, carry):
        start, kind = _tile_window(i, n_tiles, tq, win, hw, seq)
        rows = pl.ds(pl.multiple_of(i * tq, tq), tq)
        num, _, den = _band_tile(q_ref[0, rows, :], k_ref[0, pl.ds(start, win), :], v_ref[0, pl.ds(start, win), :],
                                 bias_ref[kind], sink)
        o_ref[0, rows, :] = _unstack_heads(num * (1.0 / den), tq).astype(o_ref.dtype)
        return carry

    lax.fori_loop(0, n_tiles, tile, 0, unroll=8)


def _mixer_b(proj, sink, *, batch, seq):
    n = proj.shape[0]
    bias = _band_bias(BAND_TQ, BAND_TQ + 2 * B_HALF_WINDOW, B_HALF_WINDOW)
    view = proj.reshape(n // seq, seq, IN_WIDTH)
    return pl.pallas_call(
        functools.partial(_mixer_b_kernel, seq=seq),
        grid=(batch, B_WIDTH // LANES),
        in_specs=[pl.BlockSpec(memory_space=pltpu.SMEM), pl.BlockSpec(bias.shape, lambda b, g: (0, 0, 0)),
                  pl.BlockSpec((1, seq, LANES), lambda b, g: (b, 0, QB + g)),
                  pl.BlockSpec((1, seq, LANES), lambda b, g: (b, 0, KB)),
                  pl.BlockSpec((1, seq, LANES), lambda b, g: (b, 0, VB))],
        out_specs=pl.BlockSpec((1, seq, LANES), lambda b, g: (b, 0, g)),
        out_shape=jax.ShapeDtypeStruct((batch, seq, B_WIDTH), jnp.bfloat16),
        compiler_params=pltpu.CompilerParams(dimension_semantics=("arbitrary",) * 2, vmem_limit_bytes=VMEM_LIMIT),
        name="mixer_b",
    )(sink, bias, view, view, view)


def _mixer_c_kernel(q_ref, k_ref, v_ref, o_ref, *, tq, chunk):
    lhs = _stack_heads(q_ref[0])
    seq = k_ref.shape[1]
    m = den = acc = None
    for c in range(seq // chunk):
        keys = slice(c * chunk, (c + 1) * chunk)
        s = lax.dot_general(lhs, k_ref[0, keys, :], (((1,), (1,)), ((), ())), preferred_element_type=jnp.float32)
        m_c = jnp.max(s, axis=-1, keepdims=True)
        m_new = m_c if m is None else jnp.maximum(m, m_c)
        p = jnp.exp(s - m_new)
        den_c = jnp.sum(p, axis=-1, keepdims=True)
        acc_c = jnp.dot(p.astype(jnp.bfloat16), v_ref[0, keys, :], preferred_element_type=jnp.float32)
        if m is None:
            den, acc = den_c, acc_c
        else:
            alpha = jnp.exp(m - m_new)
            den, acc = alpha * den + den_c, alpha * acc + acc_c
        m = m_new
    o_ref[0] = _unstack_heads(acc * (1.0 / den), tq).astype(o_ref.dtype)


def _mixer_c(proj, *, batch, seq, tq=512, chunk=1024):
    n = proj.shape[0]
    tq, chunk = min(tq, seq), min(chunk, seq)
    view = proj.reshape(n // seq, seq, IN_WIDTH)
    return pl.pallas_call(
        functools.partial(_mixer_c_kernel, tq=tq, chunk=chunk),
        grid=(batch, seq // tq, C_WIDTH // LANES),
        in_specs=[
            pl.BlockSpec((1, tq, LANES), lambda b, i, g: (b, i, QC + g)),
            pl.BlockSpec((1, seq, LANES), lambda b, i, g: (b, 0, KC)),
            pl.BlockSpec((1, seq, LANES), lambda b, i, g: (b, 0, VC)),
        ],
        out_specs=pl.BlockSpec((1, tq, LANES), lambda b, i, g: (b, i, g)),
        out_shape=jax.ShapeDtypeStruct((batch, seq, C_WIDTH), jnp.bfloat16),
        compiler_params=pltpu.CompilerParams(dimension_semantics=("arbitrary",) * 3, vmem_limit_bytes=VMEM_LIMIT),
        name="mixer_c",
    )(view, view, view)


def _rms(x, g):
    return x * lax.rsqrt(jnp.mean(x * x, axis=-1, keepdims=True) + EPS) * g


def _outproj_kernel(oa_ref, ob_ref, oc_ref, x_ref, w_ref, ga_ref, gb_ref, gc_ref, gf_ref, wr_ref, xo_ref, h_ref,
                    aff_ref):
    f32 = jnp.float32
    merged = jnp.concatenate([_rms(oa_ref[...].astype(f32), ga_ref[...]), _rms(ob_ref[...].astype(f32), gb_ref[...]),
                              _rms(oc_ref[...].astype(f32), gc_ref[...])], axis=-1).astype(jnp.bfloat16)
    xn = x_ref[...] + jnp.dot(merged, w_ref[...], preferred_element_type=f32)
    xo_ref[...] = xn
    h = _rms(xn, gf_ref[...])
    h_ref[:, :D_MODEL] = h
    h_ref[:, D_MODEL:] = jnp.zeros((h.shape[0], LANES), f32)
    logits = lax.dot_general(wr_ref[...], h, (((1,), (1,)), ((), ())), preferred_element_type=f32,
                             precision=lax.Precision.HIGHEST)
    z = jnp.exp(logits - jnp.max(logits, axis=0, keepdims=True))
    aff = z / jnp.sum(z, axis=0, keepdims=True)
    for c in range(aff.shape[1] // LANES):
        aff_ref[c] = aff[:, c * LANES:(c + 1) * LANES]


def _out_projection(oa, ob, oc, x, w, ga, gb, gc, gf, wr_t, tm):
    n = x.shape[0]
    rows = lambda width: pl.BlockSpec((tm, width), lambda i: (i, 0))
    const = lambda shape: pl.BlockSpec(shape, lambda i: (0, 0))
    return pl.pallas_call(
        _outproj_kernel,
        grid=(n // tm,),
        in_specs=[rows(A_WIDTH), rows(B_WIDTH), rows(C_WIDTH),
                  rows(D_MODEL), const((D_MODEL, D_MODEL)), const((1, A_WIDTH)), const((1, B_WIDTH)),
                  const((1, C_WIDTH)), const((1, D_MODEL)), const((N_EXPERTS, D_MODEL))],
        out_specs=[rows(D_MODEL), rows(ROW_WIDTH), pl.BlockSpec((tm // LANES, N_EXPERTS, LANES), lambda i: (i, 0, 0))],
        out_shape=[jax.ShapeDtypeStruct((n, D_MODEL), jnp.float32), jax.ShapeDtypeStruct((n, ROW_WIDTH), jnp.float32),
                   jax.ShapeDtypeStruct((n // LANES, N_EXPERTS, LANES), jnp.float32)],
        compiler_params=pltpu.CompilerParams(dimension_semantics=("arbitrary",), vmem_limit_bytes=VMEM_LIMIT),
        name="out_projection",
    )(oa, ob, oc, x, w, ga, gb, gc, gf, wr_t)


def _ffn_kernel(x_ref, wg_ref, wu_ref, wd_ref, o_ref, dest_ref, xb_scr, gate_scr):
    e = pl.program_id(0)
    j = pl.program_id(2)

    @pl.when(j == 0)
    def _():
        xb_scr[...] = x_ref[0, :, :D_MODEL].astype(jnp.bfloat16)
        route = x_ref[0, :, D_MODEL:]
        lane = lax.broadcasted_iota(jnp.int32, route.shape, 1)
        pick = lambda k: jnp.sum(jnp.where(lane == k + e, route, 0.0), axis=-1, keepdims=True)
        gate_scr[...] = pick(GATE_LANE)
        dest_ref[0] = pick(DEST_LANE).astype(jnp.int32)
        o_ref[0] = jnp.zeros(o_ref.shape[1:], o_ref.dtype)

    x = xb_scr[...]
    hg = jnp.dot(x, wg_ref[...].astype(jnp.bfloat16), preferred_element_type=jnp.float32)
    hu = jnp.dot(x, wu_ref[...].astype(jnp.bfloat16), preferred_element_type=jnp.float32)
    act = (hg * jax.nn.sigmoid(hg) * hu).astype(jnp.bfloat16)
    o_ref[0] += jnp.dot(act, wd_ref[...].astype(jnp.bfloat16), preferred_element_type=jnp.float32)

    @pl.when(j == pl.num_programs(2) - 1)
    def _():
        o_ref[0] = o_ref[0] * gate_scr[...]


def _expert_ffn(xe, w_gate, w_up, w_down, layer, tr, tf):
    n_e, rows, _ = xe.shape
    d = D_MODEL
    d_ff = w_gate.shape[-1]
    return pl.pallas_call(
        _ffn_kernel,
        grid=(n_e, rows // tr, d_ff // tf),
        in_specs=[
            pl.BlockSpec((1, tr, ROW_WIDTH), lambda e, c, j: (e, c, 0)),
            pl.BlockSpec((None, None, d, tf), lambda e, c, j: (layer, e, 0, j)),
            pl.BlockSpec((None, None, d, tf), lambda e, c, j: (layer, e, 0, j)),
            pl.BlockSpec((None, None, tf, d), lambda e, c, j: (layer, e, j, 0)),
        ],
        out_specs=[pl.BlockSpec((1, tr, d), lambda e, c, j: (e, c, 0)),
                   pl.BlockSpec((1, tr, 1), lambda e, c, j: (e, c, 0))],
        out_shape=[jax.ShapeDtypeStruct((n_e, rows, d), jnp.float32),
                   jax.ShapeDtypeStruct((n_e, rows, 1), jnp.int32)],
        scratch_shapes=[pltpu.VMEM((tr, d), jnp.bfloat16), pltpu.VMEM((tr, 1), jnp.float32)],
        compiler_params=pltpu.CompilerParams(dimension_semantics=("arbitrary",) * 3, vmem_limit_bytes=VMEM_LIMIT),
        name="expert_ffn",
    )(xe, w_gate, w_up, w_down)


def _lane_cumsum(m, tri):
    nc, r, _ = m.shape
    flat = m.reshape(nc * r, LANES).astype(jnp.bfloat16)
    return jnp.dot(flat, tri, preferred_element_type=jnp.float32).reshape(nc, r, LANES)


def _lead_cumsum_exclusive(t):
    n = t.shape[0]
    inc, k = t, 1
    while k < n:
        inc = inc + jnp.concatenate([jnp.zeros((k,) + t.shape[1:], t.dtype), inc[:n - k]], axis=0)
        k *= 2
    return inc - t


def _token_cumsum(m, tri):
    inside = _lane_cumsum(m, tri)
    total = inside[:, :, LANES - 1:]
    return _lead_cumsum_exclusive(total), inside, total


def _select_kernel(aff_ref, tri_ref, idx_ref, stats_ref, split_scr, before_scr, through_scr, *, groups):
    f32 = jnp.float32
    tri = tri_ref[...]
    for c0, nc, cap, s0 in groups:
        aff = aff_ref[c0:c0 + nc]
        bits = pltpu.bitcast(aff, jnp.int32)
        count = lambda mask: jnp.sum(jnp.sum(mask, axis=0, keepdims=True), axis=2, keepdims=True)

        def bisect(_, carry):
            lo, hi = carry
            mid = lo + ((hi - lo) >> 1)
            ok = count(jnp.where(bits >= mid, 1.0, 0.0)) >= cap
            return jnp.where(ok, mid, lo), jnp.where(ok, hi, mid)

        shape = (1, N_EXPERTS, 1)
        thr, _ = lax.fori_loop(0, BISECT_STEPS, bisect,
                               (jnp.zeros(shape, jnp.int32), jnp.full(shape, ONE_BITS + 1, jnp.int32)))
        above, tie = bits > thr, bits == thr
        tie_f = jnp.where(tie, 1.0, 0.0)
        need = cap - count(jnp.where(above, 1.0, 0.0))
        before, inside, _ = _token_cumsum(tie_f, tri)
        chosen = jnp.where(above | (tie & (before + inside - tie_f < need)), 1.0, 0.0)

        before, inside, total = _token_cumsum(chosen, tri)
        through = before + inside
        mult = jnp.sum(chosen, axis=1, keepdims=True)
        m_before, m_inside, _ = _token_cumsum(mult, tri)
        stats_ref[c0:c0 + nc, 0:N_EXPERTS, :] = chosen
        stats_ref[c0:c0 + nc, N_EXPERTS:N_EXPERTS + 1, :] = m_before + m_inside - mult
        stats_ref[c0:c0 + nc, N_EXPERTS + 1:N_EXPERTS + 2, :] = mult
        stats_ref[c0:c0 + nc, N_EXPERTS + 2:, :] = jnp.zeros((nc, STATS_ROWS - N_EXPERTS - 2, LANES), f32)

        for e in range(N_EXPERTS):
            t_e = through[:, e, :]
            hi_digit = jnp.floor(t_e * (1.0 / COUNT_DIGIT))
            split_scr[e, 0:nc, 0:LANES] = hi_digit.astype(jnp.bfloat16)
            split_scr[e, 0:nc, LANES:] = (t_e - COUNT_DIGIT * hi_digit).astype(jnp.bfloat16)
            before_scr[e, 0:nc, :] = jnp.broadcast_to(before[:, e, :], (nc, LANES))
            through_scr[e, 0:nc, :] = jnp.broadcast_to((before + total)[:, e, :], (nc, LANES))

        chunk_id = lax.broadcasted_iota(jnp.int32, (1, nc), 1).astype(f32)
        lane_id = lax.broadcasted_iota(jnp.int32, (1, LANES), 1)
        row_id = lax.broadcasted_iota(jnp.int32, (LANES, 1), 0)

        def compact(it, carry):
            e, s = it // (cap // LANES), it % (cap // LANES)
            slot_row = (s * LANES + lane_id).astype(f32)
            slot_col = (s * LANES + row_id).astype(f32)
            holds = (before_scr[e, 0:nc, :] <= slot_row) & (slot_row < through_scr[e, 0:nc, :])
            onehot = jnp.where(holds, 1.0, 0.0).T
            digits = jnp.dot(onehot.astype(jnp.bfloat16), split_scr[e, 0:nc, :], preferred_element_type=f32)
            counts = COUNT_DIGIT * digits[:, :LANES] + digits[:, LANES:]
            inside_pos = jnp.sum(jnp.where(counts <= slot_col, 1.0, 0.0), axis=-1, keepdims=True)
            chunk = jnp.sum(onehot * chunk_id, axis=-1, keepdims=True)
            token = (c0 + chunk) * LANES + inside_pos
            idx_ref[e, pl.ds(s0 + s, 1), :] = jnp.broadcast_to(token, (LANES, LANES)).T[0:1, :].astype(jnp.int32)
            return carry

        lax.fori_loop(0, N_EXPERTS * (cap // LANES), compact, 0, unroll=4)


def _select(aff, groups, slots):
    n_chunks = aff.shape[0]
    nc_max = max(nc for _, nc, _, _ in groups)
    assert all(cap <= 2 * LANES * COUNT_DIGIT for _, _, cap, _ in groups)
    tri = jnp.asarray(np.triu(np.ones((LANES, LANES))), jnp.bfloat16)
    return pl.pallas_call(
        functools.partial(_select_kernel, groups=groups),
        out_shape=[jax.ShapeDtypeStruct((N_EXPERTS, slots // LANES, LANES), jnp.int32),
                   jax.ShapeDtypeStruct((n_chunks, STATS_ROWS, LANES), jnp.float32)],
        scratch_shapes=[pltpu.VMEM((N_EXPERTS, nc_max, 2 * LANES), jnp.bfloat16),
                        pltpu.VMEM((N_EXPERTS, nc_max, LANES), jnp.float32),
                        pltpu.VMEM((N_EXPERTS, nc_max, LANES), jnp.float32)],
        compiler_params=pltpu.CompilerParams(vmem_limit_bytes=VMEM_LIMIT),
        name="expert_select",
    )(aff, tri)


def _route_rows_kernel(aff_ref, stats_ref, below_ref, rows_in_ref, o_ref):
    del rows_in_ref
    n = aff_ref.shape[0]
    pad = jnp.zeros((LANES - N_EXPERTS - stats_ref.shape[1], LANES), jnp.float32)
    lane = lax.broadcasted_iota(jnp.int32, (LANES, LANES), 1)
    for c in range(n):
        t = jnp.concatenate([aff_ref[c], stats_ref[c], pad], axis=0).T
        rank = jnp.dot(t.astype(jnp.bfloat16), below_ref[...], preferred_element_type=jnp.float32)
        first = t[:, OFF_LANE:OFF_LANE + 1]
        o_ref[c * LANES:(c + 1) * LANES, :] = jnp.where((lane >= DEST_LANE) & (lane < OFF_LANE), first + rank, t)


def _route_rows(aff, stats, rows, tm):
    n = rows.shape[0]
    k = tm // LANES
    below = np.zeros((LANES, LANES))
    below[DEST_LANE:OFF_LANE, DEST_LANE:OFF_LANE] = np.triu(np.ones((N_EXPERTS, N_EXPERTS)), 1)
    return pl.pallas_call(
        _route_rows_kernel,
        grid=(n // tm,),
        in_specs=[pl.BlockSpec((k, N_EXPERTS, LANES), lambda i: (i, 0, 0)),
                  pl.BlockSpec((k, stats.shape[1], LANES), lambda i: (i, 0, 0)),
                  pl.BlockSpec((LANES, LANES), lambda i: (0, 0)),
                  pl.BlockSpec(memory_space=pl.ANY)],
        out_specs=pl.BlockSpec((tm, LANES), lambda i: (i, D_MODEL // LANES)),
        out_shape=jax.ShapeDtypeStruct(rows.shape, rows.dtype),
        input_output_aliases={3: 0},
        compiler_params=pltpu.CompilerParams(dimension_semantics=("arbitrary",)),
        name="route_rows",
    )(aff, stats, jnp.asarray(below, jnp.bfloat16), rows)


SC_ROWS = 32


def _sc_mesh():
    return plsc.VectorSubcoreMesh(core_axis_name="core", subcore_axis_name="subcore")


def _sc_gather(table, idx):
    m, w = idx.shape[0], table.shape[1]
    per = m // (SC_WORKERS * SC_ROWS)
    assert per * SC_WORKERS * SC_ROWS == m

    @functools.partial(pl.kernel, out_type=jax.ShapeDtypeStruct((m, w), table.dtype), mesh=_sc_mesh(),
                       scratch_types=[pltpu.VMEM((1, SC_ROWS), jnp.int32), pltpu.VMEM((SC_ROWS, w), table.dtype)])
    def gather(table_hbm, idx_hbm, out_hbm, idx_v, buf):
        worker = lax.axis_index("core") * (SC_WORKERS // 2) + lax.axis_index("subcore")

        @pl.loop(0, per)
        def _(b):
            blk = worker * per + b
            pltpu.sync_copy(idx_hbm.at[pl.ds(blk, 1)], idx_v)
            pltpu.sync_copy(table_hbm.at[idx_v.at[0]], buf)
            pltpu.sync_copy(buf, out_hbm.at[pl.ds(blk * SC_ROWS, SC_ROWS)])

    return gather(table, idx.reshape(m // SC_ROWS, SC_ROWS))


def _sc_scatter(rows, dest):
    m, w = rows.shape
    per = m // (SC_WORKERS * SC_ROWS)
    assert per * SC_WORKERS * SC_ROWS == m

    @functools.partial(pl.kernel, out_type=jax.ShapeDtypeStruct((m, w), rows.dtype), mesh=_sc_mesh(),
                       scratch_types=[pltpu.VMEM((1, SC_ROWS), jnp.int32), pltpu.VMEM((SC_ROWS, w), rows.dtype)])
    def scatter(rows_hbm, dest_hbm, out_hbm, dest_v, buf):
        worker = lax.axis_index("core") * (SC_WORKERS // 2) + lax.axis_index("subcore")

        @pl.loop(0, per)
        def _(b):
            blk = worker * per + b
            pltpu.sync_copy(dest_hbm.at[pl.ds(blk, 1)], dest_v)
            pltpu.sync_copy(rows_hbm.at[pl.ds(blk * SC_ROWS, SC_ROWS)], buf)
            pltpu.sync_copy(buf, out_hbm.at[dest_v.at[0]])

    return scatter(rows, dest.reshape(m // SC_ROWS, SC_ROWS))


COMBINE_ROWS = 512


def _combine_kernel(tile_ref, start_ref, want_ref, flags_ref, x_ref, route_ref, z_ref, o_ref):
    s = pl.program_id(0)
    f32, bf16 = jnp.float32, jnp.bfloat16

    @pl.when(flags_ref[s] == 1)
    def _():
        o_ref[...] = x_ref[...]

    @pl.when(flags_ref[s] != 2)
    def _():
        first = route_ref[:, OFF_LANE:OFF_LANE + 1]
        last = first + route_ref[:, MULT_LANE:MULT_LANE + 1]
        row = start_ref[s] * SUBLANES + lax.broadcasted_iota(jnp.int32, (1, COMBINE_ROWS), 1)
        rowf = row.astype(f32)
        own = jnp.where((first <= rowf) & (rowf < last) & (row >= want_ref[s]), 1.0, 0.0).astype(bf16)
        z = z_ref[...]
        z1 = z.astype(bf16)
        r1 = z - z1.astype(f32)
        z2 = r1.astype(bf16)
        z3 = (r1 - z2.astype(f32)).astype(bf16)
        o_ref[...] += (jnp.dot(own, z1, preferred_element_type=f32) + jnp.dot(own, z2, preferred_element_type=f32)
                       + jnp.dot(own, z3, preferred_element_type=f32))


def _combine_steps(tile_lo, n_rows):
    n_tiles = tile_lo.shape[0] - 1
    n_steps_max = n_rows // COMBINE_ROWS + 2 * n_tiles + n_tiles // 16 + 1
    lo = (tile_lo[:-1] // SUBLANES) * SUBLANES
    per_tile = jnp.maximum((tile_lo[1:] - lo + COMBINE_ROWS - 1) // COMBINE_ROWS, 1)
    ends = jnp.cumsum(per_tile)
    s = jnp.arange(n_steps_max, dtype=jnp.int32)
    valid = s < ends[-1]
    tile = jnp.minimum(jnp.sum(ends[None, :] <= s[:, None], axis=1).astype(jnp.int32), n_tiles - 1)
    mine = tile[:, None] == jnp.arange(n_tiles, dtype=jnp.int32)[None, :]
    of_tile = lambda a: jnp.sum(jnp.where(mine, a[None, :], 0), axis=1)
    k = s - of_tile(ends - per_tile)
    want = of_tile(lo) + k * COMBINE_ROWS
    start = jnp.minimum(want, n_rows - COMBINE_ROWS)
    last_start = jnp.sum(jnp.where(s == ends[-1] - 1, start, 0))
    start = jnp.where(valid, start, last_start)
    flags = jnp.where(valid, (k == 0).astype(jnp.int32), 2)
    return tile, start // SUBLANES, want, flags, n_steps_max


def _combine(x, rows, z, tile_lo, tt):
    tile, start, want, flags, n_steps = _combine_steps(tile_lo, z.shape[0])
    return pl.pallas_call(
        _combine_kernel,
        grid_spec=pltpu.PrefetchScalarGridSpec(
            num_scalar_prefetch=4,
            grid=(n_steps,),
            in_specs=[pl.BlockSpec((tt, D_MODEL), lambda s, tile, *_: (tile[s], 0)),
                      pl.BlockSpec((tt, LANES), lambda s, tile, *_: (tile[s], D_MODEL // LANES)),
                      pl.BlockSpec((pl.Element(COMBINE_ROWS), pl.Element(D_MODEL)),
                                   lambda s, tile, start, *_: (start[s] * SUBLANES, 0))],
            out_specs=pl.BlockSpec((tt, D_MODEL), lambda s, tile, *_: (tile[s], 0)),
        ),
        out_shape=jax.ShapeDtypeStruct(x.shape, jnp.float32),
        compiler_params=pltpu.CompilerParams(dimension_semantics=("arbitrary",), vmem_limit_bytes=VMEM_LIMIT),
        name="expert_combine",
    )(tile, start, want, flags, x, rows, z)


def _final_norm_kernel(x_ref, g_ref, o_ref):
    o_ref[...] = _rms(x_ref[...], g_ref[...])


def _final_norm(x, g, tm):
    n = x.shape[0]
    return pl.pallas_call(
        _final_norm_kernel,
        grid=(n // tm,),
        in_specs=[pl.BlockSpec((tm, D_MODEL), lambda i: (i, 0)), pl.BlockSpec((1, D_MODEL), lambda i: (0, 0))],
        out_specs=pl.BlockSpec((tm, D_MODEL), lambda i: (i, 0)),
        out_shape=jax.ShapeDtypeStruct((n, D_MODEL), jnp.float32),
        name="final_norm",
    )(x, g)


def _forward(x_prompt, x_sample, g_attn, w_in, g_q_c, g_k_c, sink_b, g_out_a, g_out_b, g_out_c, w_out, g_ffn, w_router,
             w_gate, w_up, w_down, g_final, *, tm, tr, tf):
    bf16 = jnp.bfloat16
    shapes = (x_prompt.shape[:2], x_sample.shape[:2])
    assert all(l % tm == 0 and (b * l) % LANES == 0 for b, l in shapes)
    xs = [x_prompt.reshape(-1, D_MODEL), x_sample.reshape(-1, D_MODEL)]

    tables = _rope_tables(max(l for _, l in shapes))
    seg = jnp.asarray(np.kron(np.eye(LANES // HEAD_DIM), np.ones((HEAD_DIM, HEAD_DIM))), bf16)
    pb, pc, perm_out = _out_perms()
    w_in_p = w_in[:, :, _in_perm()].astype(bf16)
    w_out_p = w_out[:, perm_out, :].astype(bf16)
    sink_p = sink_b[:, np.asarray(B_Q_ORDER)]
    tile2 = lambda g: jnp.tile(g, (1, 2))[:, None, :]
    gq, gk = tile2(g_q_c), tile2(g_k_c)
    wr_t = jnp.swapaxes(w_router, 1, 2)
    tt = 256

    def layer(x, l, batch, seq):
        n = batch * seq
        cap = CAPACITY_FACTOR * n // N_EXPERTS
        assert cap % LANES == 0
        proj, a4, a16 = _in_projection(x, g_attn[l][None], w_in_p[l], tables, gq[l], gk[l], seg, seq, tm)
        oa = _mixer_a(proj, a4, a16, batch=batch, seq=seq).reshape(n, A_WIDTH)
        ob = _mixer_b(proj, sink_p[l], batch=batch, seq=seq).reshape(n, B_WIDTH)
        oc = _mixer_c(proj, batch=batch, seq=seq).reshape(n, C_WIDTH)
        x, rows, aff = _out_projection(oa, ob, oc, x, w_out_p[l], g_out_a[l][None], g_out_b[l][pb][None],
                                       g_out_c[l][pc][None], g_ffn[l][None], wr_t[l], tm)
        idx, stats = _select(aff, ((0, n // LANES, cap, 0),), cap)
        rows = _route_rows(aff, stats, rows, tm)
        xe = _sc_gather(rows, idx.reshape(-1)).reshape(N_EXPERTS, cap, ROW_WIDTH)
        ye, dest = _expert_ffn(xe, w_gate, w_up, w_down, l, min(tr, cap), tf)
        z = _sc_scatter(ye.reshape(-1, D_MODEL), dest.reshape(-1))
        first_slot = stats[::tt // LANES, N_EXPERTS, 0].astype(jnp.int32)
        tile_lo = jnp.concatenate([first_slot, jnp.full((1,), N_EXPERTS * cap, jnp.int32)])
        return _combine(x, rows, z, tile_lo, tt)

    for l in range(DEPTH):
        xs = [layer(x, l, b, s) for x, (b, s) in zip(xs, shapes)]
    return tuple(_final_norm(x, g_final[None], tm).reshape(b, s, D_MODEL) for x, (b, s) in zip(xs, shapes))


def kernel(x_prompt, x_sample, g_attn, w_in, g_q_c, g_k_c, sink_b, g_out_a, g_out_b, g_out_c, w_out, g_ffn, w_router,
           w_gate, w_up, w_down, g_final):
    return _forward(x_prompt, x_sample, g_attn, w_in, g_q_c, g_k_c, sink_b, g_out_a, g_out_b, g_out_c, w_out, g_ffn,
                    w_router, w_gate, w_up, w_down, g_final, tm=1024, tr=2048, tf=256)
```

```python
import functools
import math

import jax
import jax.numpy as jnp
import numpy as np
from jax import lax
from jax.experimental import pallas as pl
from jax.experimental.pallas import tpu as pltpu
from jax.experimental.pallas import tpu_sc as plsc

D_MODEL = 1024
DEPTH = 4
HEAD_DIM = 64
A_HEADS = 6
A_PAIRS = ((128, 1), (512, 4), (2048, 16))
B_HEADS = 4
B_KV_HEADS = 2
B_HALF_WINDOW = 128
C_HEADS = 6
C_KV_HEADS = 2
GRID_W = 64
ROPE_THETA = 10000.0
N_EXPERTS = 16
CAPACITY_FACTOR = 2
D_FF = 2816
EPS = 1e-6
NEG_INF = -1e30

LANES = 128
SUBLANES = 8
A_WIDTH = A_HEADS * HEAD_DIM
B_WIDTH = B_HEADS * HEAD_DIM
C_WIDTH = C_HEADS * HEAD_DIM
IN_WIDTH = 3 * A_WIDTH + B_WIDTH + 2 * B_KV_HEADS * HEAD_DIM + C_WIDTH + 2 * C_KV_HEADS * HEAD_DIM
N_GROUPS = IN_WIDTH // LANES
A_GROUPS = 3 * A_WIDTH // LANES
QA, KA, VA, QB, KB, VB, QC, KC, VC = 0, 3, 6, 9, 11, 12, 13, 16, 17
ROPE_NONE, ROPE_1D, ROPE_AXIAL_Q, ROPE_AXIAL_K = 0, 1, 2, 3
GROUP_KIND = ([(ROPE_1D, True)] * 3 + [(ROPE_1D, False)] * 3 + [(ROPE_NONE, False)] * 3
              + [(ROPE_1D, True)] * 2 + [(ROPE_1D, False)] + [(ROPE_NONE, False)]
              + [(ROPE_AXIAL_Q, True)] * 3 + [(ROPE_AXIAL_K, False)] + [(ROPE_NONE, False)])
Q_SCALE = HEAD_DIM ** -0.5
A_DILATIONS = tuple(d for _, d in A_PAIRS)
A_HALF_WINDOW = A_PAIRS[0][0] // 2
assert all(w // 2 // d == A_HALF_WINDOW for w, d in A_PAIRS) and A_DILATIONS == (1, 4, 16)
BAND_TQ = 128
ROW_WIDTH = D_MODEL + LANES
GATE_LANE, DEST_LANE, OFF_LANE, MULT_LANE = 0, N_EXPERTS, 2 * N_EXPERTS, 2 * N_EXPERTS + 1
ONE_BITS = 0x3F800000
BISECT_STEPS = 31
STATS_ROWS = 24
COUNT_DIGIT = 64
SC_WORKERS = 32

VMEM_LIMIT = 56 * 1024 * 1024

B_Q_ORDER = (0, 2, 1, 3)
C_Q_ORDER = (0, 3, 1, 4, 2, 5)


def _head_perm(order):
    return np.concatenate([np.arange(h * HEAD_DIM, (h + 1) * HEAD_DIM) for h in order])


def _in_perm():
    widths = [A_WIDTH] * 3 + [B_WIDTH, 128, 128, C_WIDTH, 128, 128]
    offs = np.concatenate([[0], np.cumsum(widths)])
    parts = [np.arange(offs[i], offs[i + 1]) for i in range(9)]
    parts[3] = offs[3] + _head_perm(B_Q_ORDER)
    parts[6] = offs[6] + _head_perm(C_Q_ORDER)
    return np.concatenate(parts)


def _out_perms():
    pb = _head_perm(B_Q_ORDER)
    pc = _head_perm(C_Q_ORDER)
    return pb, pc, np.concatenate([np.arange(A_WIDTH), A_WIDTH + pb, A_WIDTH + B_WIDTH + pc])


def _rope_tables(seq):
    pos = jnp.arange(seq, dtype=jnp.float32)
    inv1 = ROPE_THETA ** (-jnp.arange(0, HEAD_DIM, 2, dtype=jnp.float32) / HEAD_DIM)
    ang = pos[:, None] * inv1[None, :]
    c, s = jnp.cos(ang), jnp.sin(ang)
    cos1 = jnp.tile(jnp.concatenate([c, c], -1), (1, 2))
    sin1 = jnp.tile(jnp.concatenate([-s, s], -1), (1, 2))
    half = HEAD_DIM // 2
    inv2 = ROPE_THETA ** (-jnp.arange(0, half, 2, dtype=jnp.float32) / half)
    row = jnp.floor(pos / GRID_W)
    col = pos - row * GRID_W
    ar, ac = row[:, None] * inv2[None, :], col[:, None] * inv2[None, :]
    cr, sr, cc, sc = jnp.cos(ar), jnp.sin(ar), jnp.cos(ac), jnp.sin(ac)
    cos2 = jnp.tile(jnp.concatenate([cr, cr, cc, cc], -1), (1, 2))
    sin2 = jnp.tile(jnp.concatenate([-sr, sr, -sc, sc], -1), (1, 2))
    return cos1, sin1, cos2, sin2


def _swap_halves(x, block):
    half = block // 2
    lane = lax.broadcasted_iota(jnp.int32, x.shape, 1)
    return jnp.where(lane % block < half, pltpu.roll(x, LANES - half, 1), pltpu.roll(x, half, 1))


def _inproj_kernel(x_ref, g_ref, w_ref, cos1_ref, sin1_ref, cos2_ref, sin2_ref, gq_ref, gk_ref, seg_ref, o_ref,
                   a4_ref, a16_ref, rows_ref):
    x = x_ref[...]
    tm = x.shape[0]
    y = x * lax.rsqrt(jnp.mean(x * x, axis=-1, keepdims=True) + EPS)
    h = (y * g_ref[...]).astype(jnp.bfloat16)
    for c in range(N_GROUPS // 2):
        acc = jnp.dot(h, w_ref[:, c * 2 * LANES:(c + 1) * 2 * LANES], preferred_element_type=jnp.float32)
        for half in range(2):
            grp = 2 * c + half
            cols = slice(grp * LANES, (grp + 1) * LANES)
            a = acc[:, half * LANES:(half + 1) * LANES]
            kind, is_q = GROUP_KIND[grp]
            if kind == ROPE_1D:
                a = a * cos1_ref[...] + _swap_halves(a, HEAD_DIM) * sin1_ref[...]
            elif kind in (ROPE_AXIAL_Q, ROPE_AXIAL_K):
                gain = gq_ref[...] if kind == ROPE_AXIAL_Q else gk_ref[...]
                sq = a * a
                sq_hi = sq.astype(jnp.bfloat16)
                sq_lo = (sq - sq_hi.astype(jnp.float32)).astype(jnp.bfloat16)
                ss = (jnp.dot(sq_hi, seg_ref[...], preferred_element_type=jnp.float32)
                      + jnp.dot(sq_lo, seg_ref[...], preferred_element_type=jnp.float32))
                a = a * lax.rsqrt(ss * (1.0 / HEAD_DIM) + EPS) * gain
                a = a * cos2_ref[...] + _swap_halves(a, HEAD_DIM // 2) * sin2_ref[...]
            if is_q:
                a = a * Q_SCALE
            o_ref[:, cols] = a.astype(jnp.bfloat16)
            if grp < A_GROUPS:
                rows_ref[grp] = a
        if c == (A_GROUPS - 1) // 2:
            for d, ref in ((4, a4_ref), (16, a16_ref)):
                for r in range(d):
                    for grp in range(A_GROUPS):
                        ref[r, :, grp * LANES:(grp + 1) * LANES] = (
                            rows_ref[grp, pl.ds(r, tm // d, stride=d), :].astype(jnp.bfloat16))


def _in_projection(x, g, w, tables, gq, gk, seg, seq, tm):
    n = x.shape[0]
    tab_spec = pl.BlockSpec((tm, LANES), lambda i: (i % (seq // tm), 0))
    const = lambda shape: pl.BlockSpec(shape, lambda i: (0, 0))
    wa = A_GROUPS * LANES
    return pl.pallas_call(
        _inproj_kernel,
        grid=(n // tm,),
        in_specs=[pl.BlockSpec((tm, D_MODEL), lambda i: (i, 0)), const((1, D_MODEL)), const((D_MODEL, IN_WIDTH)),
                  tab_spec, tab_spec, tab_spec, tab_spec, const((1, LANES)), const((1, LANES)), const((LANES, LANES))],
        out_specs=[pl.BlockSpec((tm, IN_WIDTH), lambda i: (i, 0)),
                   pl.BlockSpec((4, tm // 4, wa), lambda i: (0, i, 0)),
                   pl.BlockSpec((16, tm // 16, wa), lambda i: (0, i, 0))],
        out_shape=[jax.ShapeDtypeStruct((n, IN_WIDTH), jnp.bfloat16),
                   jax.ShapeDtypeStruct((4, n // 4, wa), jnp.bfloat16),
                   jax.ShapeDtypeStruct((16, n // 16, wa), jnp.bfloat16)],
        scratch_shapes=[pltpu.VMEM((A_GROUPS, tm, LANES), jnp.float32)],
        compiler_params=pltpu.CompilerParams(dimension_semantics=("arbitrary",), vmem_limit_bytes=VMEM_LIMIT),
        name="in_projection",
    )(x, g, w, *tables, gq, gk, seg)


def _stack_heads(q):
    lane = lax.broadcasted_iota(jnp.int32, q.shape, 1)
    zero = jnp.zeros_like(q)
    return jnp.concatenate([jnp.where(lane < HEAD_DIM, q, zero), jnp.where(lane >= HEAD_DIM, q, zero)], axis=0)


def _unstack_heads(x, tq):
    lane = lax.broadcasted_iota(jnp.int32, (tq, x.shape[1]), 1)
    return jnp.where(lane < HEAD_DIM, x[:tq], x[tq:])


def _unstack_column(col, tq):
    return _unstack_heads(jnp.broadcast_to(col, (2 * tq, LANES)), tq)


def _band_bias(tq, win, half_window):
    row = np.arange(2 * tq)[:, None] % tq
    col = np.arange(win)[None, :]
    kinds = [np.where(np.abs(row + off - col) <= half_window, 0.0, NEG_INF) for off in (0, half_window, 2 * half_window)]
    return jnp.asarray(np.stack(kinds), jnp.float32)


def _band_tile(q, kw, vw, bias, sink=None):
    s = lax.dot_general(_stack_heads(q), kw, (((1,), (1,)), ((), ())), preferred_element_type=jnp.float32) + bias
    m = jnp.max(s, axis=-1, keepdims=True)
    if sink is not None:
        m = jnp.maximum(m, sink)
    p = jnp.exp(s - m)
    den = jnp.sum(p, axis=-1, keepdims=True)
    if sink is not None:
        den = den + jnp.exp(sink - m)
    num = jnp.dot(p.astype(jnp.bfloat16), vw, preferred_element_type=jnp.float32)
    return num, m, den


def _tile_window(i, n_tiles, tq, win, half_window, seq):
    start = pl.multiple_of(jnp.clip(i * tq - half_window, 0, seq - win), math.gcd(tq, half_window, seq - win))
    kind = jnp.where(i == 0, 0, jnp.where(i == n_tiles - 1, 2, 1))
    return start, kind


def _mixer_a_kernel(bias1_ref, bias4_ref, bias16_ref, q1_ref, k1_ref, v1_ref, q4_ref, k4_ref, v4_ref, q16_ref, k16_ref, v16_ref,
                    o_ref, m_scr, l_scr, n_scr, *, seq):
    tq, hw = BAND_TQ, A_HALF_WINDOW

    def run_tile(q_ref, k_ref, v_ref, b_ref, lead, i, ls):
        tqc = min(tq, ls)
        win = min(tqc + 2 * hw, ls)
        n_tiles = ls // tqc
        start, kind = _tile_window(i, n_tiles, tqc, win, hw, ls)
        num, m, den = _band_tile(q_ref[lead, pl.ds(i * tqc, tqc), :], k_ref[lead, pl.ds(start, win), :],
                                 v_ref[lead, pl.ds(start, win), :], b_ref[kind])
        return _unstack_heads(num, tqc), _unstack_column(m, tqc), _unstack_column(den, tqc), tqc

    def tile1(i, carry):
        num, m, den, _ = run_tile(q1_ref, k1_ref, v1_ref, bias1_ref, 0, i, seq)
        rows = pl.ds(pl.multiple_of(i * tq, tq), tq)
        m_scr[rows, :] = m
        l_scr[rows, :] = den
        n_scr[rows, :] = num
        return carry

    lax.fori_loop(0, seq // tq, tile1, 0, unroll=16)

    def merge(tiles):
        old = [(m_scr[rows, :], l_scr[rows, :], n_scr[rows, :]) for rows, _, _, _ in tiles]
        for (rows, num, m, den), (m_old, l_old, n_old) in zip(tiles, old):
            m_new = jnp.maximum(m_old, m)
            a, b = jnp.exp(m_old - m_new), jnp.exp(m - m_new)
            m_scr[rows, :] = m_new
            l_scr[rows, :] = a * l_old + b * den
            n_scr[rows, :] = a * n_old + b * num

    ls4 = seq // 4

    def tile4(i, carry):
        tiles = []
        for r in range(4):
            num, m, den, tqc = run_tile(q4_ref, k4_ref, v4_ref, bias4_ref, r, i, ls4)
            tiles.append((pl.ds(i * (tqc * 4) + r, tqc, stride=4), num, m, den))
        merge(tiles)
        return carry

    lax.fori_loop(0, ls4 // min(tq, ls4), tile4, 0, unroll=4)

    ls16 = seq // 16

    def class16(r2, carry):
        tiles = []
        for r in (2 * r2, 2 * r2 + 1):
            for i in range(ls16 // min(tq, ls16)):
                num, m, den, tqc = run_tile(q16_ref, k16_ref, v16_ref, bias16_ref, r, i, ls16)
                tiles.append((pl.ds(i * (tqc * 16) + r, tqc, stride=16), num, m, den))
        merge(tiles)
        return carry

    lax.fori_loop(0, 8, class16, 0, unroll=2)
    o_ref[0] = (n_scr[...] * (1.0 / l_scr[...])).astype(o_ref.dtype)


def _mixer_a(proj, a4, a16, *, batch, seq):
    n = proj.shape[0]
    hw = A_HALF_WINDOW

    def class_bias(ls):
        tq = min(BAND_TQ, ls)
        return _band_bias(tq, min(tq + 2 * hw, ls), hw)

    biases = [class_bias(seq // d) for d in A_DILATIONS]
    view = proj.reshape(n // seq, seq, IN_WIDTH)
    nat = lambda off: pl.BlockSpec((1, seq, LANES), lambda b, g: (b, 0, off + g))
    cls = lambda d, off: pl.BlockSpec((d, seq // d, LANES), lambda b, g: (0, b, off + g))
    full = lambda a: pl.BlockSpec(a.shape, lambda b, g: (0, 0, 0))
    return pl.pallas_call(
        functools.partial(_mixer_a_kernel, seq=seq),
        grid=(batch, A_WIDTH // LANES),
        in_specs=[full(biases[0]), full(biases[1]), full(biases[2]), nat(QA), nat(KA), nat(VA), cls(4, QA), cls(4, KA), cls(4, VA),
                  cls(16, QA), cls(16, KA), cls(16, VA)],
        out_specs=pl.BlockSpec((1, seq, LANES), lambda b, g: (b, 0, g)),
        out_shape=jax.ShapeDtypeStruct((batch, seq, A_WIDTH), jnp.bfloat16),
        scratch_shapes=[pltpu.VMEM((seq, LANES), jnp.float32)] * 3,
        compiler_params=pltpu.CompilerParams(dimension_semantics=("arbitrary",) * 2, vmem_limit_bytes=VMEM_LIMIT),
        name="mixer_a",
    )(*biases, view, view, view, a4, a4, a4, a16, a16, a16)


def _mixer_b_kernel(sink_ref, bias_ref, q_ref, k_ref, v_ref, o_ref, *, seq):
    tq, hw = BAND_TQ, B_HALF_WINDOW
    win = tq + 2 * hw
    n_tiles = seq // tq
    g = pl.program_id(1)
    row = lax.broadcasted_iota(jnp.int32, (2 * tq, 1), 0)
    sink = jnp.where(row < tq, sink_ref[g], sink_ref[g + B_KV_HEADS])

    def tile(i, carry):
        start, kind = _tile_window(i, n_tiles, tq, win, hw, seq)
        rows = pl.ds(pl.multiple_of(i * tq, tq), tq)
        num, _, den = _band_tile(q_ref[0, rows, :], k_ref[0, pl.ds(start, win), :], v_ref[0, pl.ds(start, win), :],
                                 bias_ref[kind], sink)
        o_ref[0, rows, :] = _unstack_heads(num * (1.0 / den), tq).astype(o_ref.dtype)
        return carry

    lax.fori_loop(0, n_tiles, tile, 0, unroll=8)


def _mixer_b(proj, sink, *, batch, seq):
    n = proj.shape[0]
    bias = _band_bias(BAND_TQ, BAND_TQ + 2 * B_HALF_WINDOW, B_HALF_WINDOW)
    view = proj.reshape(n // seq, seq, IN_WIDTH)
    return pl.pallas_call(
        functools.partial(_mixer_b_kernel, seq=seq),
        grid=(batch, B_WIDTH // LANES),
        in_specs=[pl.BlockSpec(memory_space=pltpu.SMEM), pl.BlockSpec(bias.shape, lambda b, g: (0, 0, 0)),
                  pl.BlockSpec((1, seq, LANES), lambda b, g: (b, 0, QB + g)),
                  pl.BlockSpec((1, seq, LANES), lambda b, g: (b, 0, KB)),
                  pl.BlockSpec((1, seq, LANES), lambda b, g: (b, 0, VB))],
        out_specs=pl.BlockSpec((1, seq, LANES), lambda b, g: (b, 0, g)),
        out_shape=jax.ShapeDtypeStruct((batch, seq, B_WIDTH), jnp.bfloat16),
        compiler_params=pltpu.CompilerParams(dimension_semantics=("arbitrary",) * 2, vmem_limit_bytes=VMEM_LIMIT),
        name="mixer_b",
    )(sink, bias, view, view, view)


def _mixer_c_kernel(q_ref, k_ref, v_ref, o_ref, *, tq, chunk):
    lhs = _stack_heads(q_ref[0])
    seq = k_ref.shape[1]
    m = den = acc = None
    for c in range(seq // chunk):
        keys = slice(c * chunk, (c + 1) * chunk)
        s = lax.dot_general(lhs, k_ref[0, keys, :], (((1,), (1,)), ((), ())), preferred_element_type=jnp.float32)
        m_c = jnp.max(s, axis=-1, keepdims=True)
        m_new = m_c if m is None else jnp.maximum(m, m_c)
        p = jnp.exp(s - m_new)
        den_c = jnp.sum(p, axis=-1, keepdims=True)
        acc_c = jnp.dot(p.astype(jnp.bfloat16), v_ref[0, keys, :], preferred_element_type=jnp.float32)
        if m is None:
            den, acc = den_c, acc_c
        else:
            alpha = jnp.exp(m - m_new)
            den, acc = alpha * den + den_c, alpha * acc + acc_c
        m = m_new
    o_ref[0] = _unstack_heads(acc * (1.0 / den), tq).astype(o_ref.dtype)


def _mixer_c(proj, *, batch, seq, tq=512, chunk=1024):
    n = proj.shape[0]
    tq, chunk = min(tq, seq), min(chunk, seq)
    view = proj.reshape(n // seq, seq, IN_WIDTH)
    return pl.pallas_call(
        functools.partial(_mixer_c_kernel, tq=tq, chunk=chunk),
        grid=(batch, seq // tq, C_WIDTH // LANES),
        in_specs=[
            pl.BlockSpec((1, tq, LANES), lambda b, i, g: (b, i, QC + g)),
            pl.BlockSpec((1, seq, LANES), lambda b, i, g: (b, 0, KC)),
            pl.BlockSpec((1, seq, LANES), lambda b, i, g: (b, 0, VC)),
        ],
        out_specs=pl.BlockSpec((1, tq, LANES), lambda b, i, g: (b, i, g)),
        out_shape=jax.ShapeDtypeStruct((batch, seq, C_WIDTH), jnp.bfloat16),
        compiler_params=pltpu.CompilerParams(dimension_semantics=("arbitrary",) * 3, vmem_limit_bytes=VMEM_LIMIT),
        name="mixer_c",
    )(view, view, view)


def _rms(x, g):
    return x * lax.rsqrt(jnp.mean(x * x, axis=-1, keepdims=True) + EPS) * g


def _outproj_kernel(oa_ref, ob_ref, oc_ref, x_ref, w_ref, ga_ref, gb_ref, gc_ref, gf_ref, wr_ref, xo_ref, h_ref,
                    aff_ref):
    f32 = jnp.float32
    merged = jnp.concatenate([_rms(oa_ref[...].astype(f32), ga_ref[...]), _rms(ob_ref[...].astype(f32), gb_ref[...]),
                              _rms(oc_ref[...].astype(f32), gc_ref[...])], axis=-1).astype(jnp.bfloat16)
    xn = x_ref[...] + jnp.dot(merged, w_ref[...], preferred_element_type=f32)
    xo_ref[...] = xn
    h = _rms(xn, gf_ref[...])
    h_ref[:, :D_MODEL] = h
    h_ref[:, D_MODEL:] = jnp.zeros((h.shape[0], LANES), f32)
    logits = lax.dot_general(wr_ref[...], h, (((1,), (1,)), ((), ())), preferred_element_type=f32,
                             precision=lax.Precision.HIGHEST)
    z = jnp.exp(logits - jnp.max(logits, axis=0, keepdims=True))
    aff = z / jnp.sum(z, axis=0, keepdims=True)
    for c in range(aff.shape[1] // LANES):
        aff_ref[c] = aff[:, c * LANES:(c + 1) * LANES]


def _out_projection(oa, ob, oc, x, w, ga, gb, gc, gf, wr_t, tm):
    n = x.shape[0]
    rows = lambda width: pl.BlockSpec((tm, width), lambda i: (i, 0))
    const = lambda shape: pl.BlockSpec(shape, lambda i: (0, 0))
    return pl.pallas_call(
        _outproj_kernel,
        grid=(n // tm,),
        in_specs=[rows(A_WIDTH), rows(B_WIDTH), rows(C_WIDTH),
                  rows(D_MODEL), const((D_MODEL, D_MODEL)), const((1, A_WIDTH)), const((1, B_WIDTH)),
                  const((1, C_WIDTH)), const((1, D_MODEL)), const((N_EXPERTS, D_MODEL))],
        out_specs=[rows(D_MODEL), rows(ROW_WIDTH), pl.BlockSpec((tm // LANES, N_EXPERTS, LANES), lambda i: (i, 0, 0))],
        out_shape=[jax.ShapeDtypeStruct((n, D_MODEL), jnp.float32), jax.ShapeDtypeStruct((n, ROW_WIDTH), jnp.float32),
                   jax.ShapeDtypeStruct((n // LANES, N_EXPERTS, LANES), jnp.float32)],
        compiler_params=pltpu.CompilerParams(dimension_semantics=("arbitrary",), vmem_limit_bytes=VMEM_LIMIT),
        name="out_projection",
    )(oa, ob, oc, x, w, ga, gb, gc, gf, wr_t)


def _ffn_kernel(x_ref, wg_ref, wu_ref, wd_ref, o_ref, dest_ref, xb_scr, gate_scr):
    e = pl.program_id(0)
    j = pl.program_id(2)

    @pl.when(j == 0)
    def _():
        xb_scr[...] = x_ref[0, :, :D_MODEL].astype(jnp.bfloat16)
        route = x_ref[0, :, D_MODEL:]
        lane = lax.broadcasted_iota(jnp.int32, route.shape, 1)
        pick = lambda k: jnp.sum(jnp.where(lane == k + e, route, 0.0), axis=-1, keepdims=True)
        gate_scr[...] = pick(GATE_LANE)
        dest_ref[0] = pick(DEST_LANE).astype(jnp.int32)
        o_ref[0] = jnp.zeros(o_ref.shape[1:], o_ref.dtype)

    x = xb_scr[...]
    hg = jnp.dot(x, wg_ref[...].astype(jnp.bfloat16), preferred_element_type=jnp.float32)
    hu = jnp.dot(x, wu_ref[...].astype(jnp.bfloat16), preferred_element_type=jnp.float32)
    act = (hg * jax.nn.sigmoid(hg) * hu).astype(jnp.bfloat16)
    o_ref[0] += jnp.dot(act, wd_ref[...].astype(jnp.bfloat16), preferred_element_type=jnp.float32)

    @pl.when(j == pl.num_programs(2) - 1)
    def _():
        o_ref[0] = o_ref[0] * gate_scr[...]


def _expert_ffn(xe, w_gate, w_up, w_down, layer, tr, tf):
    n_e, rows, _ = xe.shape
    d = D_MODEL
    d_ff = w_gate.shape[-1]
    return pl.pallas_call(
        _ffn_kernel,
        grid=(n_e, rows // tr, d_ff // tf),
        in_specs=[
            pl.BlockSpec((1, tr, ROW_WIDTH), lambda e, c, j: (e, c, 0)),
            pl.BlockSpec((None, None, d, tf), lambda e, c, j: (layer, e, 0, j)),
            pl.BlockSpec((None, None, d, tf), lambda e, c, j: (layer, e, 0, j)),
            pl.BlockSpec((None, None, tf, d), lambda e, c, j: (layer, e, j, 0)),
        ],
        out_specs=[pl.BlockSpec((1, tr, d), lambda e, c, j: (e, c, 0)),
                   pl.BlockSpec((1, tr, 1), lambda e, c, j: (e, c, 0))],
        out_shape=[jax.ShapeDtypeStruct((n_e, rows, d), jnp.float32),
                   jax.ShapeDtypeStruct((n_e, rows, 1), jnp.int32)],
        scratch_shapes=[pltpu.VMEM((tr, d), jnp.bfloat16), pltpu.VMEM((tr, 1), jnp.float32)],
        compiler_params=pltpu.CompilerParams(dimension_semantics=("arbitrary",) * 3, vmem_limit_bytes=VMEM_LIMIT),
        name="expert_ffn",
    )(xe, w_gate, w_up, w_down)


def _lane_cumsum(m, tri):
    nc, r, _ = m.shape
    flat = m.reshape(nc * r, LANES).astype(jnp.bfloat16)
    return jnp.dot(flat, tri, preferred_element_type=jnp.float32).reshape(nc, r, LANES)


def _lead_cumsum_exclusive(t):
    n = t.shape[0]
    inc, k = t, 1
    while k < n:
        inc = inc + jnp.concatenate([jnp.zeros((k,) + t.shape[1:], t.dtype), inc[:n - k]], axis=0)
        k *= 2
    return inc - t


def _token_cumsum(m, tri):
    inside = _lane_cumsum(m, tri)
    total = inside[:, :, LANES - 1:]
    return _lead_cumsum_exclusive(total), inside, total


def _select_kernel(aff_ref, tri_ref, idx_ref, stats_ref, split_scr, before_scr, through_scr, *, groups):
    f32 = jnp.float32
    tri = tri_ref[...]
    for c0, nc, cap, s0 in groups:
        aff = aff_ref[c0:c0 + nc]
        bits = pltpu.bitcast(aff, jnp.int32)
        count = lambda mask: jnp.sum(jnp.sum(mask, axis=0, keepdims=True), axis=2, keepdims=True)

        def bisect(_, carry):
            lo, hi = carry
            mid = lo + ((hi - lo) >> 1)
            ok = count(jnp.where(bits >= mid, 1.0, 0.0)) >= cap
            return jnp.where(ok, mid, lo), jnp.where(ok, hi, mid)

        shape = (1, N_EXPERTS, 1)
        thr, _ = lax.fori_loop(0, BISECT_STEPS, bisect,
                               (jnp.zeros(shape, jnp.int32), jnp.full(shape, ONE_BITS + 1, jnp.int32)))
        above, tie = bits > thr, bits == thr
        tie_f = jnp.where(tie, 1.0, 0.0)
        need = cap - count(jnp.where(above, 1.0, 0.0))
        before, inside, _ = _token_cumsum(tie_f, tri)
        chosen = jnp.where(above | (tie & (before + inside - tie_f < need)), 1.0, 0.0)

        before, inside, total = _token_cumsum(chosen, tri)
        through = before + inside
        mult = jnp.sum(chosen, axis=1, keepdims=True)
        m_before, m_inside, _ = _token_cumsum(mult, tri)
        stats_ref[c0:c0 + nc, 0:N_EXPERTS, :] = chosen
        stats_ref[c0:c0 + nc, N_EXPERTS:N_EXPERTS + 1, :] = m_before + m_inside - mult
        stats_ref[c0:c0 + nc, N_EXPERTS + 1:N_EXPERTS + 2, :] = mult
        stats_ref[c0:c0 + nc, N_EXPERTS + 2:, :] = jnp.zeros((nc, STATS_ROWS - N_EXPERTS - 2, LANES), f32)

        for e in range(N_EXPERTS):
            t_e = through[:, e, :]
            hi_digit = jnp.floor(t_e * (1.0 / COUNT_DIGIT))
            split_scr[e, 0:nc, 0:LANES] = hi_digit.astype(jnp.bfloat16)
            split_scr[e, 0:nc, LANES:] = (t_e - COUNT_DIGIT * hi_digit).astype(jnp.bfloat16)
            before_scr[e, 0:nc, :] = jnp.broadcast_to(before[:, e, :], (nc, LANES))
            through_scr[e, 0:nc, :] = jnp.broadcast_to((before + total)[:, e, :], (nc, LANES))

        chunk_id = lax.broadcasted_iota(jnp.int32, (1, nc), 1).astype(f32)
        lane_id = lax.broadcasted_iota(jnp.int32, (1, LANES), 1)
        row_id = lax.broadcasted_iota(jnp.int32, (LANES, 1), 0)

        def compact(it, carry):
            e, s = it // (cap // LANES), it % (cap // LANES)
            slot_row = (s * LANES + lane_id).astype(f32)
            slot_col = (s * LANES + row_id).astype(f32)
            holds = (before_scr[e, 0:nc, :] <= slot_row) & (slot_row < through_scr[e, 0:nc, :])
            onehot = jnp.where(holds, 1.0, 0.0).T
            digits = jnp.dot(onehot.astype(jnp.bfloat16), split_scr[e, 0:nc, :], preferred_element_type=f32)
            counts = COUNT_DIGIT * digits[:, :LANES] + digits[:, LANES:]
            inside_pos = jnp.sum(jnp.where(counts <= slot_col, 1.0, 0.0), axis=-1, keepdims=True)
            chunk = jnp.sum(onehot * chunk_id, axis=-1, keepdims=True)
            token = (c0 + chunk) * LANES + inside_pos
            idx_ref[e, pl.ds(s0 + s, 1), :] = jnp.broadcast_to(token, (LANES, LANES)).T[0:1, :].astype(jnp.int32)
            return carry

        lax.fori_loop(0, N_EXPERTS * (cap // LANES), compact, 0, unroll=4)


def _select(aff, groups, slots):
    n_chunks = aff.shape[0]
    nc_max = max(nc for _, nc, _, _ in groups)
    assert all(cap <= 2 * LANES * COUNT_DIGIT for _, _, cap, _ in groups)
    tri = jnp.asarray(np.triu(np.ones((LANES, LANES))), jnp.bfloat16)
    return pl.pallas_call(
        functools.partial(_select_kernel, groups=groups),
        out_shape=[jax.ShapeDtypeStruct((N_EXPERTS, slots // LANES, LANES), jnp.int32),
                   jax.ShapeDtypeStruct((n_chunks, STATS_ROWS, LANES), jnp.float32)],
        scratch_shapes=[pltpu.VMEM((N_EXPERTS, nc_max, 2 * LANES), jnp.bfloat16),
                        pltpu.VMEM((N_EXPERTS, nc_max, LANES), jnp.float32),
                        pltpu.VMEM((N_EXPERTS, nc_max, LANES), jnp.float32)],
        compiler_params=pltpu.CompilerParams(vmem_limit_bytes=VMEM_LIMIT),
        name="expert_select",
    )(aff, tri)


def _route_rows_kernel(aff_ref, stats_ref, below_ref, rows_in_ref, o_ref):
    del rows_in_ref
    n = aff_ref.shape[0]
    pad = jnp.zeros((LANES - N_EXPERTS - stats_ref.shape[1], LANES), jnp.float32)
    lane = lax.broadcasted_iota(jnp.int32, (LANES, LANES), 1)
    for c in range(n):
        t = jnp.concatenate([aff_ref[c], stats_ref[c], pad], axis=0).T
        rank = jnp.dot(t.astype(jnp.bfloat16), below_ref[...], preferred_element_type=jnp.float32)
        first = t[:, OFF_LANE:OFF_LANE + 1]
        o_ref[c * LANES:(c + 1) * LANES, :] = jnp.where((lane >= DEST_LANE) & (lane < OFF_LANE), first + rank, t)


def _route_rows(aff, stats, rows, tm):
    n = rows.shape[0]
    k = tm // LANES
    below = np.zeros((LANES, LANES))
    below[DEST_LANE:OFF_LANE, DEST_LANE:OFF_LANE] = np.triu(np.ones((N_EXPERTS, N_EXPERTS)), 1)
    return pl.pallas_call(
        _route_rows_kernel,
        grid=(n // tm,),
        in_specs=[pl.BlockSpec((k, N_EXPERTS, LANES), lambda i: (i, 0, 0)),
                  pl.BlockSpec((k, stats.shape[1], LANES), lambda i: (i, 0, 0)),
                  pl.BlockSpec((LANES, LANES), lambda i: (0, 0)),
                  pl.BlockSpec(memory_space=pl.ANY)],
        out_specs=pl.BlockSpec((tm, LANES), lambda i: (i, D_MODEL // LANES)),
        out_shape=jax.ShapeDtypeStruct(rows.shape, rows.dtype),
        input_output_aliases={3: 0},
        compiler_params=pltpu.CompilerParams(dimension_semantics=("arbitrary",)),
        name="route_rows",
    )(aff, stats, jnp.asarray(below, jnp.bfloat16), rows)


SC_ROWS = 32


def _sc_mesh():
    return plsc.VectorSubcoreMesh(core_axis_name="core", subcore_axis_name="subcore")


def _sc_gather(table, idx):
    m, w = idx.shape[0], table.shape[1]
    per = m // (SC_WORKERS * SC_ROWS)
    assert per * SC_WORKERS * SC_ROWS == m

    @functools.partial(pl.kernel, out_type=jax.ShapeDtypeStruct((m, w), table.dtype), mesh=_sc_mesh(),
                       scratch_types=[pltpu.VMEM((1, SC_ROWS), jnp.int32), pltpu.VMEM((SC_ROWS, w), table.dtype)])
    def gather(table_hbm, idx_hbm, out_hbm, idx_v, buf):
        worker = lax.axis_index("core") * (SC_WORKERS // 2) + lax.axis_index("subcore")

        @pl.loop(0, per)
        def _(b):
            blk = worker * per + b
            pltpu.sync_copy(idx_hbm.at[pl.ds(blk, 1)], idx_v)
            pltpu.sync_copy(table_hbm.at[idx_v.at[0]], buf)
            pltpu.sync_copy(buf, out_hbm.at[pl.ds(blk * SC_ROWS, SC_ROWS)])

    return gather(table, idx.reshape(m // SC_ROWS, SC_ROWS))


def _sc_scatter(rows, dest):
    m, w = rows.shape
    per = m // (SC_WORKERS * SC_ROWS)
    assert per * SC_WORKERS * SC_ROWS == m

    @functools.partial(pl.kernel, out_type=jax.ShapeDtypeStruct((m, w), rows.dtype), mesh=_sc_mesh(),
                       scratch_types=[pltpu.VMEM((1, SC_ROWS), jnp.int32), pltpu.VMEM((SC_ROWS, w), rows.dtype)])
    def scatter(rows_hbm, dest_hbm, out_hbm, dest_v, buf):
        worker = lax.axis_index("core") * (SC_WORKERS // 2) + lax.axis_index("subcore")

        @pl.loop(0, per)
        def _(b):
            blk = worker * per + b
            pltpu.sync_copy(dest_hbm.at[pl.ds(blk, 1)], dest_v)
            pltpu.sync_copy(rows_hbm.at[pl.ds(blk * SC_ROWS, SC_ROWS)], buf)
            pltpu.sync_copy(buf, out_hbm.at[dest_v.at[0]])

    return scatter(rows, dest.reshape(m // SC_ROWS, SC_ROWS))


COMBINE_ROWS = 512


def _combine_kernel(tile_ref, start_ref, want_ref, flags_ref, x_ref, route_ref, z_ref, o_ref):
    s = pl.program_id(0)
    f32, bf16 = jnp.float32, jnp.bfloat16

    @pl.when(flags_ref[s] == 1)
    def _():
        o_ref[...] = x_ref[...]

    @pl.when(flags_ref[s] != 2)
    def _():
        first = route_ref[:, OFF_LANE:OFF_LANE + 1]
        last = first + route_ref[:, MULT_LANE:MULT_LANE + 1]
        row = start_ref[s] * SUBLANES + lax.broadcasted_iota(jnp.int32, (1, COMBINE_ROWS), 1)
        rowf = row.astype(f32)
        own = jnp.where((first <= rowf) & (rowf < last) & (row >= want_ref[s]), 1.0, 0.0).astype(bf16)
        z = z_ref[...]
        z1 = z.astype(bf16)
        r1 = z - z1.astype(f32)
        z2 = r1.astype(bf16)
        z3 = (r1 - z2.astype(f32)).astype(bf16)
        o_ref[...] += (jnp.dot(own, z1, preferred_element_type=f32) + jnp.dot(own, z2, preferred_element_type=f32)
                       + jnp.dot(own, z3, preferred_element_type=f32))


def _combine_steps(tile_lo, n_rows):
    n_tiles = tile_lo.shape[0] - 1
    n_steps_max = n_rows // COMBINE_ROWS + 2 * n_tiles + n_tiles // 16 + 1
    lo = (tile_lo[:-1] // SUBLANES) * SUBLANES
    per_tile = jnp.maximum((tile_lo[1:] - lo + COMBINE_ROWS - 1) // COMBINE_ROWS, 1)
    ends = jnp.cumsum(per_tile)
    s = jnp.arange(n_steps_max, dtype=jnp.int32)
    valid = s < ends[-1]
    tile = jnp.minimum(jnp.sum(ends[None, :] <= s[:, None], axis=1).astype(jnp.int32), n_tiles - 1)
    mine = tile[:, None] == jnp.arange(n_tiles, dtype=jnp.int32)[None, :]
    of_tile = lambda a: jnp.sum(jnp.where(mine, a[None, :], 0), axis=1)
    k = s - of_tile(ends - per_tile)
    want = of_tile(lo) + k * COMBINE_ROWS
    start = jnp.minimum(want, n_rows - COMBINE_ROWS)
    last_start = jnp.sum(jnp.where(s == ends[-1] - 1, start, 0))
    start = jnp.where(valid, start, last_start)
    flags = jnp.where(valid, (k == 0).astype(jnp.int32), 2)
    return tile, start // SUBLANES, want, flags, n_steps_max


def _combine(x, rows, z, tile_lo, tt):
    tile, start, want, flags, n_steps = _combine_steps(tile_lo, z.shape[0])
    return pl.pallas_call(
        _combine_kernel,
        grid_spec=pltpu.PrefetchScalarGridSpec(
            num_scalar_prefetch=4,
            grid=(n_steps,),
            in_specs=[pl.BlockSpec((tt, D_MODEL), lambda s, tile, *_: (tile[s], 0)),
                      pl.BlockSpec((tt, LANES), lambda s, tile, *_: (tile[s], D_MODEL // LANES)),
                      pl.BlockSpec((pl.Element(COMBINE_ROWS), pl.Element(D_MODEL)),
                                   lambda s, tile, start, *_: (start[s] * SUBLANES, 0))],
            out_specs=pl.BlockSpec((tt, D_MODEL), lambda s, tile, *_: (tile[s], 0)),
        ),
        out_shape=jax.ShapeDtypeStruct(x.shape, jnp.float32),
        compiler_params=pltpu.CompilerParams(dimension_semantics=("arbitrary",), vmem_limit_bytes=VMEM_LIMIT),
        name="expert_combine",
    )(tile, start, want, flags, x, rows, z)


def _final_norm_kernel(x_ref, g_ref, o_ref):
    o_ref[...] = _rms(x_ref[...], g_ref[...])


def _final_norm(x, g, tm):
    n = x.shape[0]
    return pl.pallas_call(
        _final_norm_kernel,
        grid=(n // tm,),
        in_specs=[pl.BlockSpec((tm, D_MODEL), lambda i: (i, 0)), pl.BlockSpec((1, D_MODEL), lambda i: (0, 0))],
        out_specs=pl.BlockSpec((tm, D_MODEL), lambda i: (i, 0)),
        out_shape=jax.ShapeDtypeStruct((n, D_MODEL), jnp.float32),
        name="final_norm",
    )(x, g)


def _forward(x_prompt, x_sample, g_attn, w_in, g_q_c, g_k_c, sink_b, g_out_a, g_out_b, g_out_c, w_out, g_ffn, w_router,
             w_gate, w_up, w_down, g_final, *, tm, tr, tf):
    bf16 = jnp.bfloat16
    shapes = (x_prompt.shape[:2], x_sample.shape[:2])
    assert all(l % tm == 0 and (b * l) % LANES == 0 for b, l in shapes)
    xs = [x_prompt.reshape(-1, D_MODEL), x_sample.reshape(-1, D_MODEL)]

    tables = _rope_tables(max(l for _, l in shapes))
    seg = jnp.asarray(np.kron(np.eye(LANES // HEAD_DIM), np.ones((HEAD_DIM, HEAD_DIM))), bf16)
    pb, pc, perm_out = _out_perms()
    w_in_p = w_in[:, :, _in_perm()].astype(bf16)
    w_out_p = w_out[:, perm_out, :].astype(bf16)
    sink_p = sink_b[:, np.asarray(B_Q_ORDER)]
    tile2 = lambda g: jnp.tile(g, (1, 2))[:, None, :]
    gq, gk = tile2(g_q_c), tile2(g_k_c)
    wr_t = jnp.swapaxes(w_router, 1, 2)
    tt = 256

    def layer(x, l, batch, seq):
        n = batch * seq
        cap = CAPACITY_FACTOR * n // N_EXPERTS
        assert cap % LANES == 0
        proj, a4, a16 = _in_projection(x, g_attn[l][None], w_in_p[l], tables, gq[l], gk[l], seg, seq, tm)
        oa = _mixer_a(proj, a4, a16, batch=batch, seq=seq).reshape(n, A_WIDTH)
        ob = _mixer_b(proj, sink_p[l], batch=batch, seq=seq).reshape(n, B_WIDTH)
        oc = _mixer_c(proj, batch=batch, seq=seq).reshape(n, C_WIDTH)
        x, rows, aff = _out_projection(oa, ob, oc, x, w_out_p[l], g_out_a[l][None], g_out_b[l][pb][None],
                                       g_out_c[l][pc][None], g_ffn[l][None], wr_t[l], tm)
        idx, stats = _select(aff, ((0, n // LANES, cap, 0),), cap)
        rows = _route_rows(aff, stats, rows, tm)
        xe = _sc_gather(rows, idx.reshape(-1)).reshape(N_EXPERTS, cap, ROW_WIDTH)
        ye, dest = _expert_ffn(xe, w_gate, w_up, w_down, l, min(tr, cap), tf)
        z = _sc_scatter(ye.reshape(-1, D_MODEL), dest.reshape(-1))
        first_slot = stats[::tt // LANES, N_EXPERTS, 0].astype(jnp.int32)
        tile_lo = jnp.concatenate([first_slot, jnp.full((1,), N_EXPERTS * cap, jnp.int32)])
        return _combine(x, rows, z, tile_lo, tt)

    for l in range(DEPTH):
        xs = [layer(x, l, b, s) for x, (b, s) in zip(xs, shapes)]
    return tuple(_final_norm(x, g_final[None], tm).reshape(b, s, D_MODEL) for x, (b, s) in zip(xs, shapes))


def kernel(x_prompt, x_sample, g_attn, w_in, g_q_c, g_k_c, sink_b, g_out_a, g_out_b, g_out_c, w_out, g_ffn, w_router,
           w_gate, w_up, w_down, g_final):
    return _forward(x_prompt, x_sample, g_attn, w_in, g_q_c, g_k_c, sink_b, g_out_a, g_out_b, g_out_c, w_out, g_ffn,
                    w_router, w_gate, w_up, w_down, g_final, tm=1024, tr=2048, tf=256)
```

```python
import functools
import math

import jax
import jax.numpy as jnp
import numpy as np
from jax import lax
from jax.experimental import pallas as pl
from jax.experimental.pallas import tpu as pltpu
from jax.experimental.pallas import tpu_sc as plsc

D_MODEL = 1024
DEPTH = 4
HEAD_DIM = 64
A_HEADS = 6
A_PAIRS = ((128, 1), (512, 4), (2048, 16))
B_HEADS = 4
B_KV_HEADS = 2
B_HALF_WINDOW = 128
C_HEADS = 6
C_KV_HEADS = 2
GRID_W = 64
ROPE_THETA = 10000.0
N_EXPERTS = 16
CAPACITY_FACTOR = 2
D_FF = 2816
EPS = 1e-6
NEG_INF = -1e30

LANES = 128
SUBLANES = 8
A_WIDTH = A_HEADS * HEAD_DIM
B_WIDTH = B_HEADS * HEAD_DIM
C_WIDTH = C_HEADS * HEAD_DIM
IN_WIDTH = 3 * A_WIDTH + B_WIDTH + 2 * B_KV_HEADS * HEAD_DIM + C_WIDTH + 2 * C_KV_HEADS * HEAD_DIM
N_GROUPS = IN_WIDTH // LANES
A_GROUPS = 3 * A_WIDTH // LANES
QA, KA, VA, QB, KB, VB, QC, KC, VC = 0, 3, 6, 9, 11, 12, 13, 16, 17
ROPE_NONE, ROPE_1D, ROPE_AXIAL_Q, ROPE_AXIAL_K = 0, 1, 2, 3
GROUP_KIND = ([(ROPE_1D, True)] * 3 + [(ROPE_1D, False)] * 3 + [(ROPE_NONE, False)] * 3
              + [(ROPE_1D, True)] * 2 + [(ROPE_1D, False)] + [(ROPE_NONE, False)]
              + [(ROPE_AXIAL_Q, True)] * 3 + [(ROPE_AXIAL_K, False)] + [(ROPE_NONE, False)])
Q_SCALE = HEAD_DIM ** -0.5
A_DILATIONS = tuple(d for _, d in A_PAIRS)
A_HALF_WINDOW = A_PAIRS[0][0] // 2
assert all(w // 2 // d == A_HALF_WINDOW for w, d in A_PAIRS) and A_DILATIONS == (1, 4, 16)
BAND_TQ = 128
ROW_WIDTH = D_MODEL + LANES
GATE_LANE, DEST_LANE, OFF_LANE, MULT_LANE = 0, N_EXPERTS, 2 * N_EXPERTS, 2 * N_EXPERTS + 1
ONE_BITS = 0x3F800000
BISECT_STEPS = 31
STATS_ROWS = 24
COUNT_DIGIT = 64
SC_WORKERS = 32

VMEM_LIMIT = 56 * 1024 * 1024

B_Q_ORDER = (0, 2, 1, 3)
C_Q_ORDER = (0, 3, 1, 4, 2, 5)


def _head_perm(order):
    return np.concatenate([np.arange(h * HEAD_DIM, (h + 1) * HEAD_DIM) for h in order])


def _in_perm():
    widths = [A_WIDTH] * 3 + [B_WIDTH, 128, 128, C_WIDTH, 128, 128]
    offs = np.concatenate([[0], np.cumsum(widths)])
    parts = [np.arange(offs[i], offs[i + 1]) for i in range(9)]
    parts[3] = offs[3] + _head_perm(B_Q_ORDER)
    parts[6] = offs[6] + _head_perm(C_Q_ORDER)
    return np.concatenate(parts)


def _out_perms():
    pb = _head_perm(B_Q_ORDER)
    pc = _head_perm(C_Q_ORDER)
    return pb, pc, np.concatenate([np.arange(A_WIDTH), A_WIDTH + pb, A_WIDTH + B_WIDTH + pc])


def _rope_tables(seq):
    pos = jnp.arange(seq, dtype=jnp.float32)
    inv1 = ROPE_THETA ** (-jnp.arange(0, HEAD_DIM, 2, dtype=jnp.float32) / HEAD_DIM)
    ang = pos[:, None] * inv1[None, :]
    c, s = jnp.cos(ang), jnp.sin(ang)
    cos1 = jnp.tile(jnp.concatenate([c, c], -1), (1, 2))
    sin1 = jnp.tile(jnp.concatenate([-s, s], -1), (1, 2))
    half = HEAD_DIM // 2
    inv2 = ROPE_THETA ** (-jnp.arange(0, half, 2, dtype=jnp.float32) / half)
    row = jnp.floor(pos / GRID_W)
    col = pos - row * GRID_W
    ar, ac = row[:, None] * inv2[None, :], col[:, None] * inv2[None, :]
    cr, sr, cc, sc = jnp.cos(ar), jnp.sin(ar), jnp.cos(ac), jnp.sin(ac)
    cos2 = jnp.tile(jnp.concatenate([cr, cr, cc, cc], -1), (1, 2))
    sin2 = jnp.tile(jnp.concatenate([-sr, sr, -sc, sc], -1), (1, 2))
    return cos1, sin1, cos2, sin2


def _swap_halves(x, block):
    half = block // 2
    lane = lax.broadcasted_iota(jnp.int32, x.shape, 1)
    return jnp.where(lane % block < half, pltpu.roll(x, LANES - half, 1), pltpu.roll(x, half, 1))


def _inproj_kernel(x_ref, g_ref, w_ref, cos1_ref, sin1_ref, cos2_ref, sin2_ref, gq_ref, gk_ref, seg_ref, o_ref,
                   a4_ref, a16_ref, rows_ref):
    x = x_ref[...]
    tm = x.shape[0]
    y = x * lax.rsqrt(jnp.mean(x * x, axis=-1, keepdims=True) + EPS)
    h = (y * g_ref[...]).astype(jnp.bfloat16)
    for c in range(N_GROUPS // 2):
        acc = jnp.dot(h, w_ref[:, c * 2 * LANES:(c + 1) * 2 * LANES], preferred_element_type=jnp.float32)
        for half in range(2):
            grp = 2 * c + half
            cols = slice(grp * LANES, (grp + 1) * LANES)
            a = acc[:, half * LANES:(half + 1) * LANES]
            kind, is_q = GROUP_KIND[grp]
            if kind == ROPE_1D:
                a = a * cos1_ref[...] + _swap_halves(a, HEAD_DIM) * sin1_ref[...]
            elif kind in (ROPE_AXIAL_Q, ROPE_AXIAL_K):
                gain = gq_ref[...] if kind == ROPE_AXIAL_Q else gk_ref[...]
                sq = a * a
                sq_hi = sq.astype(jnp.bfloat16)
                sq_lo = (sq - sq_hi.astype(jnp.float32)).astype(jnp.bfloat16)
                ss = (jnp.dot(sq_hi, seg_ref[...], preferred_element_type=jnp.float32)
                      + jnp.dot(sq_lo, seg_ref[...], preferred_element_type=jnp.float32))
                a = a * lax.rsqrt(ss * (1.0 / HEAD_DIM) + EPS) * gain
                a = a * cos2_ref[...] + _swap_halves(a, HEAD_DIM // 2) * sin2_ref[...]
            if is_q:
                a = a * Q_SCALE
            o_ref[:, cols] = a.astype(jnp.bfloat16)
            if grp < A_GROUPS:
                rows_ref[grp] = a
        if c == (A_GROUPS - 1) // 2:
            for d, ref in ((4, a4_ref), (16, a16_ref)):
                for r in range(d):
                    for grp in range(A_GROUPS):
                        ref[r, :, grp * LANES:(grp + 1) * LANES] = (
                            rows_ref[grp, pl.ds(r, tm // d, stride=d), :].astype(jnp.bfloat16))


def _in_projection(x, g, w, tables, gq, gk, seg, seq, tm):
    n = x.shape[0]
    tab_spec = pl.BlockSpec((tm, LANES), lambda i: (i % (seq // tm), 0))
    const = lambda shape: pl.BlockSpec(shape, lambda i: (0, 0))
    wa = A_GROUPS * LANES
    return pl.pallas_call(
        _inproj_kernel,
        grid=(n // tm,),
        in_specs=[pl.BlockSpec((tm, D_MODEL), lambda i: (i, 0)), const((1, D_MODEL)), const((D_MODEL, IN_WIDTH)),
                  tab_spec, tab_spec, tab_spec, tab_spec, const((1, LANES)), const((1, LANES)), const((LANES, LANES))],
        out_specs=[pl.BlockSpec((tm, IN_WIDTH), lambda i: (i, 0)),
                   pl.BlockSpec((4, tm // 4, wa), lambda i: (0, i, 0)),
                   pl.BlockSpec((16, tm // 16, wa), lambda i: (0, i, 0))],
        out_shape=[jax.ShapeDtypeStruct((n, IN_WIDTH), jnp.bfloat16),
                   jax.ShapeDtypeStruct((4, n // 4, wa), jnp.bfloat16),
                   jax.ShapeDtypeStruct((16, n // 16, wa), jnp.bfloat16)],
        scratch_shapes=[pltpu.VMEM((A_GROUPS, tm, LANES), jnp.float32)],
        compiler_params=pltpu.CompilerParams(dimension_semantics=("arbitrary",), vmem_limit_bytes=VMEM_LIMIT),
        name="in_projection",
    )(x, g, w, *tables, gq, gk, seg)


def _stack_heads(q):
    lane = lax.broadcasted_iota(jnp.int32, q.shape, 1)
    zero = jnp.zeros_like(q)
    return jnp.concatenate([jnp.where(lane < HEAD_DIM, q, zero), jnp.where(lane >= HEAD_DIM, q, zero)], axis=0)


def _unstack_heads(x, tq):
    lane = lax.broadcasted_iota(jnp.int32, (tq, x.shape[1]), 1)
    return jnp.where(lane < HEAD_DIM, x[:tq], x[tq:])


def _unstack_column(col, tq):
    return _unstack_heads(jnp.broadcast_to(col, (2 * tq, LANES)), tq)


def _band_bias(tq, win, half_window):
    row = np.arange(2 * tq)[:, None] % tq
    col = np.arange(win)[None, :]
    kinds = [np.where(np.abs(row + off - col) <= half_window, 0.0, NEG_INF) for off in (0, half_window, 2 * half_window)]
    return jnp.asarray(np.stack(kinds), jnp.float32)


def _band_tile(q, kw, vw, bias, sink=None):
    s = lax.dot_general(_stack_heads(q), kw, (((1,), (1,)), ((), ())), preferred_element_type=jnp.float32) + bias
    m = jnp.max(s, axis=-1, keepdims=True)
    if sink is not None:
        m = jnp.maximum(m, sink)
    p = jnp.exp(s - m)
    den = jnp.sum(p, axis=-1, keepdims=True)
    if sink is not None:
        den = den + jnp.exp(sink - m)
    num = jnp.dot(p.astype(jnp.bfloat16), vw, preferred_element_type=jnp.float32)
    return num, m, den


def _tile_window(i, n_tiles, tq, win, half_window, seq):
    start = pl.multiple_of(jnp.clip(i * tq - half_window, 0, seq - win), math.gcd(tq, half_window, seq - win))
    kind = jnp.where(i == 0, 0, jnp.where(i == n_tiles - 1, 2, 1))
    return start, kind


def _mixer_a_kernel(bias1_ref, bias4_ref, bias16_ref, q1_ref, k1_ref, v1_ref, q4_ref, k4_ref, v4_ref, q16_ref, k16_ref, v16_ref,
                    o_ref, m_scr, l_scr, n_scr, *, seq):
    tq, hw = BAND_TQ, A_HALF_WINDOW

    def run_tile(q_ref, k_ref, v_ref, b_ref, lead, i, ls):
        tqc = min(tq, ls)
        win = min(tqc + 2 * hw, ls)
        n_tiles = ls // tqc
        start, kind = _tile_window(i, n_tiles, tqc, win, hw, ls)
        num, m, den = _band_tile(q_ref[lead, pl.ds(i * tqc, tqc), :], k_ref[lead, pl.ds(start, win), :],
                                 v_ref[lead, pl.ds(start, win), :], b_ref[kind])
        return _unstack_heads(num, tqc), _unstack_column(m, tqc), _unstack_column(den, tqc), tqc

    def tile1(i, carry):
        num, m, den, _ = run_tile(q1_ref, k1_ref, v1_ref, bias1_ref, 0, i, seq)
        rows = pl.ds(pl.multiple_of(i * tq, tq), tq)
        m_scr[rows, :] = m
        l_scr[rows, :] = den
        n_scr[rows, :] = num
        return carry

    lax.fori_loop(0, seq // tq, tile1, 0, unroll=16)

    def merge(tiles):
        old = [(m_scr[rows, :], l_scr[rows, :], n_scr[rows, :]) for rows, _, _, _ in tiles]
        for (rows, num, m, den), (m_old, l_old, n_old) in zip(tiles, old):
            m_new = jnp.maximum(m_old, m)
            a, b = jnp.exp(m_old - m_new), jnp.exp(m - m_new)
            m_scr[rows, :] = m_new
            l_scr[rows, :] = a * l_old + b * den
            n_scr[rows, :] = a * n_old + b * num

    ls4 = seq // 4

    def tile4(i, carry):
        tiles = []
        for r in range(4):
            num, m, den, tqc = run_tile(q4_ref, k4_ref, v4_ref, bias4_ref, r, i, ls4)
            tiles.append((pl.ds(i * (tqc * 4) + r, tqc, stride=4), num, m, den))
        merge(tiles)
        return carry

    lax.fori_loop(0, ls4 // min(tq, ls4), tile4, 0, unroll=4)

    ls16 = seq // 16

    def class16(r2, carry):
        tiles = []
        for r in (2 * r2, 2 * r2 + 1):
            for i in range(ls16 // min(tq, ls16)):
                num, m, den, tqc = run_tile(q16_ref, k16_ref, v16_ref, bias16_ref, r, i, ls16)
                tiles.append((pl.ds(i * (tqc * 16) + r, tqc, stride=16), num, m, den))
        merge(tiles)
        return carry

    lax.fori_loop(0, 8, class16, 0, unroll=2)
    o_ref[0] = (n_scr[...] * (1.0 / l_scr[...])).astype(o_ref.dtype)


def _mixer_a(proj, a4, a16, *, batch, seq):
    n = proj.shape[0]
    hw = A_HALF_WINDOW

    def class_bias(ls):
        tq = min(BAND_TQ, ls)
        return _band_bias(tq, min(tq + 2 * hw, ls), hw)

    biases = [class_bias(seq // d) for d in A_DILATIONS]
    view = proj.reshape(n // seq, seq, IN_WIDTH)
    nat = lambda off: pl.BlockSpec((1, seq, LANES), lambda b, g: (b, 0, off + g))
    cls = lambda d, off: pl.BlockSpec((d, seq // d, LANES), lambda b, g: (0, b, off + g))
    full = lambda a: pl.BlockSpec(a.shape, lambda b, g: (0, 0, 0))
    return pl.pallas_call(
        functools.partial(_mixer_a_kernel, seq=seq),
        grid=(batch, A_WIDTH // LANES),
        in_specs=[full(biases[0]), full(biases[1]), full(biases[2]), nat(QA), nat(KA), nat(VA), cls(4, QA), cls(4, KA), cls(4, VA),
                  cls(16, QA), cls(16, KA), cls(16, VA)],
        out_specs=pl.BlockSpec((1, seq, LANES), lambda b, g: (b, 0, g)),
        out_shape=jax.ShapeDtypeStruct((batch, seq, A_WIDTH), jnp.bfloat16),
        scratch_shapes=[pltpu.VMEM((seq, LANES), jnp.float32)] * 3,
        compiler_params=pltpu.CompilerParams(dimension_semantics=("arbitrary",) * 2, vmem_limit_bytes=VMEM_LIMIT),
        name="mixer_a",
    )(*biases, view, view, view, a4, a4, a4, a16, a16, a16)


def _mixer_b_kernel(sink_ref, bias_ref, q_ref, k_ref, v_ref, o_ref, *, seq):
    tq, hw = BAND_TQ, B_HALF_WINDOW
    win = tq + 2 * hw
    n_tiles = seq // tq
    g = pl.program_id(1)
    row = lax.broadcasted_iota(jnp.int32, (2 * tq, 1), 0)
    sink = jnp.where(row < tq, sink_ref[g], sink_ref[g + B_KV_HEADS])

    def tile(i, carry):
        start, kind = _tile_window(i, n_tiles, tq, win, hw, seq)
        rows = pl.ds(pl.multiple_of(i * tq, tq), tq)
        num, _, den = _band_tile(q_ref[0, rows, :], k_ref[0, pl.ds(start, win), :], v_ref[0, pl.ds(start, win), :],
                                 bias_ref[kind], sink)
        o_ref[0, rows, :] = _unstack_heads(num * (1.0 / den), tq).astype(o_ref.dtype)
        return carry

    lax.fori_loop(0, n_tiles, tile, 0, unroll=8)


def _mixer_b(proj, sink, *, batch, seq):
    n = proj.shape[0]
    bias = _band_bias(BAND_TQ, BAND_TQ + 2 * B_HALF_WINDOW, B_HALF_WINDOW)
    view = proj.reshape(n // seq, seq, IN_WIDTH)
    return pl.pallas_call(
        functools.partial(_mixer_b_kernel, seq=seq),
        grid=(batch, B_WIDTH // LANES),
        in_specs=[pl.BlockSpec(memory_space=pltpu.SMEM), pl.BlockSpec(bias.shape, lambda b, g: (0, 0, 0)),
                  pl.BlockSpec((1, seq, LANES), lambda b, g: (b, 0, QB + g)),
                  pl.BlockSpec((1, seq, LANES), lambda b, g: (b, 0, KB)),
                  pl.BlockSpec((1, seq, LANES), lambda b, g: (b, 0, VB))],
        out_specs=pl.BlockSpec((1, seq, LANES), lambda b, g: (b, 0, g)),
        out_shape=jax.ShapeDtypeStruct((batch, seq, B_WIDTH), jnp.bfloat16),
        compiler_params=pltpu.CompilerParams(dimension_semantics=("arbitrary",) * 2, vmem_limit_bytes=VMEM_LIMIT),
        name="mixer_b",
    )(sink, bias, view, view, view)


def _mixer_c_kernel(*refs, tq, chunk):
    q_refs, (k_ref, v_ref, o_ref) = refs[:-3], refs[-3:]
    seq = k_ref.shape[1]
    for g, q_ref in enumerate(q_refs):
        lhs = _stack_heads(q_ref[0])
        m = den = acc = None
        for c in range(seq // chunk):
            keys = slice(c * chunk, (c + 1) * chunk)
            s = lax.dot_general(lhs, k_ref[0, keys, :], (((1,), (1,)), ((), ())), preferred_element_type=jnp.float32)
            m_c = jnp.max(s, axis=-1, keepdims=True)
            m_new = m_c if m is None else jnp.maximum(m, m_c)
            p = jnp.exp(s - m_new)
            den_c = jnp.sum(p, axis=-1, keepdims=True)
            acc_c = jnp.dot(p.astype(jnp.bfloat16), v_ref[0, keys, :], preferred_element_type=jnp.float32)
            if m is None:
                den, acc = den_c, acc_c
            else:
                alpha = jnp.exp(m - m_new)
                den, acc = alpha * den + den_c, alpha * acc + acc_c
            m = m_new
        o_ref[0, :, g * LANES:(g + 1) * LANES] = _unstack_heads(acc * (1.0 / den), tq).astype(o_ref.dtype)


def _mixer_c(proj, *, batch, seq, tq=512, chunk=1024):
    n = proj.shape[0]
    tq, chunk = min(tq, seq), min(chunk, seq)
    view = proj.reshape(n // seq, seq, IN_WIDTH)
    n_groups = C_WIDTH // LANES
    q_spec = lambda g: pl.BlockSpec((1, tq, LANES), lambda b, i: (b, i, QC + g))
    return pl.pallas_call(
        functools.partial(_mixer_c_kernel, tq=tq, chunk=chunk),
        grid=(batch, seq // tq),
        in_specs=[q_spec(g) for g in range(n_groups)] + [
            pl.BlockSpec((1, seq, LANES), lambda b, i: (b, 0, KC)),
            pl.BlockSpec((1, seq, LANES), lambda b, i: (b, 0, VC)),
        ],
        out_specs=pl.BlockSpec((1, tq, C_WIDTH), lambda b, i: (b, i, 0)),
        out_shape=jax.ShapeDtypeStruct((batch, seq, C_WIDTH), jnp.bfloat16),
        compiler_params=pltpu.CompilerParams(dimension_semantics=("arbitrary",) * 2, vmem_limit_bytes=VMEM_LIMIT),
        name="mixer_c",
    )(*([view] * (n_groups + 2)))


def _rms(x, g):
    return x * lax.rsqrt(jnp.mean(x * x, axis=-1, keepdims=True) + EPS) * g


def _outproj_kernel(oa_ref, ob_ref, oc_ref, x_ref, w_ref, ga_ref, gb_ref, gc_ref, gf_ref, wr_ref, xo_ref, h_ref,
                    aff_ref):
    f32 = jnp.float32
    merged = jnp.concatenate([_rms(oa_ref[...].astype(f32), ga_ref[...]), _rms(ob_ref[...].astype(f32), gb_ref[...]),
                              _rms(oc_ref[...].astype(f32), gc_ref[...])], axis=-1).astype(jnp.bfloat16)
    xn = x_ref[...] + jnp.dot(merged, w_ref[...], preferred_element_type=f32)
    xo_ref[...] = xn
    h = _rms(xn, gf_ref[...])
    h_ref[:, :D_MODEL] = h
    h_ref[:, D_MODEL:] = jnp.zeros((h.shape[0], LANES), f32)
    logits = lax.dot_general(wr_ref[...], h, (((1,), (1,)), ((), ())), preferred_element_type=f32,
                             precision=lax.Precision.HIGHEST)
    z = jnp.exp(logits - jnp.max(logits, axis=0, keepdims=True))
    aff = z / jnp.sum(z, axis=0, keepdims=True)
    for c in range(aff.shape[1] // LANES):
        aff_ref[c] = aff[:, c * LANES:(c + 1) * LANES]


def _out_projection(oa, ob, oc, x, w, ga, gb, gc, gf, wr_t, tm):
    n = x.shape[0]
    rows = lambda width: pl.BlockSpec((tm, width), lambda i: (i, 0))
    const = lambda shape: pl.BlockSpec(shape, lambda i: (0, 0))
    return pl.pallas_call(
        _outproj_kernel,
        grid=(n // tm,),
        in_specs=[rows(A_WIDTH), rows(B_WIDTH), rows(C_WIDTH),
                  rows(D_MODEL), const((D_MODEL, D_MODEL)), const((1, A_WIDTH)), const((1, B_WIDTH)),
                  const((1, C_WIDTH)), const((1, D_MODEL)), const((N_EXPERTS, D_MODEL))],
        out_specs=[rows(D_MODEL), rows(ROW_WIDTH), pl.BlockSpec((tm // LANES, N_EXPERTS, LANES), lambda i: (i, 0, 0))],
        out_shape=[jax.ShapeDtypeStruct((n, D_MODEL), jnp.float32), jax.ShapeDtypeStruct((n, ROW_WIDTH), jnp.float32),
                   jax.ShapeDtypeStruct((n // LANES, N_EXPERTS, LANES), jnp.float32)],
        compiler_params=pltpu.CompilerParams(dimension_semantics=("arbitrary",), vmem_limit_bytes=VMEM_LIMIT),
        name="out_projection",
    )(oa, ob, oc, x, w, ga, gb, gc, gf, wr_t)


def _ffn_kernel(x_ref, wg_ref, wu_ref, wd_ref, o_ref, dest_ref, xb_scr, gate_scr):
    e = pl.program_id(0)
    j = pl.program_id(2)

    @pl.when(j == 0)
    def _():
        xb_scr[...] = x_ref[0, :, :D_MODEL].astype(jnp.bfloat16)
        route = x_ref[0, :, D_MODEL:]
        lane = lax.broadcasted_iota(jnp.int32, route.shape, 1)
        pick = lambda k: jnp.sum(jnp.where(lane == k + e, route, 0.0), axis=-1, keepdims=True)
        gate_scr[...] = pick(GATE_LANE)
        dest_ref[0] = pick(DEST_LANE).astype(jnp.int32)
        o_ref[0] = jnp.zeros(o_ref.shape[1:], o_ref.dtype)

    x = xb_scr[...]
    hg = jnp.dot(x, wg_ref[...].astype(jnp.bfloat16), preferred_element_type=jnp.float32)
    hu = jnp.dot(x, wu_ref[...].astype(jnp.bfloat16), preferred_element_type=jnp.float32)
    act = (hg * jax.nn.sigmoid(hg) * hu).astype(jnp.bfloat16)
    o_ref[0] += jnp.dot(act, wd_ref[...].astype(jnp.bfloat16), preferred_element_type=jnp.float32)

    @pl.when(j == pl.num_programs(2) - 1)
    def _():
        o_ref[0] = o_ref[0] * gate_scr[...]


def _expert_ffn(xe, w_gate, w_up, w_down, layer, tr, tf):
    n_e, rows, _ = xe.shape
    d = D_MODEL
    d_ff = w_gate.shape[-1]
    return pl.pallas_call(
        _ffn_kernel,
        grid=(n_e, rows // tr, d_ff // tf),
        in_specs=[
            pl.BlockSpec((1, tr, ROW_WIDTH), lambda e, c, j: (e, c, 0)),
            pl.BlockSpec((None, None, d, tf), lambda e, c, j: (layer, e, 0, j)),
            pl.BlockSpec((None, None, d, tf), lambda e, c, j: (layer, e, 0, j)),
            pl.BlockSpec((None, None, tf, d), lambda e, c, j: (layer, e, j, 0)),
        ],
        out_specs=[pl.BlockSpec((1, tr, d), lambda e, c, j: (e, c, 0)),
                   pl.BlockSpec((1, tr, 1), lambda e, c, j: (e, c, 0))],
        out_shape=[jax.ShapeDtypeStruct((n_e, rows, d), jnp.float32),
                   jax.ShapeDtypeStruct((n_e, rows, 1), jnp.int32)],
        scratch_shapes=[pltpu.VMEM((tr, d), jnp.bfloat16), pltpu.VMEM((tr, 1), jnp.float32)],
        compiler_params=pltpu.CompilerParams(dimension_semantics=("arbitrary",) * 3, vmem_limit_bytes=VMEM_LIMIT),
        name="expert_ffn",
    )(xe, w_gate, w_up, w_down)


def _lane_cumsum(m, tri):
    nc, r, _ = m.shape
    flat = m.reshape(nc * r, LANES).astype(jnp.bfloat16)
    return jnp.dot(flat, tri, preferred_element_type=jnp.float32).reshape(nc, r, LANES)


def _lead_cumsum_exclusive(t):
    n = t.shape[0]
    inc, k = t, 1
    while k < n:
        inc = inc + jnp.concatenate([jnp.zeros((k,) + t.shape[1:], t.dtype), inc[:n - k]], axis=0)
        k *= 2
    return inc - t


def _token_cumsum(m, tri):
    inside = _lane_cumsum(m, tri)
    total = inside[:, :, LANES - 1:]
    return _lead_cumsum_exclusive(total), inside, total


def _select_kernel(aff_ref, tri_ref, idx_ref, stats_ref, split_scr, before_scr, through_scr, *, groups):
    f32 = jnp.float32
    tri = tri_ref[...]
    for c0, nc, cap, s0 in groups:
        aff = aff_ref[c0:c0 + nc]
        bits = pltpu.bitcast(aff, jnp.int32)
        count = lambda mask: jnp.sum(jnp.sum(mask, axis=0, keepdims=True), axis=2, keepdims=True)

        def bisect(_, carry):
            lo, hi = carry
            mid = lo + ((hi - lo) >> 1)
            ok = count(jnp.where(bits >= mid, 1.0, 0.0)) >= cap
            return jnp.where(ok, mid, lo), jnp.where(ok, hi, mid)

        shape = (1, N_EXPERTS, 1)
        thr, _ = lax.fori_loop(0, BISECT_STEPS, bisect,
                               (jnp.zeros(shape, jnp.int32), jnp.full(shape, ONE_BITS + 1, jnp.int32)))
        above, tie = bits > thr, bits == thr
        tie_f = jnp.where(tie, 1.0, 0.0)
        need = cap - count(jnp.where(above, 1.0, 0.0))
        before, inside, _ = _token_cumsum(tie_f, tri)
        chosen = jnp.where(above | (tie & (before + inside - tie_f < need)), 1.0, 0.0)

        before, inside, total = _token_cumsum(chosen, tri)
        through = before + inside
        mult = jnp.sum(chosen, axis=1, keepdims=True)
        m_before, m_inside, _ = _token_cumsum(mult, tri)
        stats_ref[c0:c0 + nc, 0:N_EXPERTS, :] = chosen
        stats_ref[c0:c0 + nc, N_EXPERTS:N_EXPERTS + 1, :] = m_before + m_inside - mult
        stats_ref[c0:c0 + nc, N_EXPERTS + 1:N_EXPERTS + 2, :] = mult
        stats_ref[c0:c0 + nc, N_EXPERTS + 2:, :] = jnp.zeros((nc, STATS_ROWS - N_EXPERTS - 2, LANES), f32)

        for e in range(N_EXPERTS):
            t_e = through[:, e, :]
            hi_digit = jnp.floor(t_e * (1.0 / COUNT_DIGIT))
            split_scr[e, 0:nc, 0:LANES] = hi_digit.astype(jnp.bfloat16)
            split_scr[e, 0:nc, LANES:] = (t_e - COUNT_DIGIT * hi_digit).astype(jnp.bfloat16)
            before_scr[e, 0:nc, :] = jnp.broadcast_to(before[:, e, :], (nc, LANES))
            through_scr[e, 0:nc, :] = jnp.broadcast_to((before + total)[:, e, :], (nc, LANES))

        chunk_id = lax.broadcasted_iota(jnp.int32, (1, nc), 1).astype(f32)
        lane_id = lax.broadcasted_iota(jnp.int32, (1, LANES), 1)
        row_id = lax.broadcasted_iota(jnp.int32, (LANES, 1), 0)

        def compact(it, carry):
            e, s = it // (cap // LANES), it % (cap // LANES)
            slot_row = (s * LANES + lane_id).astype(f32)
            slot_col = (s * LANES + row_id).astype(f32)
            holds = (before_scr[e, 0:nc, :] <= slot_row) & (slot_row < through_scr[e, 0:nc, :])
            onehot = jnp.where(holds, 1.0, 0.0).T
            digits = jnp.dot(onehot.astype(jnp.bfloat16), split_scr[e, 0:nc, :], preferred_element_type=f32)
            counts = COUNT_DIGIT * digits[:, :LANES] + digits[:, LANES:]
            inside_pos = jnp.sum(jnp.where(counts <= slot_col, 1.0, 0.0), axis=-1, keepdims=True)
            chunk = jnp.sum(onehot * chunk_id, axis=-1, keepdims=True)
            token = (c0 + chunk) * LANES + inside_pos
            idx_ref[e, pl.ds(s0 + s, 1), :] = jnp.broadcast_to(token, (LANES, LANES)).T[0:1, :].astype(jnp.int32)
            return carry

        lax.fori_loop(0, N_EXPERTS * (cap // LANES), compact, 0, unroll=4)


def _select(aff, groups, slots):
    n_chunks = aff.shape[0]
    nc_max = max(nc for _, nc, _, _ in groups)
    assert all(cap <= 2 * LANES * COUNT_DIGIT for _, _, cap, _ in groups)
    tri = jnp.asarray(np.triu(np.ones((LANES, LANES))), jnp.bfloat16)
    return pl.pallas_call(
        functools.partial(_select_kernel, groups=groups),
        out_shape=[jax.ShapeDtypeStruct((N_EXPERTS, slots // LANES, LANES), jnp.int32),
                   jax.ShapeDtypeStruct((n_chunks, STATS_ROWS, LANES), jnp.float32)],
        scratch_shapes=[pltpu.VMEM((N_EXPERTS, nc_max, 2 * LANES), jnp.bfloat16),
                        pltpu.VMEM((N_EXPERTS, nc_max, LANES), jnp.float32),
                        pltpu.VMEM((N_EXPERTS, nc_max, LANES), jnp.float32)],
        compiler_params=pltpu.CompilerParams(vmem_limit_bytes=VMEM_LIMIT),
        name="expert_select",
    )(aff, tri)


def _route_rows_kernel(aff_ref, stats_ref, below_ref, rows_in_ref, o_ref):
    del rows_in_ref
    n = aff_ref.shape[0]
    pad = jnp.zeros((LANES - N_EXPERTS - stats_ref.shape[1], LANES), jnp.float32)
    lane = lax.broadcasted_iota(jnp.int32, (LANES, LANES), 1)
    for c in range(n):
        t = jnp.concatenate([aff_ref[c], stats_ref[c], pad], axis=0).T
        rank = jnp.dot(t.astype(jnp.bfloat16), below_ref[...], preferred_element_type=jnp.float32)
        first = t[:, OFF_LANE:OFF_LANE + 1]
        o_ref[c * LANES:(c + 1) * LANES, :] = jnp.where((lane >= DEST_LANE) & (lane < OFF_LANE), first + rank, t)


def _route_rows(aff, stats, rows, tm):
    n = rows.shape[0]
    k = tm // LANES
    below = np.zeros((LANES, LANES))
    below[DEST_LANE:OFF_LANE, DEST_LANE:OFF_LANE] = np.triu(np.ones((N_EXPERTS, N_EXPERTS)), 1)
    return pl.pallas_call(
        _route_rows_kernel,
        grid=(n // tm,),
        in_specs=[pl.BlockSpec((k, N_EXPERTS, LANES), lambda i: (i, 0, 0)),
                  pl.BlockSpec((k, stats.shape[1], LANES), lambda i: (i, 0, 0)),
                  pl.BlockSpec((LANES, LANES), lambda i: (0, 0)),
                  pl.BlockSpec(memory_space=pl.ANY)],
        out_specs=pl.BlockSpec((tm, LANES), lambda i: (i, D_MODEL // LANES)),
        out_shape=jax.ShapeDtypeStruct(rows.shape, rows.dtype),
        input_output_aliases={3: 0},
        compiler_params=pltpu.CompilerParams(dimension_semantics=("arbitrary",)),
        name="route_rows",
    )(aff, stats, jnp.asarray(below, jnp.bfloat16), rows)


SC_ROWS = 32


def _sc_mesh():
    return plsc.VectorSubcoreMesh(core_axis_name="core", subcore_axis_name="subcore")


def _sc_gather(table, idx):
    m, w = idx.shape[0], table.shape[1]
    per = m // (SC_WORKERS * SC_ROWS)
    assert per * SC_WORKERS * SC_ROWS == m

    @functools.partial(pl.kernel, out_type=jax.ShapeDtypeStruct((m, w), table.dtype), mesh=_sc_mesh(),
                       scratch_types=[pltpu.VMEM((1, SC_ROWS), jnp.int32), pltpu.VMEM((SC_ROWS, w), table.dtype)])
    def gather(table_hbm, idx_hbm, out_hbm, idx_v, buf):
        worker = lax.axis_index("core") * (SC_WORKERS // 2) + lax.axis_index("subcore")

        @pl.loop(0, per)
        def _(b):
            blk = worker * per + b
            pltpu.sync_copy(idx_hbm.at[pl.ds(blk, 1)], idx_v)
            pltpu.sync_copy(table_hbm.at[idx_v.at[0]], buf)
            pltpu.sync_copy(buf, out_hbm.at[pl.ds(blk * SC_ROWS, SC_ROWS)])

    return gather(table, idx.reshape(m // SC_ROWS, SC_ROWS))


def _sc_scatter(rows, dest):
    m, w = rows.shape
    per = m // (SC_WORKERS * SC_ROWS)
    assert per * SC_WORKERS * SC_ROWS == m

    @functools.partial(pl.kernel, out_type=jax.ShapeDtypeStruct((m, w), rows.dtype), mesh=_sc_mesh(),
                       scratch_types=[pltpu.VMEM((1, SC_ROWS), jnp.int32), pltpu.VMEM((SC_ROWS, w), rows.dtype)])
    def scatter(rows_hbm, dest_hbm, out_hbm, dest_v, buf):
        worker = lax.axis_index("core") * (SC_WORKERS // 2) + lax.axis_index("subcore")

        @pl.loop(0, per)
        def _(b):
            blk = worker * per + b
            pltpu.sync_copy(dest_hbm.at[pl.ds(blk, 1)], dest_v)
            pltpu.sync_copy(rows_hbm.at[pl.ds(blk * SC_ROWS, SC_ROWS)], buf)
            pltpu.sync_copy(buf, out_hbm.at[dest_v.at[0]])

    return scatter(rows, dest.reshape(m // SC_ROWS, SC_ROWS))


COMBINE_ROWS = 512


def _combine_kernel(tile_ref, start_ref, want_ref, flags_ref, x_ref, route_ref, z_ref, o_ref):
    s = pl.program_id(0)
    f32, bf16 = jnp.float32, jnp.bfloat16

    @pl.when(flags_ref[s] == 1)
    def _():
        o_ref[...] = x_ref[...]

    @pl.when(flags_ref[s] != 2)
    def _():
        first = route_ref[:, OFF_LANE:OFF_LANE + 1]
        last = first + route_ref[:, MULT_LANE:MULT_LANE + 1]
        row = start_ref[s] * SUBLANES + lax.broadcasted_iota(jnp.int32, (1, COMBINE_ROWS), 1)
        rowf = row.astype(f32)
        own = jnp.where((first <= rowf) & (rowf < last) & (row >= want_ref[s]), 1.0, 0.0).astype(bf16)
        z = z_ref[...]
        z1 = z.astype(bf16)
        r1 = z - z1.astype(f32)
        z2 = r1.astype(bf16)
        z3 = (r1 - z2.astype(f32)).astype(bf16)
        o_ref[...] += (jnp.dot(own, z1, preferred_element_type=f32) + jnp.dot(own, z2, preferred_element_type=f32)
                       + jnp.dot(own, z3, preferred_element_type=f32))


def _combine_steps(tile_lo, n_rows):
    n_tiles = tile_lo.shape[0] - 1
    n_steps_max = n_rows // COMBINE_ROWS + 2 * n_tiles + n_tiles // 16 + 1
    lo = (tile_lo[:-1] // SUBLANES) * SUBLANES
    per_tile = jnp.maximum((tile_lo[1:] - lo + COMBINE_ROWS - 1) // COMBINE_ROWS, 1)
    ends = jnp.cumsum(per_tile)
    s = jnp.arange(n_steps_max, dtype=jnp.int32)
    valid = s < ends[-1]
    tile = jnp.minimum(jnp.sum(ends[None, :] <= s[:, None], axis=1).astype(jnp.int32), n_tiles - 1)
    mine = tile[:, None] == jnp.arange(n_tiles, dtype=jnp.int32)[None, :]
    of_tile = lambda a: jnp.sum(jnp.where(mine, a[None, :], 0), axis=1)
    k = s - of_tile(ends - per_tile)
    want = of_tile(lo) + k * COMBINE_ROWS
    start = jnp.minimum(want, n_rows - COMBINE_ROWS)
    last_start = jnp.sum(jnp.where(s == ends[-1] - 1, start, 0))
    start = jnp.where(valid, start, last_start)
    flags = jnp.where(valid, (k == 0).astype(jnp.int32), 2)
    return tile, start // SUBLANES, want, flags, n_steps_max


def _combine(x, rows, z, tile_lo, tt):
    tile, start, want, flags, n_steps = _combine_steps(tile_lo, z.shape[0])
    return pl.pallas_call(
        _combine_kernel,
        grid_spec=pltpu.PrefetchScalarGridSpec(
            num_scalar_prefetch=4,
            grid=(n_steps,),
            in_specs=[pl.BlockSpec((tt, D_MODEL), lambda s, tile, *_: (tile[s], 0)),
                      pl.BlockSpec((tt, LANES), lambda s, tile, *_: (tile[s], D_MODEL // LANES)),
                      pl.BlockSpec((pl.Element(COMBINE_ROWS), pl.Element(D_MODEL)),
                                   lambda s, tile, start, *_: (start[s] * SUBLANES, 0))],
            out_specs=pl.BlockSpec((tt, D_MODEL), lambda s, tile, *_: (tile[s], 0)),
        ),
        out_shape=jax.ShapeDtypeStruct(x.shape, jnp.float32),
        compiler_params=pltpu.CompilerParams(dimension_semantics=("arbitrary",), vmem_limit_bytes=VMEM_LIMIT),
        name="expert_combine",
    )(tile, start, want, flags, x, rows, z)


def _final_norm_kernel(x_ref, g_ref, o_ref):
    o_ref[...] = _rms(x_ref[...], g_ref[...])


def _final_norm(x, g, tm):
    n = x.shape[0]
    return pl.pallas_call(
        _final_norm_kernel,
        grid=(n // tm,),
        in_specs=[pl.BlockSpec((tm, D_MODEL), lambda i: (i, 0)), pl.BlockSpec((1, D_MODEL), lambda i: (0, 0))],
        out_specs=pl.BlockSpec((tm, D_MODEL), lambda i: (i, 0)),
        out_shape=jax.ShapeDtypeStruct((n, D_MODEL), jnp.float32),
        name="final_norm",
    )(x, g)


def _forward(x_prompt, x_sample, g_attn, w_in, g_q_c, g_k_c, sink_b, g_out_a, g_out_b, g_out_c, w_out, g_ffn, w_router,
             w_gate, w_up, w_down, g_final, *, tm, tr, tf):
    bf16 = jnp.bfloat16
    shapes = (x_prompt.shape[:2], x_sample.shape[:2])
    assert all(l % tm == 0 and (b * l) % LANES == 0 for b, l in shapes)
    xs = [x_prompt.reshape(-1, D_MODEL), x_sample.reshape(-1, D_MODEL)]

    tables = _rope_tables(max(l for _, l in shapes))
    seg = jnp.asarray(np.kron(np.eye(LANES // HEAD_DIM), np.ones((HEAD_DIM, HEAD_DIM))), bf16)
    pb, pc, perm_out = _out_perms()
    w_in_p = w_in[:, :, _in_perm()].astype(bf16)
    w_out_p = w_out[:, perm_out, :].astype(bf16)
    sink_p = sink_b[:, np.asarray(B_Q_ORDER)]
    tile2 = lambda g: jnp.tile(g, (1, 2))[:, None, :]
    gq, gk = tile2(g_q_c), tile2(g_k_c)
    wr_t = jnp.swapaxes(w_router, 1, 2)
    tt = 256

    def layer(x, l, batch, seq):
        n = batch * seq
        cap = CAPACITY_FACTOR * n // N_EXPERTS
        assert cap % LANES == 0
        proj, a4, a16 = _in_projection(x, g_attn[l][None], w_in_p[l], tables, gq[l], gk[l], seg, seq, tm)
        oa = _mixer_a(proj, a4, a16, batch=batch, seq=seq).reshape(n, A_WIDTH)
        ob = _mixer_b(proj, sink_p[l], batch=batch, seq=seq).reshape(n, B_WIDTH)
        oc = _mixer_c(proj, batch=batch, seq=seq).reshape(n, C_WIDTH)
        x, rows, aff = _out_projection(oa, ob, oc, x, w_out_p[l], g_out_a[l][None], g_out_b[l][pb][None],
                                       g_out_c[l][pc][None], g_ffn[l][None], wr_t[l], tm)
        idx, stats = _select(aff, ((0, n // LANES, cap, 0),), cap)
        rows = _route_rows(aff, stats, rows, tm)
        xe = _sc_gather(rows, idx.reshape(-1)).reshape(N_EXPERTS, cap, ROW_WIDTH)
        ye, dest = _expert_ffn(xe, w_gate, w_up, w_down, l, min(tr, cap), tf)
        z = _sc_scatter(ye.reshape(-1, D_MODEL), dest.reshape(-1))
        first_slot = stats[::tt // LANES, N_EXPERTS, 0].astype(jnp.int32)
        tile_lo = jnp.concatenate([first_slot, jnp.full((1,), N_EXPERTS * cap, jnp.int32)])
        return _combine(x, rows, z, tile_lo, tt)

    for l in range(DEPTH):
        xs = [layer(x, l, b, s) for x, (b, s) in zip(xs, shapes)]
    return tuple(_final_norm(x, g_final[None], tm).reshape(b, s, D_MODEL) for x, (b, s) in zip(xs, shapes))


def kernel(x_prompt, x_sample, g_attn, w_in, g_q_c, g_k_c, sink_b, g_out_a, g_out_b, g_out_c, w_out, g_ffn, w_router,
           w_gate, w_up, w_down, g_final):
    return _forward(x_prompt, x_sample, g_attn, w_in, g_q_c, g_k_c, sink_b, g_out_a, g_out_b, g_out_c, w_out, g_ffn,
                    w_router, w_gate, w_up, w_down, g_final, tm=1024, tr=2048, tf=256)
```
